```python
import jax, jax.numpy as jnp
from jax import lax
import numpy as np

D_MODEL = 1024
BATCH = 2
SEQ = 8192
DEPTH = 2
DEC_BATCH = 32
DEC_SEQ = 1
PAST_LEN = 8192
PAGE_SIZE = 128

N_A_LAYERS = DEPTH // 2
N_B_LAYERS = DEPTH - N_A_LAYERS
RET_HEADS = 4
RET_DK = D_MODEL // RET_HEADS
RET_DV = 2 * RET_DK
RET_QK = RET_HEADS * RET_DK
RET_VD = RET_HEADS * RET_DV
RET_CHUNK = 128
ROPE_BASE = 10000.0
FOX_HEADS = 16
FOX_DH = D_MODEL // FOX_HEADS
FOX_HD = FOX_HEADS * FOX_DH
Q_BLOCK = 128
D_FF = 2816
CONV_W = 3
PLE_DIM = 256
NORM_EPS = 1e-6

kernel_name = "yoco_retention_forgetting_attention_step"


def rms_norm(x, g):
    xf = x.astype(jnp.float32)
    y = xf * lax.rsqrt(jnp.mean(xf * xf, axis=-1, keepdims=True) + NORM_EPS)
    return (y * g.astype(jnp.float32)).astype(x.dtype)


def rope(x, pos):
    half = x.shape[-1] // 2
    inv = ROPE_BASE ** (-jnp.arange(half, dtype=jnp.float32) / half)
    ang = pos.astype(jnp.float32)[:, None] * inv[None, :]
    cos = jnp.cos(ang)[None, :, None, :]
    sin = jnp.sin(ang)[None, :, None, :]
    xf = x.astype(jnp.float32)
    x1, x2 = xf[..., :half], xf[..., half:]
    return jnp.concatenate([x1 * cos - x2 * sin, x1 * sin + x2 * cos], axis=-1).astype(x.dtype)


def retention_log_gamma():
    return jnp.log1p(-jnp.exp2(-5.0 - jnp.arange(RET_HEADS, dtype=jnp.float32)))


def retention_chunk(S, q, k, v, log_gamma):
    C = q.shape[1]
    n = jnp.arange(C, dtype=jnp.float32)
    rel = n[:, None] - n[None, :]
    decay = jnp.where(rel[None] >= 0, jnp.exp(jnp.maximum(rel, 0.0)[None] * log_gamma[:, None, None]), 0.0)
    scores = jnp.einsum('bihd,bjhd->bhij', q, k) * decay[None]
    inner = jnp.einsum('bhij,bjhe->bihe', scores, v)
    q_decay = jnp.exp((n[:, None] + 1.0) * log_gamma[None, :])
    cross = jnp.einsum('bihd,bhde->bihe', q, S) * q_decay[None, :, :, None]
    k_decay = jnp.exp((C - 1.0 - n)[:, None] * log_gamma[None, :])
    S_new = S * jnp.exp(C * log_gamma)[None, :, None, None] + jnp.einsum('bjhd,bjhe->bhde', k * k_decay[None, :, :, None], v)
    return S_new, inner + cross


def retention_seq(q, k, v, S0):
    B, T = q.shape[0], q.shape[1]
    C = RET_CHUNK if T % RET_CHUNK == 0 else T
    nc = T // C
    log_gamma = retention_log_gamma()
    qc = jnp.moveaxis(q.reshape(B, nc, C, RET_HEADS, RET_DK), 1, 0)
    kc = jnp.moveaxis(k.reshape(B, nc, C, RET_HEADS, RET_DK), 1, 0)
    vc = jnp.moveaxis(v.reshape(B, nc, C, RET_HEADS, RET_DV), 1, 0)
    S_fin, out = lax.scan(lambda S, xs: retention_chunk(S, xs[0], xs[1], xs[2], log_gamma), S0, (qc, kc, vc))
    return S_fin, jnp.moveaxis(out, 0, 1).reshape(B, T, RET_HEADS, RET_DV)


def retention_mixer(x, pos, S0, w_in, w_out):
    B, T, _ = x.shape
    u = x @ w_in
    q = u[..., :RET_QK].reshape(B, T, RET_HEADS, RET_DK)
    k = u[..., RET_QK:2 * RET_QK].reshape(B, T, RET_HEADS, RET_DK)
    v = u[..., 2 * RET_QK:2 * RET_QK + RET_VD].reshape(B, T, RET_HEADS, RET_DV)
    g = u[..., 2 * RET_QK + RET_VD:]
    q = rope(q, pos).astype(jnp.float32)
    k = rope(k, pos).astype(jnp.float32) * (RET_DK ** -0.5)
    S_fin, o = retention_seq(q, k, v.astype(jnp.float32), S0.astype(jnp.float32))
    o = o * lax.rsqrt(jnp.mean(o * o, axis=-1, keepdims=True) + NORM_EPS)
    o = o.reshape(B, T, RET_VD).astype(x.dtype) * jax.nn.silu(g)
    return o @ w_out, S_fin


def shared_kv(h, norm_kv, w_kvf, b_f):
    B, T, _ = h.shape
    u = rms_norm(h, norm_kv) @ w_kvf
    k = u[..., :FOX_HD].reshape(B, T, FOX_HEADS, FOX_DH)
    v = u[..., FOX_HD:2 * FOX_HD].reshape(B, T, FOX_HEADS, FOX_DH)
    logf = jax.nn.log_sigmoid((u[..., 2 * FOX_HD:] + b_f).astype(jnp.float32))
    return k, v, logf


def fox_block(q, cq, qpos, k, v, ck_t, kpos):
    s = jnp.einsum('bqhd,bkhd->bhqk', q, k).astype(jnp.float32) * (FOX_DH ** -0.5)
    s = s + (jnp.swapaxes(cq, 1, 2)[:, :, :, None] - ck_t[:, :, None, :])
    s = jnp.where((kpos[None, :] <= qpos[:, None])[None, None], s, -jnp.inf)
    p = jax.nn.softmax(s, axis=-1)
    return jnp.einsum('bhqk,bkhd->bqhd', p.astype(v.dtype), v)


def fox_mixer(x, k_all, v_all, c_all, w_q, w_o):
    B, T, _ = x.shape
    L = k_all.shape[1]
    q = (x @ w_q).reshape(B, T, FOX_HEADS, FOX_DH)
    cq = c_all[:, L - T:]
    qpos = (L - T) + jnp.arange(T)
    kpos = jnp.arange(L)
    nb = T // Q_BLOCK if T % Q_BLOCK == 0 else 1
    bs = T // nb
    qb = jnp.swapaxes(q.reshape(B, nb, bs, FOX_HEADS, FOX_DH), 0, 1)
    cqb = jnp.swapaxes(cq.reshape(B, nb, bs, FOX_HEADS), 0, 1)
    qposb = qpos.reshape(nb, bs)
    ck_t = jnp.swapaxes(c_all, 1, 2)
    out = lax.map(lambda a: fox_block(a[0], a[1], a[2], k_all, v_all, ck_t, kpos), (qb, cqb, qposb))
    out = jnp.swapaxes(out, 0, 1).reshape(B, T, FOX_HD)
    return out @ w_o


def conv_ffn(x, buf, w_up, w_dw, b_dw, w_down):
    T = x.shape[1]
    u = x @ w_up
    a, b = u[..., :D_FF], u[..., D_FF:]
    a_ext = jnp.concatenate([buf.astype(a.dtype), a], axis=1)
    conv = b_dw + a_ext[:, 0:T] * w_dw[0]
    for j in range(1, CONV_W):
        conv = conv + a_ext[:, j:j + T] * w_dw[j]
    h = jax.nn.gelu(conv) * b
    return h @ w_down, a_ext[:, T:]


def run_group(x, p, pos, ret_states, conv_bufs, past, w):
    new_ret, new_conv = [], []
    k_new = v_new = lf_new = k_all = v_all = c_all = None
    for i in range(DEPTH):
        hn = rms_norm(x, w['norm_pre_mix'][i])
        if i < N_A_LAYERS:
            m, S = retention_mixer(hn, pos, ret_states[i], w['w_in_ret'][i], w['w_out_ret'][i])
            new_ret.append(S)
        else:
            j = i - N_A_LAYERS
            m = fox_mixer(hn, k_all, v_all, c_all, w['w_q_fox'][j], w['w_out_fox'][j])
        x = x + rms_norm(m, w['norm_post_mix'][i])
        f, buf = conv_ffn(rms_norm(x, w['norm_pre_ffn'][i]), conv_bufs[i], w['w_up'][i], w['w_dw'][i], w['b_dw'][i], w['w_down'][i])
        new_conv.append(buf)
        x = x + rms_norm(f, w['norm_post_ffn'][i])
        x = x + (p[i].astype(x.dtype) @ w['w_ple'][i]) * jax.nn.sigmoid(x @ w['w_ple_gate'][i])
        if i == N_A_LAYERS - 1:
            k_new, v_new, lf_new = shared_kv(x, w['norm_kv'], w['w_kvf'], w['b_f'])
            if past is None:
                k_all, v_all, lf_all = k_new, v_new, lf_new
            else:
                k_all = jnp.concatenate([past[0].astype(k_new.dtype), k_new], axis=1)
                v_all = jnp.concatenate([past[1].astype(v_new.dtype), v_new], axis=1)
                lf_all = jnp.concatenate([past[2].astype(jnp.float32), lf_new], axis=1)
            c_all = jnp.cumsum(lf_all, axis=1)
    return x, jnp.stack(new_ret), jnp.stack(new_conv), k_new, v_new, lf_new


def setup_inputs(seed: int = 0) -> dict:
    key = jax.random.key(seed)
    ks = jax.random.split(key, 32)
    n_pages = PAST_LEN // PAGE_SIZE
    n_used = DEC_BATCH * n_pages
    n_pool = n_used + (n_used + 3) // 4
    f32 = jnp.float32

    def nrm(k, shape, scale):
        return jax.random.normal(k, shape, f32) * scale

    page_table = jax.random.permutation(ks[0], n_pool)[:n_used].reshape(DEC_BATCH, n_pages).astype(jnp.int32)
    return {
        'x_prompt': nrm(ks[1], (BATCH, SEQ, D_MODEL), 1.0),
        'x_sample': nrm(ks[2], (DEC_BATCH, DEC_SEQ, D_MODEL), 1.0),
        'state_ret': nrm(ks[3], (N_A_LAYERS, DEC_BATCH, RET_HEADS, RET_DK, RET_DV), 0.1),
        'state_conv': nrm(ks[4], (DEPTH, DEC_BATCH, CONV_W - 1, D_FF), 1.0),
        'cache_k': nrm(ks[5], (n_pool, PAGE_SIZE, FOX_HEADS, FOX_DH), 1.0),
        'cache_v': nrm(ks[6], (n_pool, PAGE_SIZE, FOX_HEADS, FOX_DH), 1.0),
        'cache_logf': jax.nn.log_sigmoid(2.0 + nrm(ks[7], (n_pool, PAGE_SIZE, FOX_HEADS), 1.0)),
        'page_table': page_table,
        'p_prompt': nrm(ks[8], (DEPTH, BATCH, SEQ, PLE_DIM), 1.0),
        'p_sample': nrm(ks[9], (DEPTH, DEC_BATCH, DEC_SEQ, PLE_DIM), 1.0),
        'norm_pre_mix': 1.0 + nrm(ks[10], (DEPTH, D_MODEL), 0.05),
        'norm_post_mix': 1.0 + nrm(ks[11], (DEPTH, D_MODEL), 0.05),
        'norm_pre_ffn': 1.0 + nrm(ks[12], (DEPTH, D_MODEL), 0.05),
        'norm_post_ffn': 1.0 + nrm(ks[13], (DEPTH, D_MODEL), 0.05),
        'w_in_ret': nrm(ks[14], (N_A_LAYERS, D_MODEL, 2 * RET_QK + 2 * RET_VD), D_MODEL ** -0.5),
        'w_out_ret': nrm(ks[15], (N_A_LAYERS, RET_VD, D_MODEL), RET_VD ** -0.5),
        'norm_kv': 1.0 + nrm(ks[16], (D_MODEL,), 0.05),
        'w_kvf': nrm(ks[17], (D_MODEL, 2 * FOX_HD + FOX_HEADS), D_MODEL ** -0.5),
        'b_f': 2.0 + nrm(ks[18], (FOX_HEADS,), 0.5),
        'w_q_fox': nrm(ks[19], (N_B_LAYERS, D_MODEL, FOX_HD), D_MODEL ** -0.5),
        'w_out_fox': nrm(ks[20], (N_B_LAYERS, FOX_HD, D_MODEL), FOX_HD ** -0.5),
        'w_up': nrm(ks[21], (DEPTH, D_MODEL, 2 * D_FF), D_MODEL ** -0.5),
        'w_dw': nrm(ks[22], (DEPTH, CONV_W, D_FF), CONV_W ** -0.5),
        'b_dw': nrm(ks[23], (DEPTH, D_FF), 0.02),
        'w_down': nrm(ks[24], (DEPTH, D_FF, D_MODEL), D_FF ** -0.5),
        'w_ple': nrm(ks[25], (DEPTH, PLE_DIM, D_MODEL), PLE_DIM ** -0.5),
        'w_ple_gate': nrm(ks[26], (DEPTH, D_MODEL, D_MODEL), D_MODEL ** -0.5),
    }


def reference(x_prompt, x_sample, state_ret, state_conv, cache_k, cache_v, cache_logf, page_table, p_prompt, p_sample,
              norm_pre_mix, norm_post_mix, norm_pre_ffn, norm_post_ffn, w_in_ret, w_out_ret, norm_kv, w_kvf, b_f,
              w_q_fox, w_out_fox, w_up, w_dw, b_dw, w_down, w_ple, w_ple_gate):
    w = {'norm_pre_mix': norm_pre_mix, 'norm_post_mix': norm_post_mix, 'norm_pre_ffn': norm_pre_ffn,
         'norm_post_ffn': norm_post_ffn, 'w_in_ret': w_in_ret, 'w_out_ret': w_out_ret, 'norm_kv': norm_kv,
         'w_kvf': w_kvf, 'b_f': b_f, 'w_q_fox': w_q_fox, 'w_out_fox': w_out_fox, 'w_up': w_up, 'w_dw': w_dw,
         'b_dw': b_dw, 'w_down': w_down, 'w_ple': w_ple, 'w_ple_gate': w_ple_gate}
    bp, tp = x_prompt.shape[0], x_prompt.shape[1]
    ret0 = jnp.zeros((N_A_LAYERS, bp, RET_HEADS, RET_DK, RET_DV), jnp.float32)
    conv0 = jnp.zeros((DEPTH, bp, CONV_W - 1, D_FF), x_prompt.dtype)
    y_p, ret_p, conv_p, k_p, v_p, lf_p = run_group(x_prompt, p_prompt, jnp.arange(tp), ret0, conv0, None, w)
    nb, ts = x_sample.shape[0], x_sample.shape[1]
    k_past = cache_k[page_table].reshape(nb, -1, FOX_HEADS, FOX_DH)
    v_past = cache_v[page_table].reshape(nb, -1, FOX_HEADS, FOX_DH)
    lf_past = cache_logf[page_table].reshape(nb, -1, FOX_HEADS)
    past_len = k_past.shape[1]
    y_s, ret_s, conv_s, k_s, v_s, lf_s = run_group(x_sample, p_sample, past_len + jnp.arange(ts), state_ret, state_conv,
                                                   (k_past, v_past, lf_past), w)
    return (y_p, y_s, ret_p.astype(state_ret.dtype), ret_s.astype(state_ret.dtype),
            conv_p.astype(state_conv.dtype), conv_s.astype(state_conv.dtype),
            k_p.astype(cache_k.dtype), k_s.astype(cache_k.dtype), v_p.astype(cache_v.dtype), v_s.astype(cache_v.dtype),
            lf_p.astype(cache_logf.dtype), lf_s.astype(cache_logf.dtype))
```

```python
import functools
import math

import jax
import jax.numpy as jnp
from jax import lax
from jax.experimental import pallas as pl
from jax.experimental.pallas import tpu as pltpu

F32 = jnp.float32
BF16 = jnp.bfloat16

D_MODEL = 1024
RET_HEADS = 4
RET_DK = 256
RET_DV = 512
RET_QK = RET_HEADS * RET_DK
RET_VD = RET_HEADS * RET_DV
ROPE_BASE = 10000.0
ROPE_HALF = RET_DK // 2
FOX_HEADS = 16
FOX_DH = 64
FOX_HD = FOX_HEADS * FOX_DH
D_FF = 2816
PLE_DIM = 256
NORM_EPS = 1e-6

LANES = 128
SUBLANES = 8
VMEM_LIMIT = 56 * 1024 * 1024

ROW_BLOCK = 512
RET_CHUNK = 256
FFN_CHUNK = 256
ATT_BLOCK = 512
PAGES_PER_STEP = 8


def _cparams(sem):
    return pltpu.CompilerParams(dimension_semantics=sem, vmem_limit_bytes=VMEM_LIMIT)


def _resident(shape):
    return pl.BlockSpec(shape, lambda *_: (0,) * len(shape), pipeline_mode=pl.Buffered(1))


def _dot(a, b):
    return jnp.dot(a, b, preferred_element_type=F32)


def _dot_nt(a, b):
    return lax.dot_general(a, b, (((1,), (1,)), ((), ())), preferred_element_type=F32)


def _dot_tn(a, b):
    return lax.dot_general(a, b, (((0,), (0,)), ((), ())), preferred_element_type=F32)


def _rms(x, g):
    return x * lax.rsqrt(jnp.mean(x * x, axis=-1, keepdims=True) + NORM_EPS) * g


def _sigmoid(x):
    return 1.0 / (1.0 + jnp.exp(-x))


def _gelu_tanh(x):
    return x * (0.5 * (1.0 + jnp.tanh(math.sqrt(2.0 / math.pi) * (x + 0.044715 * (x * x * x)))))


def _log_sigmoid(x):
    z = -x
    return -(jnp.maximum(z, 0.0) + jnp.log1p(jnp.exp(-jnp.abs(z))))


def _split3(x):
    hi = x.astype(BF16)
    r1 = x - hi.astype(F32)
    mid = r1.astype(BF16)
    lo = (r1 - mid.astype(F32)).astype(BF16)
    return hi, mid, lo


def _rope_kernel(cos_ref, sin_ref, *, base_pos):
    n = cos_ref.shape[0]
    i = pl.program_id(0)
    lane = lax.broadcasted_iota(jnp.int32, (1, ROPE_HALF), 1).astype(F32)
    inv = jnp.power(jnp.full((1, ROPE_HALF), ROPE_BASE, F32), -(lane / ROPE_HALF))
    pos = (lax.broadcasted_iota(jnp.int32, (n, ROPE_HALF), 0) + (i * n + base_pos)).astype(F32)
    ang = pos * inv
    cos_ref[...] = jnp.cos(ang)
    sin_ref[...] = jnp.sin(ang)


def _rope_table(n_rows, base_pos):
    bn = min(n_rows, 1024)
    spec = pl.BlockSpec((bn, ROPE_HALF), lambda i: (i, 0))
    return pl.pallas_call(
        functools.partial(_rope_kernel, base_pos=base_pos),
        grid=(n_rows // bn,),
        out_specs=[spec, spec],
        out_shape=[jax.ShapeDtypeStruct((n_rows, ROPE_HALF), F32)] * 2,
        compiler_params=_cparams(("arbitrary",)),
        name="rope_table",
    )()


def _ret_in_kernel(x_ref, g_ref, w_ref, cos_ref, sin_ref, q_ref, k_ref, v_ref, gate_ref):
    xn = _rms(x_ref[...], g_ref[...]).astype(BF16)
    cos = cos_ref[...]
    sin = sin_ref[...]
    for out_ref, base, scale in ((q_ref, 0, 1.0), (k_ref, RET_QK, RET_DK ** -0.5)):
        for h in range(RET_HEADS):
            lo = h * RET_DK
            u = _dot(xn, w_ref[:, base + lo:base + lo + RET_DK])
            x1, x2 = u[:, :ROPE_HALF], u[:, ROPE_HALF:]
            out_ref[:, lo:lo + ROPE_HALF] = ((x1 * cos - x2 * sin) * scale).astype(out_ref.dtype)
            out_ref[:, lo + ROPE_HALF:lo + RET_DK] = ((x1 * sin + x2 * cos) * scale).astype(out_ref.dtype)
    for h in range(RET_HEADS):
        lo = h * RET_DV
        v_ref[:, lo:lo + RET_DV] = _dot(xn, w_ref[:, 2 * RET_QK + lo:2 * RET_QK + lo + RET_DV]).astype(v_ref.dtype)
        gate_ref[:, lo:lo + RET_DV] = _dot(xn, w_ref[:, 2 * RET_QK + RET_VD + lo:2 * RET_QK + RET_VD + lo + RET_DV])


def _ret_in(x, g, w_bf, cos, sin, bm, qkv_dtype):
    rows = x.shape[0]
    n_tab = cos.shape[0] // bm
    row = lambda i: (i, 0)
    tab = pl.BlockSpec((bm, ROPE_HALF), lambda i: (i % n_tab, 0))
    return pl.pallas_call(
        _ret_in_kernel,
        grid=(rows // bm,),
        in_specs=[pl.BlockSpec((bm, D_MODEL), row), _resident((1, D_MODEL)), _resident(w_bf.shape), tab, tab],
        out_specs=[pl.BlockSpec((bm, RET_QK), row), pl.BlockSpec((bm, RET_QK), row),
                   pl.BlockSpec((bm, RET_VD), row), pl.BlockSpec((bm, RET_VD), row)],
        out_shape=[jax.ShapeDtypeStruct((rows, RET_QK), qkv_dtype), jax.ShapeDtypeStruct((rows, RET_QK), qkv_dtype),
                   jax.ShapeDtypeStruct((rows, RET_VD), qkv_dtype), jax.ShapeDtypeStruct((rows, RET_VD), F32)],
        compiler_params=_cparams(("arbitrary",)),
        name="ret_in",
    )(x, g, w_bf, cos, sin)


def _ret_chunk_kernel(lg_ref, q_ref, k_ref, v_ref, gate_ref, o_ref, s_ref):
    h = pl.program_id(1)
    c = pl.program_id(2)
    lg = lg_ref[h]
    C = q_ref.shape[0]

    @pl.when(c == 0)
    def _():
        s_ref[...] = jnp.zeros_like(s_ref)

    q = q_ref[...]
    k = k_ref[...]
    v = v_ref[...]
    rel = lax.broadcasted_iota(jnp.int32, (C, C), 0) - lax.broadcasted_iota(jnp.int32, (C, C), 1)
    decay = jnp.where(rel >= 0, jnp.exp(jnp.maximum(rel, 0).astype(F32) * lg), 0.0)
    scores = _dot_nt(q, k) * decay
    inner = _dot(scores.astype(BF16), v)
    n = lax.broadcasted_iota(jnp.int32, (C, 1), 0).astype(F32)
    state = s_ref[0, 0]
    cross = _dot(q, state.astype(BF16)) * jnp.exp((n + 1.0) * lg)
    kd = (k.astype(F32) * jnp.exp((C - 1.0 - n) * lg)).astype(BF16)
    chunk_decay = jnp.exp(jnp.full((1, RET_DV), float(C), F32) * lg)
    s_ref[0, 0] = state * chunk_decay + _dot_tn(kd, v)
    o = inner + cross
    o = o * lax.rsqrt(jnp.mean(o * o, axis=-1, keepdims=True) + NORM_EPS)
    gate = gate_ref[...]
    o_ref[...] = (o * (gate * _sigmoid(gate))).astype(o_ref.dtype)


def _retention_prompt(q, k, v, gate, log_gamma, batch, seq):
    C = min(RET_CHUNK, seq)
    nc = seq // C
    blk = lambda b, h, c, lg: (b * nc + c, h)
    return pl.pallas_call(
        _ret_chunk_kernel,
        grid_spec=pltpu.PrefetchScalarGridSpec(
            num_scalar_prefetch=1,
            grid=(batch, RET_HEADS, nc),
            in_specs=[pl.BlockSpec((C, RET_DK), blk), pl.BlockSpec((C, RET_DK), blk),
                      pl.BlockSpec((C, RET_DV), blk), pl.BlockSpec((C, RET_DV), blk)],
            out_specs=[pl.BlockSpec((C, RET_DV), blk),
                       pl.BlockSpec((1, 1, RET_DK, RET_DV), lambda b, h, c, lg: (b, h, 0, 0))],
        ),
        out_shape=[jax.ShapeDtypeStruct((batch * seq, RET_VD), BF16),
                   jax.ShapeDtypeStruct((batch, RET_HEADS, RET_DK, RET_DV), F32)],
        compiler_params=_cparams(("arbitrary", "arbitrary", "arbitrary")),
        name="retention_chunks",
    )(log_gamma, q, k, v, gate)


def _ret_step_kernel(lg_ref, s_ref, qc_ref, kc_ref, v_ref, gate_ref, o_ref, so_ref):
    for h in range(RET_HEADS):
        gamma = jnp.exp(jnp.full((1, RET_DV), 1.0, F32) * lg_ref[h])
        state = s_ref[0, h]
        qc = qc_ref[0, h]
        kc = kc_ref[0, h]
        vr = v_ref[0, :, h * RET_DV:(h + 1) * RET_DV]
        qk = jnp.sum(qc * kc, axis=0, keepdims=True)
        qs = jnp.sum(qc * state, axis=0, keepdims=True)
        o = qk * vr + qs * gamma
        so_ref[0, h] = state * gamma + kc * vr
        o = o * lax.rsqrt(jnp.mean(o * o, axis=-1, keepdims=True) + NORM_EPS)
        gate = gate_ref[0, :, h * RET_DV:(h + 1) * RET_DV]
        o_ref[0, :, h * RET_DV:(h + 1) * RET_DV] = (o * (gate * _sigmoid(gate))).astype(o_ref.dtype)


def _retention_step(state, q, k, v, gate, log_gamma):
    nb = state.shape[0]
    qc = q.reshape(nb, RET_HEADS, RET_DK, 1)
    kc = k.reshape(nb, RET_HEADS, RET_DK, 1)
    st = pl.BlockSpec((1, RET_HEADS, RET_DK, RET_DV), lambda b, lg: (b, 0, 0, 0))
    col = pl.BlockSpec((1, RET_HEADS, RET_DK, 1), lambda b, lg: (b, 0, 0, 0))
    rowv = pl.BlockSpec((1, 1, RET_VD), lambda b, lg: (b, 0, 0))
    o, s_new = pl.pallas_call(
        _ret_step_kernel,
        grid_spec=pltpu.PrefetchScalarGridSpec(
            num_scalar_prefetch=1, grid=(nb,),
            in_specs=[st, col, col, rowv, rowv], out_specs=[rowv, st]),
        out_shape=[jax.ShapeDtypeStruct((nb, 1, RET_VD), BF16), jax.ShapeDtypeStruct(state.shape, F32)],
        compiler_params=_cparams(("arbitrary",)),
        name="retention_step",
    )(log_gamma, state, qc, kc, v.reshape(nb, 1, RET_VD), gate.reshape(nb, 1, RET_VD))
    return o.reshape(nb, RET_VD), s_new


def _mix_out_kernel(o_ref, w_ref, x_ref, g_ref, y_ref):
    y_ref[...] = x_ref[...] + _rms(_dot(o_ref[...], w_ref[...]), g_ref[...])


def _mix_out(o_bf, w_bf, x, g, bm):
    rows, kdim = o_bf.shape
    row = lambda i: (i, 0)
    return pl.pallas_call(
        _mix_out_kernel,
        grid=(rows // bm,),
        in_specs=[pl.BlockSpec((bm, kdim), row), _resident(w_bf.shape), pl.BlockSpec((bm, D_MODEL), row),
                  _resident((1, D_MODEL))],
        out_specs=pl.BlockSpec((bm, D_MODEL), row),
        out_shape=jax.ShapeDtypeStruct((rows, D_MODEL), F32),
        compiler_params=_cparams(("arbitrary",)),
        name="mix_out",
    )(o_bf, w_bf, x, g)


def _ffn_kernel(*refs, seq_mode, blocks_per_seq):
    if seq_mode:
        (x_ref, gpre_ref, wup_ref, wdw_ref, bdw_ref, wdown_ref, gpost_ref, p_ref, wple_ref, wgate_ref,
         y_ref, tail_ref, h_sc, carry_sc) = refs
    else:
        (x_ref, gpre_ref, wup_ref, wdw_ref, bdw_ref, wdown_ref, gpost_ref, p_ref, wple_ref, wgate_ref,
         prev2_ref, prev1_ref, y_ref, a_ref, h_sc) = refs
    x = x_ref[...]
    bm = x.shape[0]
    xn = _rms(x, gpre_ref[...]).astype(BF16)
    if seq_mode:
        @pl.when(pl.program_id(0) % blocks_per_seq == 0)
        def _():
            carry_sc[...] = jnp.zeros_like(carry_sc)
        row = lax.broadcasted_iota(jnp.int32, (bm, FFN_CHUNK), 0)
    for c in range(D_FF // FFN_CHUNK):
        sl = slice(c * FFN_CHUNK, (c + 1) * FFN_CHUNK)
        a = _dot(xn, wup_ref[:, sl])
        b = _dot(xn, wup_ref[:, D_FF + c * FFN_CHUNK:D_FF + (c + 1) * FFN_CHUNK])
        if seq_mode:
            c2 = carry_sc[SUBLANES - 2:SUBLANES - 1, sl]
            c1 = carry_sc[SUBLANES - 1:SUBLANES, sl]
            a1 = jnp.where(row == 0, c1, pltpu.roll(a, 1, 0))
            a2 = jnp.where(row == 0, c2, jnp.where(row == 1, c1, pltpu.roll(a, 2, 0)))
            last = a[bm - SUBLANES:bm, :]
            carry_sc[:, sl] = last
            tail_ref[0, :, sl] = last
        else:
            a1 = prev1_ref[:, sl]
            a2 = prev2_ref[:, sl]
            a_ref[:, sl] = a
        conv = bdw_ref[:, sl] + a2 * wdw_ref[0:1, sl]
        conv = conv + a1 * wdw_ref[1:2, sl]
        conv = conv + a * wdw_ref[2:3, sl]
        h_sc[:, sl] = (_gelu_tanh(conv) * b).astype(BF16)
    f = _dot(h_sc[...], wdown_ref[...])
    x2 = x + _rms(f, gpost_ref[...])
    ple = _dot(p_ref[...].astype(BF16), wple_ref[...])
    gate = _dot(x2.astype(BF16), wgate_ref[...])
    y_ref[...] = x2 + ple * _sigmoid(gate)


def _ffn(x, gpre, wup_bf, wdw, bdw, wdown_bf, gpost, p, wple_bf, wgate_bf, bm, seq_len=None, prev=None):
    rows = x.shape[0]
    row = lambda i: (i, 0)
    seq_mode = prev is None
    in_specs = [pl.BlockSpec((bm, D_MODEL), row), _resident((1, D_MODEL)), _resident(wup_bf.shape),
                _resident(wdw.shape), _resident((1, D_FF)), _resident(wdown_bf.shape), _resident((1, D_MODEL)),
                pl.BlockSpec((bm, PLE_DIM), row), _resident(wple_bf.shape), _resident(wgate_bf.shape)]
    args = [x, gpre, wup_bf, wdw, bdw, wdown_bf, gpost, p, wple_bf, wgate_bf]
    scratch = [pltpu.VMEM((bm, D_FF), BF16)]
    if seq_mode:
        bps = seq_len // bm
        out_specs = [pl.BlockSpec((bm, D_MODEL), row), pl.BlockSpec((1, SUBLANES, D_FF), lambda i: (i // bps, 0, 0))]
        out_shape = [jax.ShapeDtypeStruct((rows, D_MODEL), F32),
                     jax.ShapeDtypeStruct((rows // seq_len, SUBLANES, D_FF), F32)]
        scratch.append(pltpu.VMEM((SUBLANES, D_FF), F32))
    else:
        bps = 1
        in_specs += [pl.BlockSpec((bm, D_FF), row), pl.BlockSpec((bm, D_FF), row)]
        args += list(prev)
        out_specs = [pl.BlockSpec((bm, D_MODEL), row), pl.BlockSpec((bm, D_FF), row)]
        out_shape = [jax.ShapeDtypeStruct((rows, D_MODEL), F32), jax.ShapeDtypeStruct((rows, D_FF), F32)]
    return pl.pallas_call(
        functools.partial(_ffn_kernel, seq_mode=seq_mode, blocks_per_seq=bps),
        grid=(rows // bm,),
        in_specs=in_specs, out_specs=out_specs, out_shape=out_shape, scratch_shapes=scratch,
        compiler_params=_cparams(("arbitrary",)),
        name="conv_ffn",
    )(*args)


def _kvq_kernel(x_ref, gkv_ref, wkv_ref, wf_ref, bf_ref, gq_ref, wq_ref,
                k_ref, v_ref, lf_ref, kb_ref, vb_ref, qb_ref):
    x = x_ref[...]
    xn = _rms(x, gkv_ref[...]).astype(BF16)
    k = _dot(xn, wkv_ref[:, :FOX_HD])
    v = _dot(xn, wkv_ref[:, FOX_HD:])
    k_ref[...] = k
    v_ref[...] = v
    kb_ref[...] = k.astype(BF16)
    vb_ref[...] = v.astype(BF16)
    f = _dot(xn, wf_ref[...]) + bf_ref[...]
    lf_ref[...] = _log_sigmoid(f)[:, :FOX_HEADS]
    xq = _rms(x, gq_ref[...]).astype(BF16)
    qb_ref[...] = (_dot(xq, wq_ref[...]) * FOX_DH ** -0.5).astype(BF16)


def _kvq(x, gkv, wkv_bf, wf_bf, bf_pad, gq, wq_bf, bm):
    rows = x.shape[0]
    row = lambda i: (i, 0)
    wide = pl.BlockSpec((bm, FOX_HD), row)
    return pl.pallas_call(
        _kvq_kernel,
        grid=(rows // bm,),
        in_specs=[pl.BlockSpec((bm, D_MODEL), row), _resident((1, D_MODEL)), _resident(wkv_bf.shape),
                  _resident(wf_bf.shape), _resident((1, LANES)), _resident((1, D_MODEL)), _resident(wq_bf.shape)],
        out_specs=[wide, wide, pl.BlockSpec((bm, FOX_HEADS), row), wide, wide, wide],
        out_shape=[jax.ShapeDtypeStruct((rows, FOX_HD), F32), jax.ShapeDtypeStruct((rows, FOX_HD), F32),
                   jax.ShapeDtypeStruct((rows, FOX_HEADS), F32), jax.ShapeDtypeStruct((rows, FOX_HD), BF16),
                   jax.ShapeDtypeStruct((rows, FOX_HD), BF16), jax.ShapeDtypeStruct((rows, FOX_HD), BF16)],
        compiler_params=_cparams(("arbitrary",)),
        name="kv_q_proj",
    )(x, gkv, wkv_bf, wf_bf, bf_pad, gq, wq_bf)


def _cumsum_kernel(lf_ref, c_ref, carry_sc):
    bl = lf_ref.shape[0]

    @pl.when(pl.program_id(1) == 0)
    def _():
        carry_sc[...] = jnp.zeros_like(carry_sc)

    lf = lf_ref[...]
    tri = (lax.broadcasted_iota(jnp.int32, (bl, bl), 1) <= lax.broadcasted_iota(jnp.int32, (bl, bl), 0))
    tri = jnp.where(tri, 1.0, 0.0).astype(BF16)
    hi, mid, lo = _split3(lf)
    c = (_dot(tri, lo) + _dot(tri, mid)) + _dot(tri, hi) + carry_sc[...]
    c_ref[...] = c
    carry_sc[...] = c[bl - 1:bl, :]


def _cumsum_time(lf, batch, seq):
    bl = min(512, seq)
    nb = seq // bl
    blk = pl.BlockSpec((bl, FOX_HEADS), lambda b, i: (b * nb + i, 0))
    return pl.pallas_call(
        _cumsum_kernel,
        grid=(batch, nb),
        in_specs=[blk], out_specs=blk,
        out_shape=jax.ShapeDtypeStruct(lf.shape, F32),
        scratch_shapes=[pltpu.VMEM((1, FOX_HEADS), F32)],
        compiler_params=_cparams(("arbitrary", "arbitrary")),
        name="logf_cumsum",
    )(lf)


def _fox_prompt_kernel(q_ref, k_ref, v_ref, cq_ref, ck_ref, o_ref, m_sc, l_sc, acc_sc):
    i = pl.program_id(2)
    bq = q_ref.shape[0]
    q = q_ref[...]
    low = lax.broadcasted_iota(jnp.int32, (1, 2 * FOX_DH), 1) < FOX_DH
    causal = (lax.broadcasted_iota(jnp.int32, (bq, bq), 1) <= lax.broadcasted_iota(jnp.int32, (bq, bq), 0))

    for hl in range(2):
        qm = jnp.where(low if hl == 0 else jnp.logical_not(low), q, jnp.zeros_like(q))
        cq = cq_ref[0][:, hl:hl + 1]

        def block(j, masked):
            start = pl.multiple_of(j * bq, bq)
            kb = k_ref[pl.ds(start, bq), :]
            vb = v_ref[pl.ds(start, bq), :]
            s = _dot_nt(qm, kb) + (cq - ck_ref[0, 0, hl:hl + 1, pl.ds(start, bq)])
            if masked:
                s = jnp.where(causal, s, -jnp.inf)
            m_prev = m_sc[hl]
            m_new = jnp.maximum(m_prev, jnp.max(s, axis=-1, keepdims=True))
            alpha = jnp.exp(m_prev - m_new)
            p = jnp.exp(s - m_new)
            l_sc[hl] = alpha * l_sc[hl] + jnp.sum(p, axis=-1, keepdims=True)
            acc_sc[hl] = alpha * acc_sc[hl] + _dot(p.astype(BF16), vb)
            m_sc[hl] = m_new

        m_sc[hl] = jnp.full(m_sc.shape[1:], -jnp.inf, F32)
        l_sc[hl] = jnp.zeros(l_sc.shape[1:], F32)
        acc_sc[hl] = jnp.zeros(acc_sc.shape[1:], F32)
        block(i, True)

        def body(j, carry):
            block(j, False)
            return carry

        lax.fori_loop(0, i, body, 0)

    o0 = acc_sc[0] / l_sc[0]
    o1 = acc_sc[1] / l_sc[1]
    o_ref[...] = jnp.where(low, o0, o1).astype(o_ref.dtype)


def _fox_prompt(q_bf, k_bf, v_bf, c, batch, seq):
    bq = min(ATT_BLOCK, seq)
    nq = seq // bq
    n_pair = FOX_HEADS // 2
    cq = c.reshape(batch * seq, n_pair, 2).transpose(1, 0, 2)
    ck = c.reshape(batch, seq, n_pair, 2).transpose(0, 2, 3, 1)
    qblk = pl.BlockSpec((bq, 2 * FOX_DH), lambda b, hp, i: (b * nq + i, hp))
    kvblk = pl.BlockSpec((seq, 2 * FOX_DH), lambda b, hp, i: (b, hp))
    return pl.pallas_call(
        _fox_prompt_kernel,
        grid=(batch, n_pair, nq),
        in_specs=[qblk, kvblk, kvblk,
                  pl.BlockSpec((1, bq, 2), lambda b, hp, i: (hp, b * nq + i, 0)),
                  pl.BlockSpec((1, 1, 2, seq), lambda b, hp, i: (b, hp, 0, 0))],
        out_specs=qblk,
        out_shape=jax.ShapeDtypeStruct((batch * seq, FOX_HD), BF16),
        scratch_shapes=[pltpu.VMEM((2, bq, 1), F32), pltpu.VMEM((2, bq, 1), F32),
                        pltpu.VMEM((2, bq, 2 * FOX_DH), F32)],
        compiler_params=_cparams(("arbitrary", "arbitrary", "arbitrary")),
        name="fox_attention_prompt",
    )(q_bf, k_bf, v_bf, cq, ck)


def _fox_decode_kernel(pt_ref, qcol_ref, knew_ref, vnew_ref, lfnew_ref, *refs):
    pp = PAGES_PER_STEP
    k_refs, v_refs, lf_refs = refs[:pp], refs[pp:2 * pp], refs[2 * pp:3 * pp]
    o_ref, qmat_sc, m_sc, l_sc, acc_sc, carry_sc = refs[3 * pp:]
    step = pl.program_id(1)
    page = k_refs[0].shape[1]
    head_of_row = lax.broadcasted_iota(jnp.int32, (FOX_HD, LANES), 0) // FOX_DH
    lane = lax.broadcasted_iota(jnp.int32, (FOX_HD, LANES), 1)

    @pl.when(step == 0)
    def _():
        qmat = jnp.where(head_of_row == lane, qcol_ref[0], 0.0).astype(BF16)
        qmat_sc[...] = qmat
        s_self = _dot(knew_ref[0].astype(BF16), qmat)[:, :FOX_HEADS]
        m_sc[...] = s_self
        l_sc[...] = jnp.ones_like(l_sc)
        acc_sc[...] = jnp.broadcast_to(vnew_ref[0].astype(BF16).astype(F32), acc_sc.shape)
        carry_sc[...] = lfnew_ref[0]

    qmat = qmat_sc[...]
    later = (lax.broadcasted_iota(jnp.int32, (page, page), 1) > lax.broadcasted_iota(jnp.int32, (page, page), 0))
    later = jnp.where(later, 1.0, 0.0).astype(BF16)
    carry = carry_sc[...]
    scores = []
    for i in range(pp):
        lf = lf_refs[i][0]
        hi, mid, lo = _split3(lf)
        bias = (_dot(later, lo) + _dot(later, mid)) + _dot(later, hi) + carry
        carry = carry + jnp.sum(lf, axis=0, keepdims=True)
        scores.append(_dot(k_refs[i][0].astype(BF16), qmat)[:, :FOX_HEADS] + bias)
    carry_sc[...] = carry
    m_prev = m_sc[...]
    m_new = m_prev
    for s in scores:
        m_new = jnp.maximum(m_new, jnp.max(s, axis=0, keepdims=True))
    alpha = jnp.exp(m_prev - m_new)
    l_new = alpha * l_sc[...]
    pv = jnp.zeros(acc_sc.shape, F32)
    for i in range(pp):
        p = jnp.exp(scores[i] - m_new)
        l_new = l_new + jnp.sum(p, axis=0, keepdims=True)
        pv = pv + _dot_tn(p.astype(BF16), v_refs[i][0].astype(BF16))
    eye = lax.broadcasted_iota(jnp.int32, (FOX_HEADS, FOX_HEADS), 0) == lax.broadcasted_iota(
        jnp.int32, (FOX_HEADS, FOX_HEADS), 1)
    alpha_col = jnp.sum(jnp.where(eye, alpha, 0.0), axis=1, keepdims=True)
    acc = acc_sc[...] * alpha_col + pv
    acc_sc[...] = acc
    l_sc[...] = l_new
    m_sc[...] = m_new

    @pl.when(step == pl.num_programs(1) - 1)
    def _():
        l_col = jnp.sum(jnp.where(eye, l_new, 0.0), axis=1, keepdims=True)
        own = (lax.broadcasted_iota(jnp.int32, (FOX_HEADS, FOX_HD), 1) // FOX_DH ==
               lax.broadcasted_iota(jnp.int32, (FOX_HEADS, FOX_HD), 0))
        o_ref[0] = jnp.sum(jnp.where(own, acc / l_col, 0.0), axis=0, keepdims=True).astype(o_ref.dtype)


def _fox_decode(q, k_new, v_new, lf_new, cache_k, cache_v, cache_logf, page_table):
    nb, n_pages = page_table.shape
    n_pool, page = cache_k.shape[0], cache_k.shape[1]
    pp = PAGES_PER_STEP
    n_steps = n_pages // pp
    ck = cache_k.reshape(n_pool, page, FOX_HD)
    cv = cache_v.reshape(n_pool, page, FOX_HD)

    def paged(width, i):
        return pl.BlockSpec((1, page, width), lambda b, s, pt: (pt[b, n_pages - 1 - (s * pp + i)], 0, 0))

    per_seq = lambda width: pl.BlockSpec((1, 1, width), lambda b, s, pt: (b, 0, 0))
    in_specs = [pl.BlockSpec((1, FOX_HD, 1), lambda b, s, pt: (b, 0, 0)), per_seq(FOX_HD), per_seq(FOX_HD),
                per_seq(FOX_HEADS)]
    in_specs += [paged(FOX_HD, i) for i in range(pp)] * 2 + [paged(FOX_HEADS, i) for i in range(pp)]
    out = pl.pallas_call(
        _fox_decode_kernel,
        grid_spec=pltpu.PrefetchScalarGridSpec(
            num_scalar_prefetch=1, grid=(nb, n_steps), in_specs=in_specs, out_specs=per_seq(FOX_HD),
            scratch_shapes=[pltpu.VMEM((FOX_HD, LANES), BF16), pltpu.VMEM((1, FOX_HEADS), F32),
                            pltpu.VMEM((1, FOX_HEADS), F32), pltpu.VMEM((FOX_HEADS, FOX_HD), F32),
                            pltpu.VMEM((1, FOX_HEADS), F32)]),
        out_shape=jax.ShapeDtypeStruct((nb, 1, FOX_HD), BF16),
        compiler_params=_cparams(("arbitrary", "arbitrary")),
        name="fox_attention_decode",
    )(page_table, q.reshape(nb, FOX_HD, 1), k_new.reshape(nb, 1, FOX_HD), v_new.reshape(nb, 1, FOX_HD),
      lf_new.reshape(nb, 1, FOX_HEADS), *([ck] * pp), *([cv] * pp), *([cache_logf] * pp))
    return out.reshape(nb, FOX_HD)


def _run_group(x, p, cos, sin, w, bm, seq_len, sample):
    rows = x.shape[0]
    qkv_dtype = BF16 if sample is None else F32
    q, k, v, gate = _ret_in(x, w["g_pre_mix"][0], w["w_in"], cos, sin, bm, qkv_dtype)
    if sample is None:
        batch = rows // seq_len
        o, ret_state = _retention_prompt(q, k, v, gate, w["log_gamma"], batch, seq_len)
    else:
        o, ret_state = _retention_step(sample["state_ret"], q, k, v, gate, w["log_gamma"])
    x = _mix_out(o, w["w_out_ret"], x, w["g_post_mix"][0], bm)

    conv_out = []
    ffn_args = lambda i: (w["g_pre_ffn"][i], w["w_up"][i], w["w_dw"][i], w["b_dw"][i], w["w_down"][i],
                          w["g_post_ffn"][i], p[i], w["w_ple"][i], w["w_gate"][i], bm)
    if sample is None:
        x, tail = _ffn(x, *ffn_args(0), seq_len=seq_len)
        conv_out.append(tail[:, SUBLANES - 2:, :])
    else:
        buf = sample["state_conv"][0]
        x, a = _ffn(x, *ffn_args(0), prev=(buf[:, 0], buf[:, 1]))
        conv_out.append(jnp.stack([buf[:, 1], a], axis=1))

    k_new, v_new, lf_new, k_bf, v_bf, q_bf = _kvq(x, w["g_kv"], w["w_kv"], w["w_f"], w["b_f"], w["g_pre_mix"][1],
                                                  w["w_q"], bm)
    if sample is None:
        c = _cumsum_time(lf_new, batch, seq_len)
        att = _fox_prompt(q_bf, k_bf, v_bf, c, batch, seq_len)
    else:
        att = _fox_decode(q_bf.astype(F32), k_new, v_new, lf_new, sample["cache_k"], sample["cache_v"],
                          sample["cache_logf"], sample["page_table"])
    x = _mix_out(att, w["w_out_fox"], x, w["g_post_mix"][1], bm)

    if sample is None:
        x, tail = _ffn(x, *ffn_args(1), seq_len=seq_len)
        conv_out.append(tail[:, SUBLANES - 2:, :])
    else:
        buf = sample["state_conv"][1]
        x, a = _ffn(x, *ffn_args(1), prev=(buf[:, 0], buf[:, 1]))
        conv_out.append(jnp.stack([buf[:, 1], a], axis=1))
    return x, ret_state, jnp.stack(conv_out), k_new, v_new, lf_new


def kernel(x_prompt, x_sample, state_ret, state_conv, cache_k, cache_v, cache_logf, page_table, p_prompt, p_sample,
           norm_pre_mix, norm_post_mix, norm_pre_ffn, norm_post_ffn, w_in_ret, w_out_ret, norm_kv, w_kvf, b_f,
           w_q_fox, w_out_fox, w_up, w_dw, b_dw, w_down, w_ple, w_ple_gate):
    bp, tp, _ = x_prompt.shape
    nb, ts, _ = x_sample.shape
    assert ts == 1, "the sample group is one new token per sequence"
    past_len = page_table.shape[1] * cache_k.shape[1]
    depth = w_up.shape[0]
    row_vec = lambda a: a.reshape(a.shape[0], 1, a.shape[1])
    w = {
        "g_pre_mix": row_vec(norm_pre_mix), "g_post_mix": row_vec(norm_post_mix),
        "g_pre_ffn": row_vec(norm_pre_ffn), "g_post_ffn": row_vec(norm_post_ffn),
        "g_kv": norm_kv.reshape(1, D_MODEL),
        "w_in": w_in_ret[0].astype(BF16), "w_out_ret": w_out_ret[0].astype(BF16),
        "w_kv": w_kvf[:, :2 * FOX_HD].astype(BF16),
        "w_f": jnp.pad(w_kvf[:, 2 * FOX_HD:], ((0, 0), (0, LANES - FOX_HEADS))).astype(BF16),
        "b_f": jnp.pad(b_f, (0, LANES - FOX_HEADS)).reshape(1, LANES),
        "w_q": w_q_fox[0].astype(BF16), "w_out_fox": w_out_fox[0].astype(BF16),
        "w_up": w_up.astype(BF16), "w_dw": w_dw, "b_dw": row_vec(b_dw), "w_down": w_down.astype(BF16),
        "w_ple": w_ple.astype(BF16), "w_gate": w_ple_gate.astype(BF16),
        "log_gamma": jnp.log1p(-jnp.exp2(-5.0 - jnp.arange(RET_HEADS, dtype=F32))),
    }
    bm = min(ROW_BLOCK, tp)
    cos_p, sin_p = _rope_table(tp, 0)
    y_p, ret_p, conv_p, k_p, v_p, lf_p = _run_group(
        x_prompt.reshape(bp * tp, D_MODEL), p_prompt.reshape(depth, bp * tp, PLE_DIM), cos_p, sin_p, w, bm, tp, None)
    cos_s, sin_s = _rope_table(SUBLANES, past_len)
    cos_s = jnp.broadcast_to(cos_s[:1], (nb, ROPE_HALF))
    sin_s = jnp.broadcast_to(sin_s[:1], (nb, ROPE_HALF))
    sample = {"state_ret": state_ret[0], "state_conv": state_conv, "cache_k": cache_k, "cache_v": cache_v,
              "cache_logf": cache_logf, "page_table": page_table}
    y_s, ret_s, conv_s, k_s, v_s, lf_s = _run_group(
        x_sample.reshape(nb, D_MODEL), p_sample.reshape(depth, nb, PLE_DIM), cos_s, sin_s, w, nb, 1, sample)
    return (y_p.reshape(bp, tp, D_MODEL), y_s.reshape(nb, 1, D_MODEL),
            ret_p[None], ret_s[None], conv_p, conv_s,
            k_p.reshape(bp, tp, FOX_HEADS, FOX_DH), k_s.reshape(nb, 1, FOX_HEADS, FOX_DH),
            v_p.reshape(bp, tp, FOX_HEADS, FOX_DH), v_s.reshape(nb, 1, FOX_HEADS, FOX_DH),
            lf_p.reshape(bp, tp, FOX_HEADS), lf_s.reshape(nb, 1, FOX_HEADS))
```

```python
import functools
import math

import jax
import jax.numpy as jnp
from jax import lax
from jax.experimental import pallas as pl
from jax.experimental.pallas import tpu as pltpu

F32 = jnp.float32
BF16 = jnp.bfloat16

D_MODEL = 1024
RET_HEADS = 4
RET_DK = 256
RET_DV = 512
RET_QK = RET_HEADS * RET_DK
RET_VD = RET_HEADS * RET_DV
ROPE_BASE = 10000.0
ROPE_HALF = RET_DK // 2
FOX_HEADS = 16
FOX_DH = 64
FOX_HD = FOX_HEADS * FOX_DH
D_FF = 2816
PLE_DIM = 256
NORM_EPS = 1e-6

LANES = 128
SUBLANES = 8
VMEM_LIMIT = 56 * 1024 * 1024

ROW_BLOCK = 512
RET_CHUNK = 256
FFN_CHUNK = 256
ATT_BLOCK = 512
PAGES_PER_STEP = 8
EXP_UNDERFLOW = 110.0
NORM_PAD = 1.05


def _cparams(sem):
    return pltpu.CompilerParams(dimension_semantics=sem, vmem_limit_bytes=VMEM_LIMIT)


def _resident(shape):
    return pl.BlockSpec(shape, lambda *_: (0,) * len(shape), pipeline_mode=pl.Buffered(1))


def _dot(a, b):
    return jnp.dot(a, b, preferred_element_type=F32)


def _dot_nt(a, b):
    return lax.dot_general(a, b, (((1,), (1,)), ((), ())), preferred_element_type=F32)


def _dot_tn(a, b):
    return lax.dot_general(a, b, (((0,), (0,)), ((), ())), preferred_element_type=F32)


def _rms(x, g):
    return x * lax.rsqrt(jnp.mean(x * x, axis=-1, keepdims=True) + NORM_EPS) * g


def _sigmoid(x):
    return 1.0 / (1.0 + jnp.exp(-x))


def _gelu_tanh(x):
    return x * (0.5 * (1.0 + jnp.tanh(math.sqrt(2.0 / math.pi) * (x + 0.044715 * (x * x * x)))))


def _log_sigmoid(x):
    z = -x
    return -(jnp.maximum(z, 0.0) + jnp.log1p(jnp.exp(-jnp.abs(z))))


def _split3(x):
    hi = x.astype(BF16)
    r1 = x - hi.astype(F32)
    mid = r1.astype(BF16)
    lo = (r1 - mid.astype(F32)).astype(BF16)
    return hi, mid, lo


def _rope_kernel(cos_ref, sin_ref, *, base_pos):
    n = cos_ref.shape[0]
    i = pl.program_id(0)
    lane = lax.broadcasted_iota(jnp.int32, (1, ROPE_HALF), 1).astype(F32)
    inv = jnp.power(jnp.full((1, ROPE_HALF), ROPE_BASE, F32), -(lane / ROPE_HALF))
    pos = (lax.broadcasted_iota(jnp.int32, (n, ROPE_HALF), 0) + (i * n + base_pos)).astype(F32)
    ang = pos * inv
    cos_ref[...] = jnp.cos(ang)
    sin_ref[...] = jnp.sin(ang)


def _rope_table(n_rows, base_pos):
    bn = min(n_rows, 1024)
    spec = pl.BlockSpec((bn, ROPE_HALF), lambda i: (i, 0))
    return pl.pallas_call(
        functools.partial(_rope_kernel, base_pos=base_pos),
        grid=(n_rows // bn,),
        out_specs=[spec, spec],
        out_shape=[jax.ShapeDtypeStruct((n_rows, ROPE_HALF), F32)] * 2,
        compiler_params=_cparams(("arbitrary",)),
        name="rope_table",
    )()


def _ret_in_kernel(x_ref, g_ref, w_ref, cos_ref, sin_ref, q_ref, k_ref, v_ref, gate_ref):
    xn = _rms(x_ref[...], g_ref[...]).astype(BF16)
    cos = cos_ref[...]
    sin = sin_ref[...]
    for out_ref, base, scale in ((q_ref, 0, 1.0), (k_ref, RET_QK, RET_DK ** -0.5)):
        for h in range(RET_HEADS):
            lo = h * RET_DK
            u = _dot(xn, w_ref[:, base + lo:base + lo + RET_DK])
            x1, x2 = u[:, :ROPE_HALF], u[:, ROPE_HALF:]
            out_ref[:, lo:lo + ROPE_HALF] = ((x1 * cos - x2 * sin) * scale).astype(out_ref.dtype)
            out_ref[:, lo + ROPE_HALF:lo + RET_DK] = ((x1 * sin + x2 * cos) * scale).astype(out_ref.dtype)
    for h in range(RET_HEADS):
        lo = h * RET_DV
        v_ref[:, lo:lo + RET_DV] = _dot(xn, w_ref[:, 2 * RET_QK + lo:2 * RET_QK + lo + RET_DV]).astype(v_ref.dtype)
        gate_ref[:, lo:lo + RET_DV] = _dot(xn, w_ref[:, 2 * RET_QK + RET_VD + lo:2 * RET_QK + RET_VD + lo + RET_DV])


def _ret_in(x, g, w_bf, cos, sin, bm, qkv_dtype):
    rows = x.shape[0]
    n_tab = cos.shape[0] // bm
    row = lambda i: (i, 0)
    tab = pl.BlockSpec((bm, ROPE_HALF), lambda i: (i % n_tab, 0))
    return pl.pallas_call(
        _ret_in_kernel,
        grid=(rows // bm,),
        in_specs=[pl.BlockSpec((bm, D_MODEL), row), _resident((1, D_MODEL)), _resident(w_bf.shape), tab, tab],
        out_specs=[pl.BlockSpec((bm, RET_QK), row), pl.BlockSpec((bm, RET_QK), row),
                   pl.BlockSpec((bm, RET_VD), row), pl.BlockSpec((bm, RET_VD), row)],
        out_shape=[jax.ShapeDtypeStruct((rows, RET_QK), qkv_dtype), jax.ShapeDtypeStruct((rows, RET_QK), qkv_dtype),
                   jax.ShapeDtypeStruct((rows, RET_VD), qkv_dtype), jax.ShapeDtypeStruct((rows, RET_VD), F32)],
        compiler_params=_cparams(("arbitrary",)),
        name="ret_in",
    )(x, g, w_bf, cos, sin)


def _ret_chunk_kernel(lg_ref, q_ref, k_ref, v_ref, gate_ref, o_ref, s_ref):
    h = pl.program_id(1)
    c = pl.program_id(2)
    lg = lg_ref[h]
    C = q_ref.shape[0]

    @pl.when(c == 0)
    def _():
        s_ref[...] = jnp.zeros_like(s_ref)

    q = q_ref[...]
    k = k_ref[...]
    v = v_ref[...]
    rel = lax.broadcasted_iota(jnp.int32, (C, C), 0) - lax.broadcasted_iota(jnp.int32, (C, C), 1)
    decay = jnp.where(rel >= 0, jnp.exp(jnp.maximum(rel, 0).astype(F32) * lg), 0.0)
    scores = _dot_nt(q, k) * decay
    inner = _dot(scores.astype(BF16), v)
    n = lax.broadcasted_iota(jnp.int32, (C, 1), 0).astype(F32)
    state = s_ref[0, 0]
    cross = _dot(q, state.astype(BF16)) * jnp.exp((n + 1.0) * lg)
    kd = (k.astype(F32) * jnp.exp((C - 1.0 - n) * lg)).astype(BF16)
    chunk_decay = jnp.exp(jnp.full((1, RET_DV), float(C), F32) * lg)
    s_ref[0, 0] = state * chunk_decay + _dot_tn(kd, v)
    o = inner + cross
    o = o * lax.rsqrt(jnp.mean(o * o, axis=-1, keepdims=True) + NORM_EPS)
    gate = gate_ref[...]
    o_ref[...] = (o * (gate * _sigmoid(gate))).astype(o_ref.dtype)


def _retention_prompt(q, k, v, gate, log_gamma, batch, seq):
    C = min(RET_CHUNK, seq)
    nc = seq // C
    blk = lambda b, h, c, lg: (b * nc + c, h)
    return pl.pallas_call(
        _ret_chunk_kernel,
        grid_spec=pltpu.PrefetchScalarGridSpec(
            num_scalar_prefetch=1,
            grid=(batch, RET_HEADS, nc),
            in_specs=[pl.BlockSpec((C, RET_DK), blk), pl.BlockSpec((C, RET_DK), blk),
                      pl.BlockSpec((C, RET_DV), blk), pl.BlockSpec((C, RET_DV), blk)],
            out_specs=[pl.BlockSpec((C, RET_DV), blk),
                       pl.BlockSpec((1, 1, RET_DK, RET_DV), lambda b, h, c, lg: (b, h, 0, 0))],
        ),
        out_shape=[jax.ShapeDtypeStruct((batch * seq, RET_VD), BF16),
                   jax.ShapeDtypeStruct((batch, RET_HEADS, RET_DK, RET_DV), F32)],
        compiler_params=_cparams(("arbitrary", "arbitrary", "arbitrary")),
        name="retention_chunks",
    )(log_gamma, q, k, v, gate)


def _ret_step_kernel(lg_ref, s_ref, qc_ref, kc_ref, v_ref, gate_ref, o_ref, so_ref):
    for h in range(RET_HEADS):
        gamma = jnp.exp(jnp.full((1, RET_DV), 1.0, F32) * lg_ref[h])
        state = s_ref[0, h]
        qc = qc_ref[0, h]
        kc = kc_ref[0, h]
        vr = v_ref[0, :, h * RET_DV:(h + 1) * RET_DV]
        qk = jnp.sum(qc * kc, axis=0, keepdims=True)
        qs = jnp.sum(qc * state, axis=0, keepdims=True)
        o = qk * vr + qs * gamma
        so_ref[0, h] = state * gamma + kc * vr
        o = o * lax.rsqrt(jnp.mean(o * o, axis=-1, keepdims=True) + NORM_EPS)
        gate = gate_ref[0, :, h * RET_DV:(h + 1) * RET_DV]
        o_ref[0, :, h * RET_DV:(h + 1) * RET_DV] = (o * (gate * _sigmoid(gate))).astype(o_ref.dtype)


def _retention_step(state, q, k, v, gate, log_gamma):
    nb = state.shape[0]
    qc = q.reshape(nb, RET_HEADS, RET_DK, 1)
    kc = k.reshape(nb, RET_HEADS, RET_DK, 1)
    st = pl.BlockSpec((1, RET_HEADS, RET_DK, RET_DV), lambda b, lg: (b, 0, 0, 0))
    col = pl.BlockSpec((1, RET_HEADS, RET_DK, 1), lambda b, lg: (b, 0, 0, 0))
    rowv = pl.BlockSpec((1, 1, RET_VD), lambda b, lg: (b, 0, 0))
    o, s_new = pl.pallas_call(
        _ret_step_kernel,
        grid_spec=pltpu.PrefetchScalarGridSpec(
            num_scalar_prefetch=1, grid=(nb,),
            in_specs=[st, col, col, rowv, rowv], out_specs=[rowv, st]),
        out_shape=[jax.ShapeDtypeStruct((nb, 1, RET_VD), BF16), jax.ShapeDtypeStruct(state.shape, F32)],
        compiler_params=_cparams(("arbitrary",)),
        name="retention_step",
    )(log_gamma, state, qc, kc, v.reshape(nb, 1, RET_VD), gate.reshape(nb, 1, RET_VD))
    return o.reshape(nb, RET_VD), s_new


def _mix_out_kernel(o_ref, w_ref, x_ref, g_ref, y_ref):
    y_ref[...] = x_ref[...] + _rms(_dot(o_ref[...], w_ref[...]), g_ref[...])


def _mix_out(o_bf, w_bf, x, g, bm):
    rows, kdim = o_bf.shape
    row = lambda i: (i, 0)
    return pl.pallas_call(
        _mix_out_kernel,
        grid=(rows // bm,),
        in_specs=[pl.BlockSpec((bm, kdim), row), _resident(w_bf.shape), pl.BlockSpec((bm, D_MODEL), row),
                  _resident((1, D_MODEL))],
        out_specs=pl.BlockSpec((bm, D_MODEL), row),
        out_shape=jax.ShapeDtypeStruct((rows, D_MODEL), F32),
        compiler_params=_cparams(("arbitrary",)),
        name="mix_out",
    )(o_bf, w_bf, x, g)


def _ffn_kernel(*refs, seq_mode, blocks_per_seq):
    if seq_mode:
        (x_ref, gpre_ref, wup_ref, wdw_ref, bdw_ref, wdown_ref, gpost_ref, p_ref, wple_ref, wgate_ref,
         y_ref, tail_ref, h_sc, carry_sc) = refs
    else:
        (x_ref, gpre_ref, wup_ref, wdw_ref, bdw_ref, wdown_ref, gpost_ref, p_ref, wple_ref, wgate_ref,
         prev2_ref, prev1_ref, y_ref, a_ref, h_sc) = refs
    x = x_ref[...]
    bm = x.shape[0]
    xn = _rms(x, gpre_ref[...]).astype(BF16)
    if seq_mode:
        @pl.when(pl.program_id(0) % blocks_per_seq == 0)
        def _():
            carry_sc[...] = jnp.zeros_like(carry_sc)
        row = lax.broadcasted_iota(jnp.int32, (bm, FFN_CHUNK), 0)
    for c in range(D_FF // FFN_CHUNK):
        sl = slice(c * FFN_CHUNK, (c + 1) * FFN_CHUNK)
        a = _dot(xn, wup_ref[:, sl])
        b = _dot(xn, wup_ref[:, D_FF + c * FFN_CHUNK:D_FF + (c + 1) * FFN_CHUNK])
        if seq_mode:
            c2 = carry_sc[SUBLANES - 2:SUBLANES - 1, sl]
            c1 = carry_sc[SUBLANES - 1:SUBLANES, sl]
            a1 = jnp.where(row == 0, c1, pltpu.roll(a, 1, 0))
            a2 = jnp.where(row == 0, c2, jnp.where(row == 1, c1, pltpu.roll(a, 2, 0)))
            last = a[bm - SUBLANES:bm, :]
            carry_sc[:, sl] = last
            tail_ref[0, :, sl] = last
        else:
            a1 = prev1_ref[:, sl]
            a2 = prev2_ref[:, sl]
            a_ref[:, sl] = a
        conv = bdw_ref[:, sl] + a2 * wdw_ref[0:1, sl]
        conv = conv + a1 * wdw_ref[1:2, sl]
        conv = conv + a * wdw_ref[2:3, sl]
        h_sc[:, sl] = (_gelu_tanh(conv) * b).astype(BF16)
    f = _dot(h_sc[...], wdown_ref[...])
    x2 = x + _rms(f, gpost_ref[...])
    ple = _dot(p_ref[...].astype(BF16), wple_ref[...])
    gate = _dot(x2.astype(BF16), wgate_ref[...])
    y_ref[...] = x2 + ple * _sigmoid(gate)


def _ffn(x, gpre, wup_bf, wdw, bdw, wdown_bf, gpost, p, wple_bf, wgate_bf, bm, seq_len=None, prev=None):
    rows = x.shape[0]
    row = lambda i: (i, 0)
    seq_mode = prev is None
    in_specs = [pl.BlockSpec((bm, D_MODEL), row), _resident((1, D_MODEL)), _resident(wup_bf.shape),
                _resident(wdw.shape), _resident((1, D_FF)), _resident(wdown_bf.shape), _resident((1, D_MODEL)),
                pl.BlockSpec((bm, PLE_DIM), row), _resident(wple_bf.shape), _resident(wgate_bf.shape)]
    args = [x, gpre, wup_bf, wdw, bdw, wdown_bf, gpost, p, wple_bf, wgate_bf]
    scratch = [pltpu.VMEM((bm, D_FF), BF16)]
    if seq_mode:
        bps = seq_len // bm
        out_specs = [pl.BlockSpec((bm, D_MODEL), row), pl.BlockSpec((1, SUBLANES, D_FF), lambda i: (i // bps, 0, 0))]
        out_shape = [jax.ShapeDtypeStruct((rows, D_MODEL), F32),
                     jax.ShapeDtypeStruct((rows // seq_len, SUBLANES, D_FF), F32)]
        scratch.append(pltpu.VMEM((SUBLANES, D_FF), F32))
    else:
        bps = 1
        in_specs += [pl.BlockSpec((bm, D_FF), row), pl.BlockSpec((bm, D_FF), row)]
        args += list(prev)
        out_specs = [pl.BlockSpec((bm, D_MODEL), row), pl.BlockSpec((bm, D_FF), row)]
        out_shape = [jax.ShapeDtypeStruct((rows, D_MODEL), F32), jax.ShapeDtypeStruct((rows, D_FF), F32)]
    return pl.pallas_call(
        functools.partial(_ffn_kernel, seq_mode=seq_mode, blocks_per_seq=bps),
        grid=(rows // bm,),
        in_specs=in_specs, out_specs=out_specs, out_shape=out_shape, scratch_shapes=scratch,
        compiler_params=_cparams(("arbitrary",)),
        name="conv_ffn",
    )(*args)


def _kvq_kernel(x_ref, gkv_ref, wkv_ref, wf_ref, bf_ref, gq_ref, wq_ref,
                k_ref, v_ref, lf_ref, kb_ref, vb_ref, qb_ref, kn_ref, qn_ref):
    x = x_ref[...]
    xn = _rms(x, gkv_ref[...]).astype(BF16)
    k = _dot(xn, wkv_ref[:, :FOX_HD])
    v = _dot(xn, wkv_ref[:, FOX_HD:])
    k_ref[...] = k
    v_ref[...] = v
    kb = k.astype(BF16)
    kb_ref[...] = kb
    vb_ref[...] = v.astype(BF16)
    f = _dot(xn, wf_ref[...]) + bf_ref[...]
    lf_ref[...] = _log_sigmoid(f)[:, :FOX_HEADS]
    xq = _rms(x, gq_ref[...]).astype(BF16)
    qb = (_dot(xq, wq_ref[...]) * FOX_DH ** -0.5).astype(BF16)
    qb_ref[...] = qb
    head_sum = (lax.broadcasted_iota(jnp.int32, (FOX_HD, LANES), 0) // FOX_DH ==
                lax.broadcasted_iota(jnp.int32, (FOX_HD, LANES), 1))
    head_sum = jnp.where(head_sum, 1.0, 0.0).astype(BF16)
    for src, dst in ((kb, kn_ref), (qb, qn_ref)):
        sf = src.astype(F32)
        n2 = jnp.max(_dot((sf * sf).astype(BF16), head_sum), axis=0, keepdims=True)
        dst[0] = jnp.broadcast_to(n2, (SUBLANES, LANES))


def _kvq(x, gkv, wkv_bf, wf_bf, bf_pad, gq, wq_bf, bm):
    rows = x.shape[0]
    row = lambda i: (i, 0)
    wide = pl.BlockSpec((bm, FOX_HD), row)
    norm = pl.BlockSpec((1, SUBLANES, LANES), lambda i: (i, 0, 0))
    norm_shape = jax.ShapeDtypeStruct((rows // bm, SUBLANES, LANES), F32)
    return pl.pallas_call(
        _kvq_kernel,
        grid=(rows // bm,),
        in_specs=[pl.BlockSpec((bm, D_MODEL), row), _resident((1, D_MODEL)), _resident(wkv_bf.shape),
                  _resident(wf_bf.shape), _resident((1, LANES)), _resident((1, D_MODEL)), _resident(wq_bf.shape)],
        out_specs=[wide, wide, pl.BlockSpec((bm, FOX_HEADS), row), wide, wide, wide, norm, norm],
        out_shape=[jax.ShapeDtypeStruct((rows, FOX_HD), F32), jax.ShapeDtypeStruct((rows, FOX_HD), F32),
                   jax.ShapeDtypeStruct((rows, FOX_HEADS), F32), jax.ShapeDtypeStruct((rows, FOX_HD), BF16),
                   jax.ShapeDtypeStruct((rows, FOX_HD), BF16), jax.ShapeDtypeStruct((rows, FOX_HD), BF16),
                   norm_shape, norm_shape],
        compiler_params=_cparams(("arbitrary",)),
        name="kv_q_proj",
    )(x, gkv, wkv_bf, wf_bf, bf_pad, gq, wq_bf)


def _cumsum_kernel(lf_ref, cq_ref, ck_ref, first_ref, last_ref, carry_sc):
    bl = lf_ref.shape[0]
    width = FOX_HEADS * LANES

    @pl.when(pl.program_id(1) == 0)
    def _():
        carry_sc[...] = jnp.zeros_like(carry_sc)

    lf = lf_ref[...]
    tri = (lax.broadcasted_iota(jnp.int32, (bl, bl), 1) <= lax.broadcasted_iota(jnp.int32, (bl, bl), 0))
    tri = jnp.where(tri, 1.0, 0.0).astype(BF16)
    hi, mid, lo = _split3(lf)
    c = (_dot(tri, lo) + _dot(tri, mid)) + _dot(tri, hi) + carry_sc[...]
    carry_sc[...] = c[bl - 1:bl, :]
    first_ref[0] = c[0:1, :]
    last_ref[0] = c[bl - 1:bl, :]

    col = lax.broadcasted_iota(jnp.int32, (FOX_HEADS, width), 1)
    slot = col - lax.broadcasted_iota(jnp.int32, (FOX_HEADS, width), 0) * LANES
    lane = lax.broadcasted_iota(jnp.int32, (1, width), 1) % LANES
    cq = jnp.where(jnp.logical_and(lane >= 3, lane < 6), 1.0, 0.0)
    ck = jnp.where(lane < 3, 1.0, 0.0)
    for piece, part in enumerate(_split3(c)):
        cq = cq + _dot(part, jnp.where(slot == piece, 1.0, 0.0).astype(BF16))
        ck = ck + _dot(part, jnp.where(slot == piece + 3, -1.0, 0.0).astype(BF16))
    cq_ref[...] = cq.astype(BF16)
    ck_ref[...] = ck.astype(BF16)


def _cumsum_time(lf, batch, seq):
    bl = min(ATT_BLOCK, seq)
    nb = seq // bl
    blk = lambda b, i: (b * nb + i, 0)
    wide = pl.BlockSpec((bl, FOX_HEADS * LANES), blk)
    edge = pl.BlockSpec((1, 1, FOX_HEADS), lambda b, i: (b * nb + i, 0, 0))
    return pl.pallas_call(
        _cumsum_kernel,
        grid=(batch, nb),
        in_specs=[pl.BlockSpec((bl, FOX_HEADS), blk)], out_specs=[wide, wide, edge, edge],
        out_shape=[jax.ShapeDtypeStruct((lf.shape[0], FOX_HEADS * LANES), BF16)] * 2
        + [jax.ShapeDtypeStruct((batch * nb, 1, FOX_HEADS), F32)] * 2,
        scratch_shapes=[pltpu.VMEM((1, FOX_HEADS), F32)],
        compiler_params=_cparams(("arbitrary", "arbitrary")),
        name="logf_cumsum",
    )(lf)


def _fox_prompt_kernel(cfirst_ref, clast_ref, qkb_ref, q_ref, k_ref, v_ref, cq_ref, ck_ref, o_ref, m_sc, acc_sc):
    b = pl.program_id(0)
    hp = pl.program_id(1)
    i = pl.program_id(2)

    def dead(j):
        worst = None
        for hl in range(2):
            h = 2 * hp + hl
            gap = cfirst_ref[b, i, h] - clast_ref[b, j, h] + 2.0 * qkb_ref[b, h]
            worst = gap if worst is None else jnp.maximum(worst, gap)
        return worst < -EXP_UNDERFLOW

    j_lo = lax.while_loop(lambda j: jnp.logical_and(j < i, dead(j)), lambda j: j + 1, jnp.int32(0))
    bq = q_ref.shape[0]
    pair = 2 * FOX_DH
    q = q_ref[...]
    low = lax.broadcasted_iota(jnp.int32, (1, pair), 1) < FOX_DH
    causal = (lax.broadcasted_iota(jnp.int32, (bq, bq), 1) <= lax.broadcasted_iota(jnp.int32, (bq, bq), 0))
    ones = jnp.ones((bq, pair), BF16)
    q_aug = []
    for hl in range(2):
        qm = jnp.where(low if hl == 0 else jnp.logical_not(low), q, jnp.zeros_like(q))
        q_aug.append(jnp.concatenate([qm, cq_ref[:, hl * LANES:(hl + 1) * LANES]], axis=1))
        m_sc[hl] = jnp.full(m_sc.shape[1:], -jnp.inf, F32)
        acc_sc[hl] = jnp.zeros(acc_sc.shape[1:], F32)

    def block(j, masked):
        start = pl.multiple_of(j * bq, bq)
        kb = k_ref[pl.ds(start, bq), :]
        v_aug = jnp.concatenate([v_ref[pl.ds(start, bq), :], ones], axis=1)
        for hl in range(2):
            k_aug = jnp.concatenate([kb, ck_ref[pl.ds(start, bq), hl * LANES:(hl + 1) * LANES]], axis=1)
            s = _dot_nt(q_aug[hl], k_aug)
            if masked:
                s = jnp.where(causal, s, -jnp.inf)
            m_prev = m_sc[hl]
            m_next = jnp.maximum(m_prev, jnp.max(s, axis=1, keepdims=True))
            alpha = jnp.exp(m_prev - m_next)
            p = jnp.exp(s - pltpu.repeat(m_next, bq // LANES, 1))
            acc_sc[hl] = pltpu.repeat(alpha, 2, 1) * acc_sc[hl] + _dot(p.astype(BF16), v_aug)
            m_sc[hl] = m_next

    n_live = i - j_lo + 1

    @pl.when(n_live == 1)
    def _():
        block(i, True)

    @pl.when(n_live >= 2)
    def _():
        block(i, True)
        block(i - 1, False)

    def body(t, carry):
        block(i - 2 - 2 * t, False)
        block(i - 3 - 2 * t, False)
        return carry

    rest = jnp.maximum(n_live - 2, 0)
    lax.fori_loop(0, rest // 2, body, 0)

    @pl.when(rest % 2 == 1)
    def _():
        block(j_lo, False)

    o0 = acc_sc[0, :, :pair] / acc_sc[0, :, pair:]
    o1 = acc_sc[1, :, :pair] / acc_sc[1, :, pair:]
    o_ref[...] = jnp.where(low, o0, o1).astype(o_ref.dtype)


def _fox_prompt(q_bf, k_bf, v_bf, cq, ck, c_first, c_last, qk_bound, batch, seq):
    bq = min(ATT_BLOCK, seq)
    nq = seq // bq
    pair = 2 * FOX_DH
    qblk = pl.BlockSpec((bq, pair), lambda b, hp, i, *_: (b * nq + i, hp))
    kvblk = pl.BlockSpec((seq, pair), lambda b, hp, i, *_: (b, hp))
    return pl.pallas_call(
        _fox_prompt_kernel,
        grid_spec=pltpu.PrefetchScalarGridSpec(
            num_scalar_prefetch=3,
            grid=(batch, FOX_HEADS // 2, nq),
            in_specs=[qblk, kvblk, kvblk,
                      pl.BlockSpec((bq, 2 * LANES), lambda b, hp, i, *_: (b * nq + i, hp)),
                      pl.BlockSpec((seq, 2 * LANES), lambda b, hp, i, *_: (b, hp))],
            out_specs=qblk,
            scratch_shapes=[pltpu.VMEM((2, bq, LANES), F32), pltpu.VMEM((2, bq, 2 * pair), F32)]),
        out_shape=jax.ShapeDtypeStruct((batch * seq, FOX_HD), BF16),
        compiler_params=_cparams(("arbitrary", "arbitrary", "arbitrary")),
        name="fox_attention_prompt",
    )(c_first, c_last, qk_bound, q_bf, k_bf, v_bf, cq, ck)


def _fox_decode_kernel(pt_ref, q_ref, knew_ref, vnew_ref, lfnew_ref, *refs):
    pp = PAGES_PER_STEP
    k_refs, v_refs, lf_refs = refs[:pp], refs[pp:2 * pp], refs[2 * pp:3 * pp]
    o_ref, m_sc, l_sc, acc_sc, carry_sc = refs[3 * pp:]
    step = pl.program_id(1)
    page = k_refs[0].shape[3]
    q = q_ref[0]

    @pl.when(step == 0)
    def _():
        m_sc[...] = jnp.sum(q * knew_ref[0], axis=1)
        l_sc[...] = jnp.ones_like(l_sc)
        lane = lax.broadcasted_iota(jnp.int32, acc_sc.shape, 2)
        acc_sc[...] = jnp.where(lane == 0, vnew_ref[0], 0.0)
        carry_sc[...] = lfnew_ref[0]

    later = (lax.broadcasted_iota(jnp.int32, (page, page), 0) > lax.broadcasted_iota(jnp.int32, (page, page), 1))
    later = jnp.where(later, 1.0, 0.0).astype(BF16)
    carry = carry_sc[...]
    scores = []
    for i in range(pp):
        lf = lf_refs[i][0]
        hi, mid, lo = _split3(lf)
        bias = (_dot(lo, later) + _dot(mid, later)) + _dot(hi, later) + carry
        carry = carry + jnp.sum(lf, axis=1, keepdims=True)
        scores.append(jnp.sum(k_refs[i][0] * q, axis=1) + bias)
    carry_sc[...] = carry
    m_prev = m_sc[...]
    m_new = m_prev
    for s in scores:
        m_new = jnp.maximum(m_new, jnp.max(s, axis=1, keepdims=True))
    alpha = jnp.exp(m_prev - m_new)
    l_new = alpha * l_sc[...]
    probs = []
    for s in scores:
        p = jnp.exp(s - m_new)
        l_new = l_new + jnp.sum(p, axis=1, keepdims=True)
        probs.append(p)
    l_sc[...] = l_new
    m_sc[...] = m_new
    for h in range(FOX_HEADS):
        acc = acc_sc[h] * alpha[h:h + 1, :]
        for i in range(pp):
            acc = acc + v_refs[i][0, h] * probs[i][h:h + 1, :]
        acc_sc[h] = acc

    @pl.when(step == pl.num_programs(1) - 1)
    def _():
        o_ref[0] = (jnp.sum(acc_sc[...], axis=2) / l_new).astype(o_ref.dtype)


def _fox_decode(q, k_new, v_new, lf_new, cache_k, cache_v, cache_logf, page_table):
    nb, n_pages = page_table.shape
    page = cache_k.shape[1]
    pp = PAGES_PER_STEP
    n_steps = n_pages // pp
    ck = jnp.transpose(cache_k, (0, 2, 3, 1))
    cv = jnp.transpose(cache_v, (0, 2, 3, 1))
    clf = jnp.transpose(cache_logf, (0, 2, 1))
    def kv_page(i):
        return pl.BlockSpec((1, FOX_HEADS, FOX_DH, page),
                            lambda b, s, pt: (pt[b, n_pages - 1 - (s * pp + i)], 0, 0, 0))

    def lf_page(i):
        return pl.BlockSpec((1, FOX_HEADS, page), lambda b, s, pt: (pt[b, n_pages - 1 - (s * pp + i)], 0, 0))

    col = pl.BlockSpec((1, FOX_HEADS, FOX_DH, 1), lambda b, s, pt: (b, 0, 0, 0))
    in_specs = [col, col, col, pl.BlockSpec((1, FOX_HEADS, 1), lambda b, s, pt: (b, 0, 0))]
    in_specs += [kv_page(i) for i in range(pp)] * 2 + [lf_page(i) for i in range(pp)]
    as_col = lambda a: a.reshape(nb, FOX_HEADS, FOX_DH, 1)
    out = pl.pallas_call(
        _fox_decode_kernel,
        grid_spec=pltpu.PrefetchScalarGridSpec(
            num_scalar_prefetch=1, grid=(nb, n_steps), in_specs=in_specs,
            out_specs=pl.BlockSpec((1, FOX_HEADS, FOX_DH), lambda b, s, pt: (b, 0, 0)),
            scratch_shapes=[pltpu.VMEM((FOX_HEADS, 1), F32), pltpu.VMEM((FOX_HEADS, 1), F32),
                            pltpu.VMEM((FOX_HEADS, FOX_DH, page), F32), pltpu.VMEM((FOX_HEADS, 1), F32)]),
        out_shape=jax.ShapeDtypeStruct((nb, FOX_HEADS, FOX_DH), BF16),
        compiler_params=_cparams(("arbitrary", "arbitrary")),
        name="fox_attention_decode",
    )(page_table, as_col(q), as_col(k_new), as_col(v_new), lf_new.reshape(nb, FOX_HEADS, 1),
      *([ck] * pp), *([cv] * pp), *([clf] * pp))
    return out.reshape(nb, FOX_HD)


def _run_group(x, p, cos, sin, w, bm, seq_len, sample):
    rows = x.shape[0]
    qkv_dtype = BF16 if sample is None else F32
    q, k, v, gate = _ret_in(x, w["g_pre_mix"][0], w["w_in"], cos, sin, bm, qkv_dtype)
    if sample is None:
        batch = rows // seq_len
        o, ret_state = _retention_prompt(q, k, v, gate, w["log_gamma"], batch, seq_len)
    else:
        o, ret_state = _retention_step(sample["state_ret"], q, k, v, gate, w["log_gamma"])
    x = _mix_out(o, w["w_out_ret"], x, w["g_post_mix"][0], bm)

    conv_out = []
    ffn_args = lambda i: (w["g_pre_ffn"][i], w["w_up"][i], w["w_dw"][i], w["b_dw"][i], w["w_down"][i],
                          w["g_post_ffn"][i], p[i], w["w_ple"][i], w["w_gate"][i], bm)
    if sample is None:
        x, tail = _ffn(x, *ffn_args(0), seq_len=seq_len)
        conv_out.append(tail[:, SUBLANES - 2:, :])
    else:
        buf = sample["state_conv"][0]
        x, a = _ffn(x, *ffn_args(0), prev=(buf[:, 0], buf[:, 1]))
        conv_out.append(jnp.stack([buf[:, 1], a], axis=1))

    k_new, v_new, lf_new, k_bf, v_bf, q_bf, k_n2, q_n2 = _kvq(
        x, w["g_kv"], w["w_kv"], w["w_f"], w["b_f"], w["g_pre_mix"][1], w["w_q"], bm)
    if sample is None:
        cq, ck, c_first, c_last = _cumsum_time(lf_new, batch, seq_len)
        per_seq_max = lambda n2: jnp.max(n2.reshape(batch, -1, SUBLANES, LANES)[:, :, 0, :FOX_HEADS], axis=1)
        qk_bound = jnp.sqrt(per_seq_max(k_n2) * per_seq_max(q_n2)) * NORM_PAD
        edges = lambda e: e.reshape(batch, -1, FOX_HEADS)
        att = _fox_prompt(q_bf, k_bf, v_bf, cq, ck, edges(c_first), edges(c_last), qk_bound, batch, seq_len)
    else:
        att = _fox_decode(q_bf.astype(F32), k_new, v_new, lf_new, sample["cache_k"], sample["cache_v"],
                          sample["cache_logf"], sample["page_table"])
    x = _mix_out(att, w["w_out_fox"], x, w["g_post_mix"][1], bm)

    if sample is None:
        x, tail = _ffn(x, *ffn_args(1), seq_len=seq_len)
        conv_out.append(tail[:, SUBLANES - 2:, :])
    else:
        buf = sample["state_conv"][1]
        x, a = _ffn(x, *ffn_args(1), prev=(buf[:, 0], buf[:, 1]))
        conv_out.append(jnp.stack([buf[:, 1], a], axis=1))
    return x, ret_state, jnp.stack(conv_out), k_new, v_new, lf_new


def kernel(x_prompt, x_sample, state_ret, state_conv, cache_k, cache_v, cache_logf, page_table, p_prompt, p_sample,
           norm_pre_mix, norm_post_mix, norm_pre_ffn, norm_post_ffn, w_in_ret, w_out_ret, norm_kv, w_kvf, b_f,
           w_q_fox, w_out_fox, w_up, w_dw, b_dw, w_down, w_ple, w_ple_gate):
    bp, tp, _ = x_prompt.shape
    nb, ts, _ = x_sample.shape
    assert ts == 1, "the sample group is one new token per sequence"
    past_len = page_table.shape[1] * cache_k.shape[1]
    depth = w_up.shape[0]
    row_vec = lambda a: a.reshape(a.shape[0], 1, a.shape[1])
    w = {
        "g_pre_mix": row_vec(norm_pre_mix), "g_post_mix": row_vec(norm_post_mix),
        "g_pre_ffn": row_vec(norm_pre_ffn), "g_post_ffn": row_vec(norm_post_ffn),
        "g_kv": norm_kv.reshape(1, D_MODEL),
        "w_in": w_in_ret[0].astype(BF16), "w_out_ret": w_out_ret[0].astype(BF16),
        "w_kv": w_kvf[:, :2 * FOX_HD].astype(BF16),
        "w_f": jnp.pad(w_kvf[:, 2 * FOX_HD:], ((0, 0), (0, LANES - FOX_HEADS))).astype(BF16),
        "b_f": jnp.pad(b_f, (0, LANES - FOX_HEADS)).reshape(1, LANES),
        "w_q": w_q_fox[0].astype(BF16), "w_out_fox": w_out_fox[0].astype(BF16),
        "w_up": w_up.astype(BF16), "w_dw": w_dw, "b_dw": row_vec(b_dw), "w_down": w_down.astype(BF16),
        "w_ple": w_ple.astype(BF16), "w_gate": w_ple_gate.astype(BF16),
        "log_gamma": jnp.log1p(-jnp.exp2(-5.0 - jnp.arange(RET_HEADS, dtype=F32))),
    }
    bm = min(ROW_BLOCK, tp)
    cos_p, sin_p = _rope_table(tp, 0)
    y_p, ret_p, conv_p, k_p, v_p, lf_p = _run_group(
        x_prompt.reshape(bp * tp, D_MODEL), p_prompt.reshape(depth, bp * tp, PLE_DIM), cos_p, sin_p, w, bm, tp, None)
    cos_s, sin_s = _rope_table(SUBLANES, past_len)
    cos_s = jnp.broadcast_to(cos_s[:1], (nb, ROPE_HALF))
    sin_s = jnp.broadcast_to(sin_s[:1], (nb, ROPE_HALF))
    sample = {"state_ret": state_ret[0], "state_conv": state_conv, "cache_k": cache_k, "cache_v": cache_v,
              "cache_logf": cache_logf, "page_table": page_table}
    y_s, ret_s, conv_s, k_s, v_s, lf_s = _run_group(
        x_sample.reshape(nb, D_MODEL), p_sample.reshape(depth, nb, PLE_DIM), cos_s, sin_s, w, nb, 1, sample)
    return (y_p.reshape(bp, tp, D_MODEL), y_s.reshape(nb, 1, D_MODEL),
            ret_p[None], ret_s[None], conv_p, conv_s,
            k_p.reshape(bp, tp, FOX_HEADS, FOX_DH), k_s.reshape(nb, 1, FOX_HEADS, FOX_DH),
            v_p.reshape(bp, tp, FOX_HEADS, FOX_DH), v_s.reshape(nb, 1, FOX_HEADS, FOX_DH),
            lf_p.reshape(bp, tp, FOX_HEADS), lf_s.reshape(nb, 1, FOX_HEADS))
```

```python
import functools
import math

import jax
import jax.numpy as jnp
from jax import lax
from jax.experimental import pallas as pl
from jax.experimental.pallas import tpu as pltpu

F32 = jnp.float32
BF16 = jnp.bfloat16

D_MODEL = 1024
RET_HEADS = 4
RET_DK = 256
RET_DV = 512
RET_QK = RET_HEADS * RET_DK
RET_VD = RET_HEADS * RET_DV
ROPE_BASE = 10000.0
ROPE_HALF = RET_DK // 2
FOX_HEADS = 16
FOX_DH = 64
FOX_HD = FOX_HEADS * FOX_DH
D_FF = 2816
PLE_DIM = 256
NORM_EPS = 1e-6

LANES = 128
SUBLANES = 8
VMEM_LIMIT = 56 * 1024 * 1024

ROW_BLOCK = 512
RET_CHUNK = 256
FFN_CHUNK = 256
ATT_BLOCK = 512
PAGES_PER_STEP = 8
BIAS_ONES = 48
EXP_UNDERFLOW = 110.0
NORM_PAD = 1.05


def _cparams(sem):
    return pltpu.CompilerParams(dimension_semantics=sem, vmem_limit_bytes=VMEM_LIMIT)


def _resident(shape):
    return pl.BlockSpec(shape, lambda *_: (0,) * len(shape), pipeline_mode=pl.Buffered(1))


def _dot(a, b):
    return jnp.dot(a, b, preferred_element_type=F32)


def _dot_nt(a, b):
    return lax.dot_general(a, b, (((1,), (1,)), ((), ())), preferred_element_type=F32)


def _dot_tn(a, b):
    return lax.dot_general(a, b, (((0,), (0,)), ((), ())), preferred_element_type=F32)


def _rms(x, g):
    return x * lax.rsqrt(jnp.mean(x * x, axis=-1, keepdims=True) + NORM_EPS) * g


def _sigmoid(x):
    return 1.0 / (1.0 + jnp.exp(-x))


def _gelu_tanh(x):
    return x * (0.5 * (1.0 + jnp.tanh(math.sqrt(2.0 / math.pi) * (x + 0.044715 * (x * x * x)))))


def _log_sigmoid(x):
    z = -x
    return -(jnp.maximum(z, 0.0) + jnp.log1p(jnp.exp(-jnp.abs(z))))


def _split3(x):
    hi = x.astype(BF16)
    r1 = x - hi.astype(F32)
    mid = r1.astype(BF16)
    lo = (r1 - mid.astype(F32)).astype(BF16)
    return hi, mid, lo


def _rope_kernel(cos_ref, sin_ref, *, base_pos):
    n = cos_ref.shape[0]
    i = pl.program_id(0)
    lane = lax.broadcasted_iota(jnp.int32, (1, ROPE_HALF), 1).astype(F32)
    inv = jnp.power(jnp.full((1, ROPE_HALF), ROPE_BASE, F32), -(lane / ROPE_HALF))
    pos = (lax.broadcasted_iota(jnp.int32, (n, ROPE_HALF), 0) + (i * n + base_pos)).astype(F32)
    ang = pos * inv
    cos_ref[...] = jnp.cos(ang)
    sin_ref[...] = jnp.sin(ang)


def _rope_table(n_rows, base_pos):
    bn = min(n_rows, 1024)
    spec = pl.BlockSpec((bn, ROPE_HALF), lambda i: (i, 0))
    return pl.pallas_call(
        functools.partial(_rope_kernel, base_pos=base_pos),
        grid=(n_rows // bn,),
        out_specs=[spec, spec],
        out_shape=[jax.ShapeDtypeStruct((n_rows, ROPE_HALF), F32)] * 2,
        compiler_params=_cparams(("arbitrary",)),
        name="rope_table",
    )()


def _ret_in_kernel(x_ref, g_ref, w_ref, cos_ref, sin_ref, q_ref, k_ref, v_ref, gate_ref):
    xn = _rms(x_ref[...], g_ref[...]).astype(BF16)
    cos = cos_ref[...]
    sin = sin_ref[...]
    for out_ref, base, scale in ((q_ref, 0, 1.0), (k_ref, RET_QK, RET_DK ** -0.5)):
        for h in range(RET_HEADS):
            lo = h * RET_DK
            u = _dot(xn, w_ref[:, base + lo:base + lo + RET_DK])
            x1, x2 = u[:, :ROPE_HALF], u[:, ROPE_HALF:]
            out_ref[:, lo:lo + ROPE_HALF] = ((x1 * cos - x2 * sin) * scale).astype(out_ref.dtype)
            out_ref[:, lo + ROPE_HALF:lo + RET_DK] = ((x1 * sin + x2 * cos) * scale).astype(out_ref.dtype)
    for h in range(RET_HEADS):
        lo = h * RET_DV
        v_ref[:, lo:lo + RET_DV] = _dot(xn, w_ref[:, 2 * RET_QK + lo:2 * RET_QK + lo + RET_DV]).astype(v_ref.dtype)
        gate_ref[:, lo:lo + RET_DV] = _dot(xn, w_ref[:, 2 * RET_QK + RET_VD + lo:2 * RET_QK + RET_VD + lo + RET_DV])


def _ret_in(x, g, w_bf, cos, sin, bm, qkv_dtype):
    rows = x.shape[0]
    n_tab = cos.shape[0] // bm
    row = lambda i: (i, 0)
    tab = pl.BlockSpec((bm, ROPE_HALF), lambda i: (i % n_tab, 0))
    return pl.pallas_call(
        _ret_in_kernel,
        grid=(rows // bm,),
        in_specs=[pl.BlockSpec((bm, D_MODEL), row), _resident((1, D_MODEL)), _resident(w_bf.shape), tab, tab],
        out_specs=[pl.BlockSpec((bm, RET_QK), row), pl.BlockSpec((bm, RET_QK), row),
                   pl.BlockSpec((bm, RET_VD), row), pl.BlockSpec((bm, RET_VD), row)],
        out_shape=[jax.ShapeDtypeStruct((rows, RET_QK), qkv_dtype), jax.ShapeDtypeStruct((rows, RET_QK), qkv_dtype),
                   jax.ShapeDtypeStruct((rows, RET_VD), qkv_dtype), jax.ShapeDtypeStruct((rows, RET_VD), F32)],
        compiler_params=_cparams(("arbitrary",)),
        name="ret_in",
    )(x, g, w_bf, cos, sin)


def _ret_chunk_kernel(lg_ref, q_ref, k_ref, v_ref, gate_ref, o_ref, s_ref, decay_sc):
    c = pl.program_id(1)
    C = q_ref.shape[0]

    @pl.when(jnp.logical_and(pl.program_id(0) == 0, c == 0))
    def _():
        rel = lax.broadcasted_iota(jnp.int32, (C, C), 0) - lax.broadcasted_iota(jnp.int32, (C, C), 1)
        dist = jnp.maximum(rel, 0).astype(F32)
        for h in range(RET_HEADS):
            decay_sc[h] = jnp.where(rel >= 0, jnp.exp(dist * lg_ref[h]), 0.0)

    @pl.when(c == 0)
    def _():
        s_ref[...] = jnp.zeros_like(s_ref)

    n = lax.broadcasted_iota(jnp.int32, (C, 1), 0).astype(F32)
    for h in range(RET_HEADS):
        lg = lg_ref[h]
        q = q_ref[:, h * RET_DK:(h + 1) * RET_DK]
        k = k_ref[:, h * RET_DK:(h + 1) * RET_DK]
        v = v_ref[:, h * RET_DV:(h + 1) * RET_DV]
        scores = _dot_nt(q, k) * decay_sc[h]
        inner = _dot(scores.astype(BF16), v)
        state = s_ref[0, h]
        cross = _dot(q, state.astype(BF16)) * jnp.exp((n + 1.0) * lg)
        kd = (k.astype(F32) * jnp.exp((C - 1.0 - n) * lg)).astype(BF16)
        chunk_decay = jnp.exp(jnp.full((1, RET_DV), float(C), F32) * lg)
        s_ref[0, h] = state * chunk_decay + _dot_tn(kd, v)
        o = inner + cross
        o = o * lax.rsqrt(jnp.mean(o * o, axis=-1, keepdims=True) + NORM_EPS)
        gate = gate_ref[:, h * RET_DV:(h + 1) * RET_DV]
        o_ref[:, h * RET_DV:(h + 1) * RET_DV] = (o * (gate * _sigmoid(gate))).astype(o_ref.dtype)


def _retention_prompt(q, k, v, gate, log_gamma, batch, seq):
    C = min(RET_CHUNK, seq)
    nc = seq // C
    blk = lambda b, c, lg: (b * nc + c, 0)
    return pl.pallas_call(
        _ret_chunk_kernel,
        grid_spec=pltpu.PrefetchScalarGridSpec(
            num_scalar_prefetch=1,
            grid=(batch, nc),
            in_specs=[pl.BlockSpec((C, RET_QK), blk), pl.BlockSpec((C, RET_QK), blk),
                      pl.BlockSpec((C, RET_VD), blk), pl.BlockSpec((C, RET_VD), blk)],
            out_specs=[pl.BlockSpec((C, RET_VD), blk),
                       pl.BlockSpec((1, RET_HEADS, RET_DK, RET_DV), lambda b, c, lg: (b, 0, 0, 0))],
            scratch_shapes=[pltpu.VMEM((RET_HEADS, C, C), F32)],
        ),
        out_shape=[jax.ShapeDtypeStruct((batch * seq, RET_VD), BF16),
                   jax.ShapeDtypeStruct((batch, RET_HEADS, RET_DK, RET_DV), F32)],
        compiler_params=_cparams(("arbitrary", "arbitrary")),
        name="retention_chunks",
    )(log_gamma, q, k, v, gate)


def _as_column(row):
    return jnp.transpose(jnp.broadcast_to(row, (LANES, LANES)))


def _ret_step_kernel(lg_ref, s_ref, q_ref, k_ref, v_ref, gate_ref, o_ref, so_ref):
    wide = lambda x: jnp.concatenate([x] * (RET_DV // LANES), axis=1)
    for h in range(RET_HEADS):
        gamma = jnp.exp(jnp.full((1, RET_DV), 1.0, F32) * lg_ref[h])
        state = s_ref[0, h]
        column = lambda ref: jnp.concatenate(
            [_as_column(ref[0, :, h * RET_DK + c * LANES:h * RET_DK + (c + 1) * LANES])
             for c in range(RET_DK // LANES)], axis=0)
        qc = column(q_ref)
        kc = column(k_ref)
        vr = v_ref[0, :, h * RET_DV:(h + 1) * RET_DV]
        qk = jnp.sum(qc * kc, axis=0, keepdims=True)
        qs = jnp.sum(wide(qc) * state, axis=0, keepdims=True)
        o = wide(qk) * vr + qs * gamma
        so_ref[0, h] = state * gamma + wide(kc) * vr
        o = o * lax.rsqrt(jnp.mean(o * o, axis=-1, keepdims=True) + NORM_EPS)
        gate = gate_ref[0, :, h * RET_DV:(h + 1) * RET_DV]
        o_ref[0, :, h * RET_DV:(h + 1) * RET_DV] = (o * (gate * _sigmoid(gate))).astype(o_ref.dtype)


def _retention_step(state, q, k, v, gate, log_gamma):
    nb = state.shape[0]
    st = pl.BlockSpec((1, RET_HEADS, RET_DK, RET_DV), lambda b, lg: (b, 0, 0, 0))
    rowk = pl.BlockSpec((1, 1, RET_QK), lambda b, lg: (b, 0, 0))
    rowv = pl.BlockSpec((1, 1, RET_VD), lambda b, lg: (b, 0, 0))
    o, s_new = pl.pallas_call(
        _ret_step_kernel,
        grid_spec=pltpu.PrefetchScalarGridSpec(
            num_scalar_prefetch=1, grid=(nb,),
            in_specs=[st, rowk, rowk, rowv, rowv], out_specs=[rowv, st]),
        out_shape=[jax.ShapeDtypeStruct((nb, 1, RET_VD), BF16), jax.ShapeDtypeStruct(state.shape, F32)],
        compiler_params=_cparams(("arbitrary",)),
        name="retention_step",
    )(log_gamma, state, q.reshape(nb, 1, RET_QK), k.reshape(nb, 1, RET_QK), v.reshape(nb, 1, RET_VD),
      gate.reshape(nb, 1, RET_VD))
    return o.reshape(nb, RET_VD), s_new


def _mix_out_kernel(o_ref, w_ref, x_ref, g_ref, y_ref):
    y_ref[...] = x_ref[...] + _rms(_dot(o_ref[...], w_ref[...]), g_ref[...])


def _mix_out(o_bf, w_bf, x, g, bm):
    rows, kdim = o_bf.shape
    row = lambda i: (i, 0)
    return pl.pallas_call(
        _mix_out_kernel,
        grid=(rows // bm,),
        in_specs=[pl.BlockSpec((bm, kdim), row), _resident(w_bf.shape), pl.BlockSpec((bm, D_MODEL), row),
                  _resident((1, D_MODEL))],
        out_specs=pl.BlockSpec((bm, D_MODEL), row),
        out_shape=jax.ShapeDtypeStruct((rows, D_MODEL), F32),
        compiler_params=_cparams(("arbitrary",)),
        name="mix_out",
    )(o_bf, w_bf, x, g)


def _ffn_kernel(*refs, seq_mode, blocks_per_seq):
    if seq_mode:
        (x_ref, gpre_ref, wup_ref, wdw_ref, bdw_ref, wdown_ref, gpost_ref, p_ref, wple_ref, wgate_ref,
         y_ref, tail_ref, h_sc, carry_sc) = refs
    else:
        (x_ref, gpre_ref, wup_ref, wdw_ref, bdw_ref, wdown_ref, gpost_ref, p_ref, wple_ref, wgate_ref,
         prev2_ref, prev1_ref, y_ref, a_ref, h_sc) = refs
    x = x_ref[...]
    bm = x.shape[0]
    xn = _rms(x, gpre_ref[...]).astype(BF16)
    if seq_mode:
        @pl.when(pl.program_id(0) % blocks_per_seq == 0)
        def _():
            carry_sc[...] = jnp.zeros_like(carry_sc)
        row = lax.broadcasted_iota(jnp.int32, (bm, FFN_CHUNK), 0)
    for c in range(D_FF // FFN_CHUNK):
        sl = slice(c * FFN_CHUNK, (c + 1) * FFN_CHUNK)
        a = _dot(xn, wup_ref[:, sl])
        b = _dot(xn, wup_ref[:, D_FF + c * FFN_CHUNK:D_FF + (c + 1) * FFN_CHUNK])
        if seq_mode:
            c2 = carry_sc[SUBLANES - 2:SUBLANES - 1, sl]
            c1 = carry_sc[SUBLANES - 1:SUBLANES, sl]
            a1 = jnp.where(row == 0, c1, pltpu.roll(a, 1, 0))
            a2 = jnp.where(row == 0, c2, jnp.where(row == 1, c1, pltpu.roll(a, 2, 0)))
            last = a[bm - SUBLANES:bm, :]
            carry_sc[:, sl] = last
            tail_ref[0, :, sl] = last
        else:
            a1 = prev1_ref[:, sl]
            a2 = prev2_ref[:, sl]
            a_ref[:, sl] = a
        conv = bdw_ref[:, sl] + a2 * wdw_ref[0:1, sl]
        conv = conv + a1 * wdw_ref[1:2, sl]
        conv = conv + a * wdw_ref[2:3, sl]
        h_sc[:, sl] = (_gelu_tanh(conv) * b).astype(BF16)
    f = _dot(h_sc[...], wdown_ref[...])
    x2 = x + _rms(f, gpost_ref[...])
    ple = _dot(p_ref[...].astype(BF16), wple_ref[...])
    gate = _dot(x2.astype(BF16), wgate_ref[...])
    y_ref[...] = x2 + ple * _sigmoid(gate)


def _ffn(x, gpre, wup_bf, wdw, bdw, wdown_bf, gpost, p, wple_bf, wgate_bf, bm, seq_len=None, prev=None):
    rows = x.shape[0]
    row = lambda i: (i, 0)
    seq_mode = prev is None
    in_specs = [pl.BlockSpec((bm, D_MODEL), row), _resident((1, D_MODEL)), _resident(wup_bf.shape),
                _resident(wdw.shape), _resident((1, D_FF)), _resident(wdown_bf.shape), _resident((1, D_MODEL)),
                pl.BlockSpec((bm, PLE_DIM), row), _resident(wple_bf.shape), _resident(wgate_bf.shape)]
    args = [x, gpre, wup_bf, wdw, bdw, wdown_bf, gpost, p, wple_bf, wgate_bf]
    scratch = [pltpu.VMEM((bm, D_FF), BF16)]
    if seq_mode:
        bps = seq_len // bm
        out_specs = [pl.BlockSpec((bm, D_MODEL), row), pl.BlockSpec((1, SUBLANES, D_FF), lambda i: (i // bps, 0, 0))]
        out_shape = [jax.ShapeDtypeStruct((rows, D_MODEL), F32),
                     jax.ShapeDtypeStruct((rows // seq_len, SUBLANES, D_FF), F32)]
        scratch.append(pltpu.VMEM((SUBLANES, D_FF), F32))
    else:
        bps = 1
        in_specs += [pl.BlockSpec((bm, D_FF), row), pl.BlockSpec((bm, D_FF), row)]
        args += list(prev)
        out_specs = [pl.BlockSpec((bm, D_MODEL), row), pl.BlockSpec((bm, D_FF), row)]
        out_shape = [jax.ShapeDtypeStruct((rows, D_MODEL), F32), jax.ShapeDtypeStruct((rows, D_FF), F32)]
    return pl.pallas_call(
        functools.partial(_ffn_kernel, seq_mode=seq_mode, blocks_per_seq=bps),
        grid=(rows // bm,),
        in_specs=in_specs, out_specs=out_specs, out_shape=out_shape, scratch_shapes=scratch,
        compiler_params=_cparams(("arbitrary",)),
        name="conv_ffn",
    )(*args)


def _kvq_kernel(x_ref, gkv_ref, wkv_ref, wf_ref, bf_ref, gq_ref, wq_ref,
                k_ref, v_ref, lf_ref, kb_ref, vb_ref, qb_ref, kn_ref, qn_ref):
    x = x_ref[...]
    xn = _rms(x, gkv_ref[...]).astype(BF16)
    k = _dot(xn, wkv_ref[:, :FOX_HD])
    v = _dot(xn, wkv_ref[:, FOX_HD:])
    k_ref[...] = k
    v_ref[...] = v
    kb = k.astype(BF16)
    kb_ref[...] = kb
    vb_ref[...] = v.astype(BF16)
    f = _dot(xn, wf_ref[...]) + bf_ref[...]
    lf_ref[...] = _log_sigmoid(f)[:, :FOX_HEADS]
    xq = _rms(x, gq_ref[...]).astype(BF16)
    qb = (_dot(xq, wq_ref[...]) * FOX_DH ** -0.5).astype(BF16)
    qb_ref[...] = qb
    head_sum = (lax.broadcasted_iota(jnp.int32, (FOX_HD, LANES), 0) // FOX_DH ==
                lax.broadcasted_iota(jnp.int32, (FOX_HD, LANES), 1))
    head_sum = jnp.where(head_sum, 1.0, 0.0).astype(BF16)
    for src, dst in ((kb, kn_ref), (qb, qn_ref)):
        sf = src.astype(F32)
        n2 = jnp.max(_dot((sf * sf).astype(BF16), head_sum), axis=0, keepdims=True)
        dst[0] = jnp.broadcast_to(n2, (SUBLANES, LANES))


def _kvq(x, gkv, wkv_bf, wf_bf, bf_pad, gq, wq_bf, bm):
    rows = x.shape[0]
    row = lambda i: (i, 0)
    wide = pl.BlockSpec((bm, FOX_HD), row)
    norm = pl.BlockSpec((1, SUBLANES, LANES), lambda i: (i, 0, 0))
    norm_shape = jax.ShapeDtypeStruct((rows // bm, SUBLANES, LANES), F32)
    return pl.pallas_call(
        _kvq_kernel,
        grid=(rows // bm,),
        in_specs=[pl.BlockSpec((bm, D_MODEL), row), _resident((1, D_MODEL)), _resident(wkv_bf.shape),
                  _resident(wf_bf.shape), _resident((1, LANES)), _resident((1, D_MODEL)), _resident(wq_bf.shape)],
        out_specs=[wide, wide, pl.BlockSpec((bm, FOX_HEADS), row), wide, wide, wide, norm, norm],
        out_shape=[jax.ShapeDtypeStruct((rows, FOX_HD), F32), jax.ShapeDtypeStruct((rows, FOX_HD), F32),
                   jax.ShapeDtypeStruct((rows, FOX_HEADS), F32), jax.ShapeDtypeStruct((rows, FOX_HD), BF16),
                   jax.ShapeDtypeStruct((rows, FOX_HD), BF16), jax.ShapeDtypeStruct((rows, FOX_HD), BF16),
                   norm_shape, norm_shape],
        compiler_params=_cparams(("arbitrary",)),
        name="kv_q_proj",
    )(x, gkv, wkv_bf, wf_bf, bf_pad, gq, wq_bf)


def _cumsum_kernel(lf_ref, cq_ref, ck_ref, first_ref, last_ref, carry_sc):
    bl = lf_ref.shape[0]

    @pl.when(pl.program_id(1) == 0)
    def _():
        carry_sc[...] = jnp.zeros_like(carry_sc)

    lf = lf_ref[...]
    tri = (lax.broadcasted_iota(jnp.int32, (bl, bl), 1) <= lax.broadcasted_iota(jnp.int32, (bl, bl), 0))
    tri = jnp.where(tri, 1.0, 0.0).astype(BF16)
    hi, mid, lo = _split3(lf)
    c = (_dot(tri, lo) + _dot(tri, mid)) + _dot(tri, hi) + carry_sc[...]
    carry_sc[...] = c[bl - 1:bl, :]
    first_ref[0] = c[0:1, :]
    last_ref[0] = c[bl - 1:bl, :]

    lane_of_head = lax.broadcasted_iota(jnp.int32, (FOX_HEADS, LANES), 1) - 3 * lax.broadcasted_iota(
        jnp.int32, (FOX_HEADS, LANES), 0)
    lane = lax.broadcasted_iota(jnp.int32, (1, LANES), 1)
    cq = jnp.where(lane < BIAS_ONES, 1.0, 0.0)
    ck = jnp.where(jnp.logical_and(lane >= BIAS_ONES, lane < 2 * BIAS_ONES), 1.0, 0.0)
    for piece, part in enumerate(_split3(c)):
        cq = cq + _dot(part, jnp.where(lane_of_head == BIAS_ONES + piece, 1.0, 0.0).astype(BF16))
        ck = ck + _dot(part, jnp.where(lane_of_head == piece, -1.0, 0.0).astype(BF16))
    cq_ref[...] = cq.astype(BF16)
    ck_ref[...] = ck.astype(BF16)


def _cumsum_time(lf, batch, seq):
    bl = min(ATT_BLOCK, seq)
    nb = seq // bl
    blk = lambda b, i: (b * nb + i, 0)
    wide = pl.BlockSpec((bl, LANES), blk)
    edge = pl.BlockSpec((1, 1, FOX_HEADS), lambda b, i: (b * nb + i, 0, 0))
    return pl.pallas_call(
        _cumsum_kernel,
        grid=(batch, nb),
        in_specs=[pl.BlockSpec((bl, FOX_HEADS), blk)], out_specs=[wide, wide, edge, edge],
        out_shape=[jax.ShapeDtypeStruct((lf.shape[0], LANES), BF16)] * 2
        + [jax.ShapeDtypeStruct((batch * nb, 1, FOX_HEADS), F32)] * 2,
        scratch_shapes=[pltpu.VMEM((1, FOX_HEADS), F32)],
        compiler_params=_cparams(("arbitrary", "arbitrary")),
        name="logf_cumsum",
    )(lf)


def _fox_prompt_kernel(cfirst_ref, clast_ref, qkb_ref, q_ref, k_ref, v_ref, cq_ref, ck_ref, o_ref, m_sc, acc_sc):
    b = pl.program_id(0)
    hp = pl.program_id(1)
    i = pl.program_id(2)

    def dead(j):
        worst = None
        for hl in range(2):
            h = 2 * hp + hl
            gap = cfirst_ref[b, i, h] - clast_ref[b, j, h] + 2.0 * qkb_ref[b, h]
            worst = gap if worst is None else jnp.maximum(worst, gap)
        return worst < -EXP_UNDERFLOW

    j_lo = lax.while_loop(lambda j: jnp.logical_and(j < i, dead(j)), lambda j: j + 1, jnp.int32(0))
    bq = q_ref.shape[0]
    pair = 2 * FOX_DH
    q = q_ref[...]
    low = lax.broadcasted_iota(jnp.int32, (1, pair), 1) < FOX_DH
    causal = (lax.broadcasted_iota(jnp.int32, (bq, bq), 1) <= lax.broadcasted_iota(jnp.int32, (bq, bq), 0))
    ones = jnp.ones((bq, pair), BF16)
    lanes = lambda x, n: jnp.concatenate([x] * n, axis=1)
    bias_lane = lax.broadcasted_iota(jnp.int32, (1, LANES), 1)
    cq = cq_ref[...]
    q_aug = []
    for hl in range(2):
        qm = jnp.where(low if hl == 0 else jnp.logical_not(low), q, jnp.zeros_like(q))
        first = 3 * (2 * hp + hl)
        own = jnp.logical_or(jnp.logical_and(bias_lane >= first, bias_lane < first + 3),
                             jnp.logical_and(bias_lane >= first + BIAS_ONES, bias_lane < first + BIAS_ONES + 3))
        q_aug.append(jnp.concatenate([qm, jnp.where(own, cq, jnp.zeros_like(cq))], axis=1))
        m_sc[hl] = jnp.full(m_sc.shape[1:], -jnp.inf, F32)
        acc_sc[hl] = jnp.zeros(acc_sc.shape[1:], F32)

    def block(j, masked):
        start = pl.multiple_of(j * bq, bq)
        k_aug = jnp.concatenate([k_ref[pl.ds(start, bq), :], ck_ref[pl.ds(start, bq), :]], axis=1)
        v_aug = jnp.concatenate([v_ref[pl.ds(start, bq), :], ones], axis=1)
        for hl in range(2):
            s = _dot_nt(q_aug[hl], k_aug)
            if masked:
                s = jnp.where(causal, s, -jnp.inf)
            m_prev = m_sc[hl]
            m_next = jnp.maximum(m_prev, jnp.max(s, axis=1, keepdims=True))
            alpha = jnp.exp(m_prev - m_next)
            p = jnp.exp(s - lanes(m_next, bq // LANES))
            acc_sc[hl] = lanes(alpha, 2) * acc_sc[hl] + _dot(p.astype(BF16), v_aug)
            m_sc[hl] = m_next

    n_live = i - j_lo + 1

    @pl.when(n_live == 1)
    def _():
        block(i, True)

    @pl.when(n_live == 2)
    def _():
        block(i, True)
        block(i - 1, False)

    @pl.when(n_live >= 3)
    def _():
        block(i, True)
        block(i - 1, False)
        block(i - 2, False)

    def body(t, carry):
        block(i - 3 - 2 * t, False)
        block(i - 4 - 2 * t, False)
        return carry

    rest = jnp.maximum(n_live - 3, 0)
    lax.fori_loop(0, rest // 2, body, 0)

    @pl.when(rest % 2 == 1)
    def _():
        block(j_lo, False)

    o0 = acc_sc[0, :, :pair] / acc_sc[0, :, pair:]
    o1 = acc_sc[1, :, :pair] / acc_sc[1, :, pair:]
    o_ref[...] = jnp.where(low, o0, o1).astype(o_ref.dtype)


def _fox_prompt(q_bf, k_bf, v_bf, cq, ck, c_first, c_last, qk_bound, batch, seq):
    bq = min(ATT_BLOCK, seq)
    nq = seq // bq
    pair = 2 * FOX_DH
    qblk = pl.BlockSpec((bq, pair), lambda b, hp, i, *_: (b * nq + i, hp))
    kvblk = pl.BlockSpec((seq, pair), lambda b, hp, i, *_: (b, hp))
    return pl.pallas_call(
        _fox_prompt_kernel,
        grid_spec=pltpu.PrefetchScalarGridSpec(
            num_scalar_prefetch=3,
            grid=(batch, FOX_HEADS // 2, nq),
            in_specs=[qblk, kvblk, kvblk,
                      pl.BlockSpec((bq, LANES), lambda b, hp, i, *_: (b * nq + i, 0)),
                      pl.BlockSpec((seq, LANES), lambda b, hp, i, *_: (b, 0))],
            out_specs=qblk,
            scratch_shapes=[pltpu.VMEM((2, bq, LANES), F32), pltpu.VMEM((2, bq, 2 * pair), F32)]),
        out_shape=jax.ShapeDtypeStruct((batch * seq, FOX_HD), BF16),
        compiler_params=_cparams(("arbitrary", "arbitrary", "arbitrary")),
        name="fox_attention_prompt",
    )(c_first, c_last, qk_bound, q_bf, k_bf, v_bf, cq, ck)


def _fox_decode_kernel(pt_ref, q_ref, knew_ref, vnew_ref, lfnew_ref, *refs):
    pp = PAGES_PER_STEP
    k_refs, v_refs, lf_refs = refs[:pp], refs[pp:2 * pp], refs[2 * pp:3 * pp]
    o_ref, q_sc, m_sc, l_sc, acc_sc, carry_sc = refs[3 * pp:]
    step = pl.program_id(1)
    page = k_refs[0].shape[3]

    @pl.when(step == 0)
    def _():
        lane = lax.broadcasted_iota(jnp.int32, (FOX_DH, page), 1)
        for hp in range(FOX_HEADS // 2):
            cols = [_as_column(r[0, :, hp * LANES:(hp + 1) * LANES]) for r in (q_ref, knew_ref, vnew_ref)]
            for hl in range(2):
                h = 2 * hp + hl
                qh, kh, vh = [c[hl * FOX_DH:(hl + 1) * FOX_DH] for c in cols]
                q_sc[h] = qh
                m_sc[h:h + 1, :] = jnp.sum(qh * kh, axis=0, keepdims=True)[:, :1]
                acc_sc[h] = jnp.where(lane == 0, vh, 0.0)
        l_sc[...] = jnp.ones_like(l_sc)
        carry_sc[...] = _as_column(lfnew_ref[0])[:FOX_HEADS]

    q = q_sc[...]

    later = (lax.broadcasted_iota(jnp.int32, (page, page), 0) > lax.broadcasted_iota(jnp.int32, (page, page), 1))
    later = jnp.where(later, 1.0, 0.0).astype(BF16)
    carry = carry_sc[...]
    scores = []
    for i in range(pp):
        lf = lf_refs[i][0]
        hi, mid, lo = _split3(lf)
        bias = (_dot(lo, later) + _dot(mid, later)) + _dot(hi, later) + carry
        carry = carry + jnp.sum(lf, axis=1, keepdims=True)
        scores.append(jnp.sum(k_refs[i][0] * q, axis=1) + bias)
    carry_sc[...] = carry
    m_prev = m_sc[...]
    m_new = m_prev
    for s in scores:
        m_new = jnp.maximum(m_new, jnp.max(s, axis=1, keepdims=True))
    alpha = jnp.exp(m_prev - m_new)
    l_new = alpha * l_sc[...]
    probs = []
    for s in scores:
        p = jnp.exp(s - m_new)
        l_new = l_new + jnp.sum(p, axis=1, keepdims=True)
        probs.append(p)
    l_sc[...] = l_new
    m_sc[...] = m_new
    for h in range(FOX_HEADS):
        acc = acc_sc[h] * alpha[h:h + 1, :]
        for i in range(pp):
            acc = acc + v_refs[i][0, h] * probs[i][h:h + 1, :]
        acc_sc[h] = acc

    @pl.when(step == pl.num_programs(1) - 1)
    def _():
        o_ref[0] = (jnp.sum(acc_sc[...], axis=2) / l_new).astype(o_ref.dtype)


def _fox_decode(q, k_new, v_new, lf_new, cache_k, cache_v, cache_logf, page_table):
    nb, n_pages = page_table.shape
    page = cache_k.shape[1]
    pp = PAGES_PER_STEP
    n_steps = n_pages // pp
    ck = jnp.transpose(cache_k, (0, 2, 3, 1))
    cv = jnp.transpose(cache_v, (0, 2, 3, 1))
    clf = jnp.transpose(cache_logf, (0, 2, 1))
    def kv_page(i):
        return pl.BlockSpec((1, FOX_HEADS, FOX_DH, page),
                            lambda b, s, pt: (pt[b, n_pages - 1 - (s * pp + i)], 0, 0, 0))

    def lf_page(i):
        return pl.BlockSpec((1, FOX_HEADS, page), lambda b, s, pt: (pt[b, n_pages - 1 - (s * pp + i)], 0, 0))

    assert page == LANES, "one cache page fills the lane axis"
    row = pl.BlockSpec((1, 1, FOX_HD), lambda b, s, pt: (b, 0, 0))
    in_specs = [row, row, row, pl.BlockSpec((1, 1, LANES), lambda b, s, pt: (b, 0, 0))]
    in_specs += [kv_page(i) for i in range(pp)] * 2 + [lf_page(i) for i in range(pp)]
    as_row = lambda a: a.reshape(nb, 1, FOX_HD)
    lf_row = jnp.pad(lf_new, ((0, 0), (0, LANES - FOX_HEADS))).reshape(nb, 1, LANES)
    out = pl.pallas_call(
        _fox_decode_kernel,
        grid_spec=pltpu.PrefetchScalarGridSpec(
            num_scalar_prefetch=1, grid=(nb, n_steps), in_specs=in_specs,
            out_specs=pl.BlockSpec((1, FOX_HEADS, FOX_DH), lambda b, s, pt: (b, 0, 0)),
            scratch_shapes=[pltpu.VMEM((FOX_HEADS, FOX_DH, page), F32), pltpu.VMEM((FOX_HEADS, 1), F32),
                            pltpu.VMEM((FOX_HEADS, 1), F32), pltpu.VMEM((FOX_HEADS, FOX_DH, page), F32),
                            pltpu.VMEM((FOX_HEADS, page), F32)]),
        out_shape=jax.ShapeDtypeStruct((nb, FOX_HEADS, FOX_DH), BF16),
        compiler_params=_cparams(("arbitrary", "arbitrary")),
        name="fox_attention_decode",
    )(page_table, as_row(q), as_row(k_new), as_row(v_new), lf_row, *([ck] * pp), *([cv] * pp), *([clf] * pp))
    return out.reshape(nb, FOX_HD)


def _run_group(x, p, cos, sin, w, bm, seq_len, sample):
    rows = x.shape[0]
    qkv_dtype = BF16 if sample is None else F32
    q, k, v, gate = _ret_in(x, w["g_pre_mix"][0], w["w_in"], cos, sin, bm, qkv_dtype)
    if sample is None:
        batch = rows // seq_len
        o, ret_state = _retention_prompt(q, k, v, gate, w["log_gamma"], batch, seq_len)
    else:
        o, ret_state = _retention_step(sample["state_ret"], q, k, v, gate, w["log_gamma"])
    x = _mix_out(o, w["w_out_ret"], x, w["g_post_mix"][0], bm)

    conv_out = []
    ffn_args = lambda i: (w["g_pre_ffn"][i], w["w_up"][i], w["w_dw"][i], w["b_dw"][i], w["w_down"][i],
                          w["g_post_ffn"][i], p[i], w["w_ple"][i], w["w_gate"][i], bm)
    if sample is None:
        x, tail = _ffn(x, *ffn_args(0), seq_len=seq_len)
        conv_out.append(tail[:, SUBLANES - 2:, :])
    else:
        buf = sample["state_conv"][0]
        x, a = _ffn(x, *ffn_args(0), prev=(buf[:, 0], buf[:, 1]))
        conv_out.append(jnp.stack([buf[:, 1], a], axis=1))

    k_new, v_new, lf_new, k_bf, v_bf, q_bf, k_n2, q_n2 = _kvq(
        x, w["g_kv"], w["w_kv"], w["w_f"], w["b_f"], w["g_pre_mix"][1], w["w_q"], bm)
    if sample is None:
        cq, ck, c_first, c_last = _cumsum_time(lf_new, batch, seq_len)
        per_seq_max = lambda n2: jnp.max(n2.reshape(batch, -1, SUBLANES, LANES)[:, :, 0, :FOX_HEADS], axis=1)
        qk_bound = jnp.sqrt(per_seq_max(k_n2) * per_seq_max(q_n2)) * NORM_PAD
        edges = lambda e: e.reshape(batch, -1, FOX_HEADS)
        att = _fox_prompt(q_bf, k_bf, v_bf, cq, ck, edges(c_first), edges(c_last), qk_bound, batch, seq_len)
    else:
        att = _fox_decode(q_bf.astype(F32), k_new, v_new, lf_new, sample["cache_k"], sample["cache_v"],
                          sample["cache_logf"], sample["page_table"])
    x = _mix_out(att, w["w_out_fox"], x, w["g_post_mix"][1], bm)

    if sample is None:
        x, tail = _ffn(x, *ffn_args(1), seq_len=seq_len)
        conv_out.append(tail[:, SUBLANES - 2:, :])
    else:
        buf = sample["state_conv"][1]
        x, a = _ffn(x, *ffn_args(1), prev=(buf[:, 0], buf[:, 1]))
        conv_out.append(jnp.stack([buf[:, 1], a], axis=1))
    return x, ret_state, jnp.stack(conv_out), k_new, v_new, lf_new


def kernel(x_prompt, x_sample, state_ret, state_conv, cache_k, cache_v, cache_logf, page_table, p_prompt, p_sample,
           norm_pre_mix, norm_post_mix, norm_pre_ffn, norm_post_ffn, w_in_ret, w_out_ret, norm_kv, w_kvf, b_f,
           w_q_fox, w_out_fox, w_up, w_dw, b_dw, w_down, w_ple, w_ple_gate):
    bp, tp, _ = x_prompt.shape
    nb, ts, _ = x_sample.shape
    assert ts == 1, "the sample group is one new token per sequence"
    past_len = page_table.shape[1] * cache_k.shape[1]
    depth = w_up.shape[0]
    row_vec = lambda a: a.reshape(a.shape[0], 1, a.shape[1])
    w = {
        "g_pre_mix": row_vec(norm_pre_mix), "g_post_mix": row_vec(norm_post_mix),
        "g_pre_ffn": row_vec(norm_pre_ffn), "g_post_ffn": row_vec(norm_post_ffn),
        "g_kv": norm_kv.reshape(1, D_MODEL),
        "w_in": w_in_ret[0].astype(BF16), "w_out_ret": w_out_ret[0].astype(BF16),
        "w_kv": w_kvf[:, :2 * FOX_HD].astype(BF16),
        "w_f": jnp.pad(w_kvf[:, 2 * FOX_HD:], ((0, 0), (0, LANES - FOX_HEADS))).astype(BF16),
        "b_f": jnp.pad(b_f, (0, LANES - FOX_HEADS)).reshape(1, LANES),
        "w_q": w_q_fox[0].astype(BF16), "w_out_fox": w_out_fox[0].astype(BF16),
        "w_up": w_up.astype(BF16), "w_dw": w_dw, "b_dw": row_vec(b_dw), "w_down": w_down.astype(BF16),
        "w_ple": w_ple.astype(BF16), "w_gate": w_ple_gate.astype(BF16),
        "log_gamma": jnp.log1p(-jnp.exp2(-5.0 - jnp.arange(RET_HEADS, dtype=F32))),
    }
    bm = min(ROW_BLOCK, tp)
    cos_p, sin_p = _rope_table(tp, 0)
    y_p, ret_p, conv_p, k_p, v_p, lf_p = _run_group(
        x_prompt.reshape(bp * tp, D_MODEL), p_prompt.reshape(depth, bp * tp, PLE_DIM), cos_p, sin_p, w, bm, tp, None)
    cos_s, sin_s = _rope_table(SUBLANES, past_len)
    cos_s = jnp.broadcast_to(cos_s[:1], (nb, ROPE_HALF))
    sin_s = jnp.broadcast_to(sin_s[:1], (nb, ROPE_HALF))
    sample = {"state_ret": state_ret[0], "state_conv": state_conv, "cache_k": cache_k, "cache_v": cache_v,
              "cache_logf": cache_logf, "page_table": page_table}
    y_s, ret_s, conv_s, k_s, v_s, lf_s = _run_group(
        x_sample.reshape(nb, D_MODEL), p_sample.reshape(depth, nb, PLE_DIM), cos_s, sin_s, w, nb, 1, sample)
    return (y_p.reshape(bp, tp, D_MODEL), y_s.reshape(nb, 1, D_MODEL),
            ret_p[None], ret_s[None], conv_p, conv_s,
            k_p.reshape(bp, tp, FOX_HEADS, FOX_DH), k_s.reshape(nb, 1, FOX_HEADS, FOX_DH),
            v_p.reshape(bp, tp, FOX_HEADS, FOX_DH), v_s.reshape(nb, 1, FOX_HEADS, FOX_DH),
            lf_p.reshape(bp, tp, FOX_HEADS), lf_s.reshape(nb, 1, FOX_HEADS))
```

```python
import functools
import math

import jax
import jax.numpy as jnp
from jax import lax
from jax.experimental import pallas as pl
from jax.experimental.pallas import tpu as pltpu

F32 = jnp.float32
BF16 = jnp.bfloat16

D_MODEL = 1024
RET_HEADS = 4
RET_DK = 256
RET_DV = 512
RET_QK = RET_HEADS * RET_DK
RET_VD = RET_HEADS * RET_DV
ROPE_BASE = 10000.0
ROPE_HALF = RET_DK // 2
FOX_HEADS = 16
FOX_DH = 64
FOX_HD = FOX_HEADS * FOX_DH
D_FF = 2816
PLE_DIM = 256
NORM_EPS = 1e-6

LANES = 128
SUBLANES = 8
VMEM_LIMIT = 56 * 1024 * 1024

ROW_BLOCK = 512
RET_CHUNK = 256
FFN_CHUNK = 256
ATT_BLOCK = 512
PAGES_PER_STEP = 16
BIAS_ONES = 48
EXP_UNDERFLOW = 110.0
NORM_PAD = 1.05


def _cparams(sem):
    return pltpu.CompilerParams(dimension_semantics=sem, vmem_limit_bytes=VMEM_LIMIT)


def _resident(shape):
    return pl.BlockSpec(shape, lambda *_: (0,) * len(shape), pipeline_mode=pl.Buffered(1))


def _dot(a, b):
    return jnp.dot(a, b, preferred_element_type=F32)


def _dot_nt(a, b):
    return lax.dot_general(a, b, (((1,), (1,)), ((), ())), preferred_element_type=F32)


def _dot_tn(a, b):
    return lax.dot_general(a, b, (((0,), (0,)), ((), ())), preferred_element_type=F32)


def _rms(x, g):
    return x * lax.rsqrt(jnp.mean(x * x, axis=-1, keepdims=True) + NORM_EPS) * g


def _sigmoid(x):
    return 1.0 / (1.0 + jnp.exp(-x))


def _gelu_tanh(x):
    return x * (0.5 * (1.0 + jnp.tanh(math.sqrt(2.0 / math.pi) * (x + 0.044715 * (x * x * x)))))


def _log_sigmoid(x):
    z = -x
    return -(jnp.maximum(z, 0.0) + jnp.log1p(jnp.exp(-jnp.abs(z))))


def _split3(x):
    hi = x.astype(BF16)
    r1 = x - hi.astype(F32)
    mid = r1.astype(BF16)
    lo = (r1 - mid.astype(F32)).astype(BF16)
    return hi, mid, lo


def _rope_kernel(cos_ref, sin_ref, *, base_pos):
    n = cos_ref.shape[0]
    i = pl.program_id(0)
    lane = lax.broadcasted_iota(jnp.int32, (1, ROPE_HALF), 1).astype(F32)
    inv = jnp.power(jnp.full((1, ROPE_HALF), ROPE_BASE, F32), -(lane / ROPE_HALF))
    pos = (lax.broadcasted_iota(jnp.int32, (n, ROPE_HALF), 0) + (i * n + base_pos)).astype(F32)
    ang = pos * inv
    cos_ref[...] = jnp.cos(ang)
    sin_ref[...] = jnp.sin(ang)


def _rope_table(n_rows, base_pos):
    bn = min(n_rows, 1024)
    spec = pl.BlockSpec((bn, ROPE_HALF), lambda i: (i, 0))
    return pl.pallas_call(
        functools.partial(_rope_kernel, base_pos=base_pos),
        grid=(n_rows // bn,),
        out_specs=[spec, spec],
        out_shape=[jax.ShapeDtypeStruct((n_rows, ROPE_HALF), F32)] * 2,
        compiler_params=_cparams(("arbitrary",)),
        name="rope_table",
    )()


def _ret_in_kernel(x_ref, g_ref, w_ref, cos_ref, sin_ref, q_ref, k_ref, v_ref, gate_ref):
    xn = _rms(x_ref[...], g_ref[...]).astype(BF16)
    cos = cos_ref[...]
    sin = sin_ref[...]
    for out_ref, base, scale in ((q_ref, 0, 1.0), (k_ref, RET_QK, RET_DK ** -0.5)):
        for h in range(RET_HEADS):
            lo = h * RET_DK
            u = _dot(xn, w_ref[:, base + lo:base + lo + RET_DK])
            x1, x2 = u[:, :ROPE_HALF], u[:, ROPE_HALF:]
            out_ref[:, lo:lo + ROPE_HALF] = ((x1 * cos - x2 * sin) * scale).astype(out_ref.dtype)
            out_ref[:, lo + ROPE_HALF:lo + RET_DK] = ((x1 * sin + x2 * cos) * scale).astype(out_ref.dtype)
    for h in range(RET_HEADS):
        lo = h * RET_DV
        v_ref[:, lo:lo + RET_DV] = _dot(xn, w_ref[:, 2 * RET_QK + lo:2 * RET_QK + lo + RET_DV]).astype(v_ref.dtype)
        gate_ref[:, lo:lo + RET_DV] = _dot(xn, w_ref[:, 2 * RET_QK + RET_VD + lo:2 * RET_QK + RET_VD + lo + RET_DV])


def _ret_in(x, g, w_bf, cos, sin, bm, qkv_dtype):
    rows = x.shape[0]
    n_tab = cos.shape[0] // bm
    row = lambda i: (i, 0)
    tab = pl.BlockSpec((bm, ROPE_HALF), lambda i: (i % n_tab, 0))
    return pl.pallas_call(
        _ret_in_kernel,
        grid=(rows // bm,),
        in_specs=[pl.BlockSpec((bm, D_MODEL), row), _resident((1, D_MODEL)), _resident(w_bf.shape), tab, tab],
        out_specs=[pl.BlockSpec((bm, RET_QK), row), pl.BlockSpec((bm, RET_QK), row),
                   pl.BlockSpec((bm, RET_VD), row), pl.BlockSpec((bm, RET_VD), row)],
        out_shape=[jax.ShapeDtypeStruct((rows, RET_QK), qkv_dtype), jax.ShapeDtypeStruct((rows, RET_QK), qkv_dtype),
                   jax.ShapeDtypeStruct((rows, RET_VD), qkv_dtype), jax.ShapeDtypeStruct((rows, RET_VD), F32)],
        compiler_params=_cparams(("arbitrary",)),
        name="ret_in",
    )(x, g, w_bf, cos, sin)


def _ret_chunk_kernel(lg_ref, q_ref, k_ref, v_ref, gate_ref, o_ref, s_ref, decay_sc):
    c = pl.program_id(1)
    C = q_ref.shape[0]

    @pl.when(jnp.logical_and(pl.program_id(0) == 0, c == 0))
    def _():
        rel = lax.broadcasted_iota(jnp.int32, (C, C), 0) - lax.broadcasted_iota(jnp.int32, (C, C), 1)
        dist = jnp.maximum(rel, 0).astype(F32)
        for h in range(RET_HEADS):
            decay_sc[h] = jnp.where(rel >= 0, jnp.exp(dist * lg_ref[h]), 0.0)

    @pl.when(c == 0)
    def _():
        s_ref[...] = jnp.zeros_like(s_ref)

    n = lax.broadcasted_iota(jnp.int32, (C, 1), 0).astype(F32)
    for h in range(RET_HEADS):
        lg = lg_ref[h]
        q = q_ref[:, h * RET_DK:(h + 1) * RET_DK]
        k = k_ref[:, h * RET_DK:(h + 1) * RET_DK]
        v = v_ref[:, h * RET_DV:(h + 1) * RET_DV]
        scores = _dot_nt(q, k) * decay_sc[h]
        inner = _dot(scores.astype(BF16), v)
        state = s_ref[0, h]
        cross = _dot(q, state.astype(BF16)) * jnp.exp((n + 1.0) * lg)
        kd = (k.astype(F32) * jnp.exp((C - 1.0 - n) * lg)).astype(BF16)
        chunk_decay = jnp.exp(jnp.full((1, RET_DV), float(C), F32) * lg)
        s_ref[0, h] = state * chunk_decay + _dot_tn(kd, v)
        o = inner + cross
        o = o * lax.rsqrt(jnp.mean(o * o, axis=-1, keepdims=True) + NORM_EPS)
        gate = gate_ref[:, h * RET_DV:(h + 1) * RET_DV]
        o_ref[:, h * RET_DV:(h + 1) * RET_DV] = (o * (gate * _sigmoid(gate))).astype(o_ref.dtype)


def _retention_prompt(q, k, v, gate, log_gamma, batch, seq):
    C = min(RET_CHUNK, seq)
    nc = seq // C
    blk = lambda b, c, lg: (b * nc + c, 0)
    return pl.pallas_call(
        _ret_chunk_kernel,
        grid_spec=pltpu.PrefetchScalarGridSpec(
            num_scalar_prefetch=1,
            grid=(batch, nc),
            in_specs=[pl.BlockSpec((C, RET_QK), blk), pl.BlockSpec((C, RET_QK), blk),
                      pl.BlockSpec((C, RET_VD), blk), pl.BlockSpec((C, RET_VD), blk)],
            out_specs=[pl.BlockSpec((C, RET_VD), blk),
                       pl.BlockSpec((1, RET_HEADS, RET_DK, RET_DV), lambda b, c, lg: (b, 0, 0, 0))],
            scratch_shapes=[pltpu.VMEM((RET_HEADS, C, C), F32)],
        ),
        out_shape=[jax.ShapeDtypeStruct((batch * seq, RET_VD), BF16),
                   jax.ShapeDtypeStruct((batch, RET_HEADS, RET_DK, RET_DV), F32)],
        compiler_params=_cparams(("arbitrary", "arbitrary")),
        name="retention_chunks",
    )(log_gamma, q, k, v, gate)


def _as_column(row):
    return jnp.transpose(jnp.broadcast_to(row, (LANES, LANES)))


def _ret_step_kernel(lg_ref, s_ref, q_ref, k_ref, v_ref, gate_ref, o_ref, so_ref):
    wide = lambda x: jnp.concatenate([x] * (RET_DV // LANES), axis=1)
    for h in range(RET_HEADS):
        gamma = jnp.exp(jnp.full((1, RET_DV), 1.0, F32) * lg_ref[h])
        state = s_ref[0, h]
        column = lambda ref: jnp.concatenate(
            [_as_column(ref[0, :, h * RET_DK + c * LANES:h * RET_DK + (c + 1) * LANES])
             for c in range(RET_DK // LANES)], axis=0)
        qc = column(q_ref)
        kc = column(k_ref)
        vr = v_ref[0, :, h * RET_DV:(h + 1) * RET_DV]
        qk = jnp.sum(qc * kc, axis=0, keepdims=True)
        qs = jnp.sum(wide(qc) * state, axis=0, keepdims=True)
        o = wide(qk) * vr + qs * gamma
        so_ref[0, h] = state * gamma + wide(kc) * vr
        o = o * lax.rsqrt(jnp.mean(o * o, axis=-1, keepdims=True) + NORM_EPS)
        gate = gate_ref[0, :, h * RET_DV:(h + 1) * RET_DV]
        o_ref[0, :, h * RET_DV:(h + 1) * RET_DV] = (o * (gate * _sigmoid(gate))).astype(o_ref.dtype)


def _retention_step(state, q, k, v, gate, log_gamma):
    nb = state.shape[0]
    st = pl.BlockSpec((1, RET_HEADS, RET_DK, RET_DV), lambda b, lg: (b, 0, 0, 0))
    rowk = pl.BlockSpec((1, 1, RET_QK), lambda b, lg: (b, 0, 0))
    rowv = pl.BlockSpec((1, 1, RET_VD), lambda b, lg: (b, 0, 0))
    o, s_new = pl.pallas_call(
        _ret_step_kernel,
        grid_spec=pltpu.PrefetchScalarGridSpec(
            num_scalar_prefetch=1, grid=(nb,),
            in_specs=[st, rowk, rowk, rowv, rowv], out_specs=[rowv, st]),
        out_shape=[jax.ShapeDtypeStruct((nb, 1, RET_VD), BF16), jax.ShapeDtypeStruct(state.shape, F32)],
        compiler_params=_cparams(("arbitrary",)),
        name="retention_step",
    )(log_gamma, state, q.reshape(nb, 1, RET_QK), k.reshape(nb, 1, RET_QK), v.reshape(nb, 1, RET_VD),
      gate.reshape(nb, 1, RET_VD))
    return o.reshape(nb, RET_VD), s_new


def _mix_out_kernel(o_ref, w_ref, x_ref, g_ref, y_ref):
    y_ref[...] = x_ref[...] + _rms(_dot(o_ref[...], w_ref[...]), g_ref[...])


def _mix_out(o_bf, w_bf, x, g, bm):
    rows, kdim = o_bf.shape
    row = lambda i: (i, 0)
    return pl.pallas_call(
        _mix_out_kernel,
        grid=(rows // bm,),
        in_specs=[pl.BlockSpec((bm, kdim), row), _resident(w_bf.shape), pl.BlockSpec((bm, D_MODEL), row),
                  _resident((1, D_MODEL))],
        out_specs=pl.BlockSpec((bm, D_MODEL), row),
        out_shape=jax.ShapeDtypeStruct((rows, D_MODEL), F32),
        compiler_params=_cparams(("arbitrary",)),
        name="mix_out",
    )(o_bf, w_bf, x, g)


def _ffn_kernel(*refs, seq_mode, blocks_per_seq):
    if seq_mode:
        (x_ref, gpre_ref, wup_ref, wdw_ref, bdw_ref, wdown_ref, gpost_ref, p_ref, wple_ref, wgate_ref,
         y_ref, tail_ref, h_sc, carry_sc) = refs
    else:
        (x_ref, gpre_ref, wup_ref, wdw_ref, bdw_ref, wdown_ref, gpost_ref, p_ref, wple_ref, wgate_ref,
         prev2_ref, prev1_ref, y_ref, a_ref, h_sc) = refs
    x = x_ref[...]
    bm = x.shape[0]
    xn = _rms(x, gpre_ref[...]).astype(BF16)
    if seq_mode:
        @pl.when(pl.program_id(0) % blocks_per_seq == 0)
        def _():
            carry_sc[...] = jnp.zeros_like(carry_sc)
        row = lax.broadcasted_iota(jnp.int32, (bm, FFN_CHUNK), 0)
    for c in range(D_FF // FFN_CHUNK):
        sl = slice(c * FFN_CHUNK, (c + 1) * FFN_CHUNK)
        a = _dot(xn, wup_ref[:, sl])
        b = _dot(xn, wup_ref[:, D_FF + c * FFN_CHUNK:D_FF + (c + 1) * FFN_CHUNK])
        if seq_mode:
            c2 = carry_sc[SUBLANES - 2:SUBLANES - 1, sl]
            c1 = carry_sc[SUBLANES - 1:SUBLANES, sl]
            a1 = jnp.where(row == 0, c1, pltpu.roll(a, 1, 0))
            a2 = jnp.where(row == 0, c2, jnp.where(row == 1, c1, pltpu.roll(a, 2, 0)))
            last = a[bm - SUBLANES:bm, :]
            carry_sc[:, sl] = last
            tail_ref[0, :, sl] = last
        else:
            a1 = prev1_ref[:, sl]
            a2 = prev2_ref[:, sl]
            a_ref[:, sl] = a
        conv = bdw_ref[:, sl] + a2 * wdw_ref[0:1, sl]
        conv = conv + a1 * wdw_ref[1:2, sl]
        conv = conv + a * wdw_ref[2:3, sl]
        h_sc[:, sl] = (_gelu_tanh(conv) * b).astype(BF16)
    f = _dot(h_sc[...], wdown_ref[...])
    x2 = x + _rms(f, gpost_ref[...])
    ple = _dot(p_ref[...].astype(BF16), wple_ref[...])
    gate = _dot(x2.astype(BF16), wgate_ref[...])
    y_ref[...] = x2 + ple * _sigmoid(gate)


def _ffn(x, gpre, wup_bf, wdw, bdw, wdown_bf, gpost, p, wple_bf, wgate_bf, bm, seq_len=None, prev=None):
    rows = x.shape[0]
    row = lambda i: (i, 0)
    seq_mode = prev is None
    in_specs = [pl.BlockSpec((bm, D_MODEL), row), _resident((1, D_MODEL)), _resident(wup_bf.shape),
                _resident(wdw.shape), _resident((1, D_FF)), _resident(wdown_bf.shape), _resident((1, D_MODEL)),
                pl.BlockSpec((bm, PLE_DIM), row), _resident(wple_bf.shape), _resident(wgate_bf.shape)]
    args = [x, gpre, wup_bf, wdw, bdw, wdown_bf, gpost, p, wple_bf, wgate_bf]
    scratch = [pltpu.VMEM((bm, D_FF), BF16)]
    if seq_mode:
        bps = seq_len // bm
        out_specs = [pl.BlockSpec((bm, D_MODEL), row), pl.BlockSpec((1, SUBLANES, D_FF), lambda i: (i // bps, 0, 0))]
        out_shape = [jax.ShapeDtypeStruct((rows, D_MODEL), F32),
                     jax.ShapeDtypeStruct((rows // seq_len, SUBLANES, D_FF), F32)]
        scratch.append(pltpu.VMEM((SUBLANES, D_FF), F32))
    else:
        bps = 1
        in_specs += [pl.BlockSpec((bm, D_FF), row), pl.BlockSpec((bm, D_FF), row)]
        args += list(prev)
        out_specs = [pl.BlockSpec((bm, D_MODEL), row), pl.BlockSpec((bm, D_FF), row)]
        out_shape = [jax.ShapeDtypeStruct((rows, D_MODEL), F32), jax.ShapeDtypeStruct((rows, D_FF), F32)]
    return pl.pallas_call(
        functools.partial(_ffn_kernel, seq_mode=seq_mode, blocks_per_seq=bps),
        grid=(rows // bm,),
        in_specs=in_specs, out_specs=out_specs, out_shape=out_shape, scratch_shapes=scratch,
        compiler_params=_cparams(("arbitrary",)),
        name="conv_ffn",
    )(*args)


def _kvq_kernel(x_ref, gkv_ref, wkv_ref, wf_ref, bf_ref, gq_ref, wq_ref,
                k_ref, v_ref, lf_ref, kb_ref, vb_ref, qb_ref, kn_ref, qn_ref):
    x = x_ref[...]
    xn = _rms(x, gkv_ref[...]).astype(BF16)
    k = _dot(xn, wkv_ref[:, :FOX_HD])
    v = _dot(xn, wkv_ref[:, FOX_HD:])
    k_ref[...] = k
    v_ref[...] = v
    kb = k.astype(BF16)
    kb_ref[...] = kb
    vb_ref[...] = v.astype(BF16)
    f = _dot(xn, wf_ref[...]) + bf_ref[...]
    lf_ref[...] = _log_sigmoid(f)[:, :FOX_HEADS]
    xq = _rms(x, gq_ref[...]).astype(BF16)
    qb = (_dot(xq, wq_ref[...]) * FOX_DH ** -0.5).astype(BF16)
    qb_ref[...] = qb
    head_sum = (lax.broadcasted_iota(jnp.int32, (FOX_HD, LANES), 0) // FOX_DH ==
                lax.broadcasted_iota(jnp.int32, (FOX_HD, LANES), 1))
    head_sum = jnp.where(head_sum, 1.0, 0.0).astype(BF16)
    for src, dst in ((kb, kn_ref), (qb, qn_ref)):
        sf = src.astype(F32)
        n2 = jnp.max(_dot((sf * sf).astype(BF16), head_sum), axis=0, keepdims=True)
        dst[0] = jnp.broadcast_to(n2, (SUBLANES, LANES))


def _kvq(x, gkv, wkv_bf, wf_bf, bf_pad, gq, wq_bf, bm):
    rows = x.shape[0]
    row = lambda i: (i, 0)
    wide = pl.BlockSpec((bm, FOX_HD), row)
    norm = pl.BlockSpec((1, SUBLANES, LANES), lambda i: (i, 0, 0))
    norm_shape = jax.ShapeDtypeStruct((rows // bm, SUBLANES, LANES), F32)
    return pl.pallas_call(
        _kvq_kernel,
        grid=(rows // bm,),
        in_specs=[pl.BlockSpec((bm, D_MODEL), row), _resident((1, D_MODEL)), _resident(wkv_bf.shape),
                  _resident(wf_bf.shape), _resident((1, LANES)), _resident((1, D_MODEL)), _resident(wq_bf.shape)],
        out_specs=[wide, wide, pl.BlockSpec((bm, FOX_HEADS), row), wide, wide, wide, norm, norm],
        out_shape=[jax.ShapeDtypeStruct((rows, FOX_HD), F32), jax.ShapeDtypeStruct((rows, FOX_HD), F32),
                   jax.ShapeDtypeStruct((rows, FOX_HEADS), F32), jax.ShapeDtypeStruct((rows, FOX_HD), BF16),
                   jax.ShapeDtypeStruct((rows, FOX_HD), BF16), jax.ShapeDtypeStruct((rows, FOX_HD), BF16),
                   norm_shape, norm_shape],
        compiler_params=_cparams(("arbitrary",)),
        name="kv_q_proj",
    )(x, gkv, wkv_bf, wf_bf, bf_pad, gq, wq_bf)


def _cumsum_kernel(lf_ref, cq_ref, ck_ref, first_ref, last_ref, carry_sc):
    bl = lf_ref.shape[0]

    @pl.when(pl.program_id(1) == 0)
    def _():
        carry_sc[...] = jnp.zeros_like(carry_sc)

    lf = lf_ref[...]
    tri = (lax.broadcasted_iota(jnp.int32, (bl, bl), 1) <= lax.broadcasted_iota(jnp.int32, (bl, bl), 0))
    tri = jnp.where(tri, 1.0, 0.0).astype(BF16)
    hi, mid, lo = _split3(lf)
    c = (_dot(tri, lo) + _dot(tri, mid)) + _dot(tri, hi) + carry_sc[...]
    carry_sc[...] = c[bl - 1:bl, :]
    first_ref[0] = c[0:1, :]
    last_ref[0] = c[bl - 1:bl, :]

    lane_of_head = lax.broadcasted_iota(jnp.int32, (FOX_HEADS, LANES), 1) - 3 * lax.broadcasted_iota(
        jnp.int32, (FOX_HEADS, LANES), 0)
    lane = lax.broadcasted_iota(jnp.int32, (1, LANES), 1)
    cq = jnp.where(lane < BIAS_ONES, 1.0, 0.0)
    ck = jnp.where(jnp.logical_and(lane >= BIAS_ONES, lane < 2 * BIAS_ONES), 1.0, 0.0)
    for piece, part in enumerate(_split3(c)):
        cq = cq + _dot(part, jnp.where(lane_of_head == BIAS_ONES + piece, 1.0, 0.0).astype(BF16))
        ck = ck + _dot(part, jnp.where(lane_of_head == piece, -1.0, 0.0).astype(BF16))
    cq_ref[...] = cq.astype(BF16)
    ck_ref[...] = ck.astype(BF16)


def _cumsum_time(lf, batch, seq):
    bl = min(ATT_BLOCK, seq)
    nb = seq // bl
    blk = lambda b, i: (b * nb + i, 0)
    wide = pl.BlockSpec((bl, LANES), blk)
    edge = pl.BlockSpec((1, 1, FOX_HEADS), lambda b, i: (b * nb + i, 0, 0))
    return pl.pallas_call(
        _cumsum_kernel,
        grid=(batch, nb),
        in_specs=[pl.BlockSpec((bl, FOX_HEADS), blk)], out_specs=[wide, wide, edge, edge],
        out_shape=[jax.ShapeDtypeStruct((lf.shape[0], LANES), BF16)] * 2
        + [jax.ShapeDtypeStruct((batch * nb, 1, FOX_HEADS), F32)] * 2,
        scratch_shapes=[pltpu.VMEM((1, FOX_HEADS), F32)],
        compiler_params=_cparams(("arbitrary", "arbitrary")),
        name="logf_cumsum",
    )(lf)


def _fox_prompt_kernel(cfirst_ref, clast_ref, qkb_ref, q_ref, k_ref, v_ref, cq_ref, ck_ref, o_ref, m_sc, acc_sc):
    b = pl.program_id(0)
    hp = pl.program_id(1)
    i = pl.program_id(2)

    def dead(j):
        worst = None
        for hl in range(2):
            h = 2 * hp + hl
            gap = cfirst_ref[b, i, h] - clast_ref[b, j, h] + 2.0 * qkb_ref[b, h]
            worst = gap if worst is None else jnp.maximum(worst, gap)
        return worst < -EXP_UNDERFLOW

    j_lo = lax.while_loop(lambda j: jnp.logical_and(j < i, dead(j)), lambda j: j + 1, jnp.int32(0))
    bq = q_ref.shape[0]
    pair = 2 * FOX_DH
    q = q_ref[...]
    low = lax.broadcasted_iota(jnp.int32, (1, pair), 1) < FOX_DH
    causal = (lax.broadcasted_iota(jnp.int32, (bq, bq), 1) <= lax.broadcasted_iota(jnp.int32, (bq, bq), 0))
    ones = jnp.ones((bq, pair), BF16)
    lanes = lambda x, n: jnp.concatenate([x] * n, axis=1)
    bias_lane = lax.broadcasted_iota(jnp.int32, (1, LANES), 1)
    cq = cq_ref[...]
    q_aug = []
    for hl in range(2):
        qm = jnp.where(low if hl == 0 else jnp.logical_not(low), q, jnp.zeros_like(q))
        first = 3 * (2 * hp + hl)
        own = jnp.logical_or(jnp.logical_and(bias_lane >= first, bias_lane < first + 3),
                             jnp.logical_and(bias_lane >= first + BIAS_ONES, bias_lane < first + BIAS_ONES + 3))
        q_aug.append(jnp.concatenate([qm, jnp.where(own, cq, jnp.zeros_like(cq))], axis=1))
        m_sc[hl] = jnp.full(m_sc.shape[1:], -jnp.inf, F32)
        acc_sc[hl] = jnp.zeros(acc_sc.shape[1:], F32)

    def block(j, masked):
        start = pl.multiple_of(j * bq, bq)
        k_aug = jnp.concatenate([k_ref[pl.ds(start, bq), :], ck_ref[pl.ds(start, bq), :]], axis=1)
        v_aug = jnp.concatenate([v_ref[pl.ds(start, bq), :], ones], axis=1)
        for hl in range(2):
            s = _dot_nt(q_aug[hl], k_aug)
            if masked:
                s = jnp.where(causal, s, -jnp.inf)
            m_prev = m_sc[hl]
            m_next = jnp.maximum(m_prev, jnp.max(s, axis=1, keepdims=True))
            alpha = jnp.exp(m_prev - m_next)
            p = jnp.exp(s - lanes(m_next, bq // LANES))
            acc_sc[hl] = lanes(alpha, 2) * acc_sc[hl] + _dot(p.astype(BF16), v_aug)
            m_sc[hl] = m_next

    n_live = i - j_lo + 1

    @pl.when(n_live == 1)
    def _():
        block(i, True)

    @pl.when(n_live == 2)
    def _():
        block(i, True)
        block(i - 1, False)

    @pl.when(n_live >= 3)
    def _():
        block(i, True)
        block(i - 1, False)
        block(i - 2, False)

    def body(t, carry):
        block(i - 3 - 2 * t, False)
        block(i - 4 - 2 * t, False)
        return carry

    rest = jnp.maximum(n_live - 3, 0)
    lax.fori_loop(0, rest // 2, body, 0)

    @pl.when(rest % 2 == 1)
    def _():
        block(j_lo, False)

    o0 = acc_sc[0, :, :pair] / acc_sc[0, :, pair:]
    o1 = acc_sc[1, :, :pair] / acc_sc[1, :, pair:]
    o_ref[...] = jnp.where(low, o0, o1).astype(o_ref.dtype)


def _fox_prompt(q_bf, k_bf, v_bf, cq, ck, c_first, c_last, qk_bound, batch, seq):
    bq = min(ATT_BLOCK, seq)
    nq = seq // bq
    pair = 2 * FOX_DH
    qblk = pl.BlockSpec((bq, pair), lambda b, hp, i, *_: (b * nq + i, hp))
    kvblk = pl.BlockSpec((seq, pair), lambda b, hp, i, *_: (b, hp))
    return pl.pallas_call(
        _fox_prompt_kernel,
        grid_spec=pltpu.PrefetchScalarGridSpec(
            num_scalar_prefetch=3,
            grid=(batch, FOX_HEADS // 2, nq),
            in_specs=[qblk, kvblk, kvblk,
                      pl.BlockSpec((bq, LANES), lambda b, hp, i, *_: (b * nq + i, 0)),
                      pl.BlockSpec((seq, LANES), lambda b, hp, i, *_: (b, 0))],
            out_specs=qblk,
            scratch_shapes=[pltpu.VMEM((2, bq, LANES), F32), pltpu.VMEM((2, bq, 2 * pair), F32)]),
        out_shape=jax.ShapeDtypeStruct((batch * seq, FOX_HD), BF16),
        compiler_params=_cparams(("arbitrary", "arbitrary", "arbitrary")),
        name="fox_attention_prompt",
    )(c_first, c_last, qk_bound, q_bf, k_bf, v_bf, cq, ck)


def _fox_scores_kernel(pt_ref, q_ref, knew_ref, lfnew_ref, *refs, pp):
    k_refs, lf_refs = refs[:pp], refs[pp:2 * pp]
    s_ref, pmax_ref, m_ref, sself_ref, q_sc, carry_sc = refs[2 * pp:]
    step = pl.program_id(1)
    page = k_refs[0].shape[3]

    @pl.when(step == 0)
    def _():
        for hp in range(FOX_HEADS // 2):
            qc, kc = [_as_column(r[0, :, hp * LANES:(hp + 1) * LANES]) for r in (q_ref, knew_ref)]
            for hl in range(2):
                h = 2 * hp + hl
                qh = qc[hl * FOX_DH:(hl + 1) * FOX_DH]
                q_sc[h] = qh
                sself_ref[0, h:h + 1, :] = jnp.sum(qh * kc[hl * FOX_DH:(hl + 1) * FOX_DH], axis=0, keepdims=True)
        m_ref[0] = sself_ref[0]
        carry_sc[...] = _as_column(lfnew_ref[0])[:FOX_HEADS]

    q = q_sc[...]

    later = (lax.broadcasted_iota(jnp.int32, (page, page), 0) > lax.broadcasted_iota(jnp.int32, (page, page), 1))
    later = jnp.where(later, 1.0, 0.0).astype(BF16)
    carry = carry_sc[...]
    m = m_ref[0]
    for i in range(pp):
        lf = lf_refs[i][0]
        hi, mid, lo = _split3(lf)
        bias = (_dot(lo, later) + _dot(mid, later)) + _dot(hi, later) + carry
        carry = carry + jnp.sum(lf, axis=1, keepdims=True)
        s = jnp.sum(k_refs[i][0] * q, axis=1) + bias
        s_ref[0, :, i * page:(i + 1) * page] = s
        page_max = jnp.max(s, axis=1, keepdims=True)
        pmax_ref[0, i] = jnp.broadcast_to(page_max, (FOX_HEADS, LANES))
        m = jnp.maximum(m, page_max)
    carry_sc[...] = carry
    m_ref[0] = m


def _fox_values_kernel(page_ref, walk_ref, count_ref, s_ref, m_ref, sself_ref, vnew_ref, *refs, pp):
    v_refs = refs[:pp]
    o_ref, l_sc, acc_sc = refs[pp:]
    b = pl.program_id(0)
    step = pl.program_id(1)
    page = v_refs[0].shape[3]
    m = m_ref[0]

    @pl.when(step == 0)
    def _():
        p_self = jnp.exp(sself_ref[0] - m)
        l_sc[...] = p_self
        lane = lax.broadcasted_iota(jnp.int32, (FOX_DH, page), 1)
        for hp in range(FOX_HEADS // 2):
            vc = _as_column(vnew_ref[0, :, hp * LANES:(hp + 1) * LANES])
            for hl in range(2):
                h = 2 * hp + hl
                acc_sc[h] = jnp.where(lane == 0, vc[hl * FOX_DH:(hl + 1) * FOX_DH] * p_self[h:h + 1, :], 0.0)

    n_live = count_ref[b]
    for i in range(pp):
        slot = step * pp + i

        @pl.when(slot < n_live)
        def _():
            start = pl.multiple_of(walk_ref[b, slot] * page, page)
            p = jnp.exp(s_ref[0, :, pl.ds(start, page)] - m)
            l_sc[...] = l_sc[...] + jnp.sum(p, axis=1, keepdims=True)
            for h in range(FOX_HEADS):
                acc_sc[h] = acc_sc[h] + v_refs[i][0, h] * p[h:h + 1, :]

    @pl.when(step == pl.num_programs(1) - 1)
    def _():
        o_ref[0] = (jnp.sum(acc_sc[...], axis=2) / l_sc[:, :1]).astype(o_ref.dtype)


def _fox_decode(q, k_new, v_new, lf_new, cache_k, cache_v, cache_logf, page_table):
    nb, n_pages = page_table.shape
    page = cache_k.shape[1]
    assert page == LANES, "one cache page fills the lane axis"
    pp = min(PAGES_PER_STEP, n_pages)
    n_steps = n_pages // pp
    ck = jnp.transpose(cache_k, (0, 2, 3, 1))
    cv = jnp.transpose(cache_v, (0, 2, 3, 1))
    clf = jnp.transpose(cache_logf, (0, 2, 1))
    as_row = lambda a: a.reshape(nb, 1, FOX_HD)
    lf_row = jnp.pad(lf_new, ((0, 0), (0, LANES - FOX_HEADS))).reshape(nb, 1, LANES)
    heads_lanes = jax.ShapeDtypeStruct((nb, FOX_HEADS, LANES), F32)

    def k_page(i):
        return pl.BlockSpec((1, FOX_HEADS, FOX_DH, page),
                            lambda b, s, pt: (pt[b, n_pages - 1 - (s * pp + i)], 0, 0, 0))

    def lf_page(i):
        return pl.BlockSpec((1, FOX_HEADS, page), lambda b, s, pt: (pt[b, n_pages - 1 - (s * pp + i)], 0, 0))

    row = pl.BlockSpec((1, 1, FOX_HD), lambda b, s, *_: (b, 0, 0))
    per_seq = pl.BlockSpec((1, FOX_HEADS, LANES), lambda b, s, *_: (b, 0, 0))
    scores, page_max, m, s_self = pl.pallas_call(
        functools.partial(_fox_scores_kernel, pp=pp),
        grid_spec=pltpu.PrefetchScalarGridSpec(
            num_scalar_prefetch=1, grid=(nb, n_steps),
            in_specs=[row, row, pl.BlockSpec((1, 1, LANES), lambda b, s, pt: (b, 0, 0))]
            + [k_page(i) for i in range(pp)] + [lf_page(i) for i in range(pp)],
            out_specs=[pl.BlockSpec((1, FOX_HEADS, pp * page), lambda b, s, pt: (b, 0, s)),
                       pl.BlockSpec((1, pp, FOX_HEADS, LANES), lambda b, s, pt: (b, s, 0, 0)), per_seq, per_seq],
            scratch_shapes=[pltpu.VMEM((FOX_HEADS, FOX_DH, page), F32), pltpu.VMEM((FOX_HEADS, page), F32)]),
        out_shape=[jax.ShapeDtypeStruct((nb, FOX_HEADS, n_pages * page), F32),
                   jax.ShapeDtypeStruct((nb, n_pages, FOX_HEADS, LANES), F32), heads_lanes, heads_lanes],
        compiler_params=_cparams(("arbitrary", "arbitrary")),
        name="fox_decode_scores",
    )(page_table, as_row(q), as_row(k_new), lf_row, *([ck] * pp), *([clf] * pp))

    live = jnp.any(page_max[:, :, :, 0] - m[:, None, :, 0] >= -EXP_UNDERFLOW, axis=2)
    count = jnp.sum(live, axis=1).astype(jnp.int32)
    walk = jnp.argsort(jnp.logical_not(live), axis=1, stable=True).astype(jnp.int32)
    pages = jnp.take_along_axis(page_table[:, ::-1], walk, axis=1)
    is_live = (jnp.arange(n_pages, dtype=jnp.int32)[None, :] < count[:, None]).reshape(nb * n_steps, pp)
    flat = pages.reshape(nb * n_steps, pp)
    last_live = lax.cummax(jnp.where(is_live, jnp.arange(nb * n_steps, dtype=jnp.int32)[:, None], 0), axis=0)
    pages = jnp.take_along_axis(flat, last_live, axis=0).reshape(nb, n_pages)

    def v_page(i):
        return pl.BlockSpec((1, FOX_HEADS, FOX_DH, page), lambda b, s, pg, wk, ct: (pg[b, s * pp + i], 0, 0, 0))

    out = pl.pallas_call(
        functools.partial(_fox_values_kernel, pp=pp),
        grid_spec=pltpu.PrefetchScalarGridSpec(
            num_scalar_prefetch=3, grid=(nb, n_steps),
            in_specs=[pl.BlockSpec((1, FOX_HEADS, n_pages * page), lambda b, s, *_: (b, 0, 0)), per_seq, per_seq, row]
            + [v_page(i) for i in range(pp)],
            out_specs=pl.BlockSpec((1, FOX_HEADS, FOX_DH), lambda b, s, *_: (b, 0, 0)),
            scratch_shapes=[pltpu.VMEM((FOX_HEADS, LANES), F32), pltpu.VMEM((FOX_HEADS, FOX_DH, page), F32)]),
        out_shape=jax.ShapeDtypeStruct((nb, FOX_HEADS, FOX_DH), BF16),
        compiler_params=_cparams(("arbitrary", "arbitrary")),
        name="fox_decode_values",
    )(pages, walk, count, scores, m, s_self, as_row(v_new), *([cv] * pp))
    return out.reshape(nb, FOX_HD)


def _run_group(x, p, cos, sin, w, bm, seq_len, sample):
    rows = x.shape[0]
    qkv_dtype = BF16 if sample is None else F32
    q, k, v, gate = _ret_in(x, w["g_pre_mix"][0], w["w_in"], cos, sin, bm, qkv_dtype)
    if sample is None:
        batch = rows // seq_len
        o, ret_state = _retention_prompt(q, k, v, gate, w["log_gamma"], batch, seq_len)
    else:
        o, ret_state = _retention_step(sample["state_ret"], q, k, v, gate, w["log_gamma"])
    x = _mix_out(o, w["w_out_ret"], x, w["g_post_mix"][0], bm)

    conv_out = []
    ffn_args = lambda i: (w["g_pre_ffn"][i], w["w_up"][i], w["w_dw"][i], w["b_dw"][i], w["w_down"][i],
                          w["g_post_ffn"][i], p[i], w["w_ple"][i], w["w_gate"][i], bm)
    if sample is None:
        x, tail = _ffn(x, *ffn_args(0), seq_len=seq_len)
        conv_out.append(tail[:, SUBLANES - 2:, :])
    else:
        buf = sample["state_conv"][0]
        x, a = _ffn(x, *ffn_args(0), prev=(buf[:, 0], buf[:, 1]))
        conv_out.append(jnp.stack([buf[:, 1], a], axis=1))

    k_new, v_new, lf_new, k_bf, v_bf, q_bf, k_n2, q_n2 = _kvq(
        x, w["g_kv"], w["w_kv"], w["w_f"], w["b_f"], w["g_pre_mix"][1], w["w_q"], bm)
    if sample is None:
        cq, ck, c_first, c_last = _cumsum_time(lf_new, batch, seq_len)
        per_seq_max = lambda n2: jnp.max(n2.reshape(batch, -1, SUBLANES, LANES)[:, :, 0, :FOX_HEADS], axis=1)
        qk_bound = jnp.sqrt(per_seq_max(k_n2) * per_seq_max(q_n2)) * NORM_PAD
        edges = lambda e: e.reshape(batch, -1, FOX_HEADS)
        att = _fox_prompt(q_bf, k_bf, v_bf, cq, ck, edges(c_first), edges(c_last), qk_bound, batch, seq_len)
    else:
        att = _fox_decode(q_bf.astype(F32), k_new, v_new, lf_new, sample["cache_k"], sample["cache_v"],
                          sample["cache_logf"], sample["page_table"])
    x = _mix_out(att, w["w_out_fox"], x, w["g_post_mix"][1], bm)

    if sample is None:
        x, tail = _ffn(x, *ffn_args(1), seq_len=seq_len)
        conv_out.append(tail[:, SUBLANES - 2:, :])
    else:
        buf = sample["state_conv"][1]
        x, a = _ffn(x, *ffn_args(1), prev=(buf[:, 0], buf[:, 1]))
        conv_out.append(jnp.stack([buf[:, 1], a], axis=1))
    return x, ret_state, jnp.stack(conv_out), k_new, v_new, lf_new


def kernel(x_prompt, x_sample, state_ret, state_conv, cache_k, cache_v, cache_logf, page_table, p_prompt, p_sample,
           norm_pre_mix, norm_post_mix, norm_pre_ffn, norm_post_ffn, w_in_ret, w_out_ret, norm_kv, w_kvf, b_f,
           w_q_fox, w_out_fox, w_up, w_dw, b_dw, w_down, w_ple, w_ple_gate):
    bp, tp, _ = x_prompt.shape
    nb, ts, _ = x_sample.shape
    assert ts == 1, "the sample group is one new token per sequence"
    past_len = page_table.shape[1] * cache_k.shape[1]
    depth = w_up.shape[0]
    row_vec = lambda a: a.reshape(a.shape[0], 1, a.shape[1])
    w = {
        "g_pre_mix": row_vec(norm_pre_mix), "g_post_mix": row_vec(norm_post_mix),
        "g_pre_ffn": row_vec(norm_pre_ffn), "g_post_ffn": row_vec(norm_post_ffn),
        "g_kv": norm_kv.reshape(1, D_MODEL),
        "w_in": w_in_ret[0].astype(BF16), "w_out_ret": w_out_ret[0].astype(BF16),
        "w_kv": w_kvf[:, :2 * FOX_HD].astype(BF16),
        "w_f": jnp.pad(w_kvf[:, 2 * FOX_HD:], ((0, 0), (0, LANES - FOX_HEADS))).astype(BF16),
        "b_f": jnp.pad(b_f, (0, LANES - FOX_HEADS)).reshape(1, LANES),
        "w_q": w_q_fox[0].astype(BF16), "w_out_fox": w_out_fox[0].astype(BF16),
        "w_up": w_up.astype(BF16), "w_dw": w_dw, "b_dw": row_vec(b_dw), "w_down": w_down.astype(BF16),
        "w_ple": w_ple.astype(BF16), "w_gate": w_ple_gate.astype(BF16),
        "log_gamma": jnp.log1p(-jnp.exp2(-5.0 - jnp.arange(RET_HEADS, dtype=F32))),
    }
    bm = min(ROW_BLOCK, tp)
    cos_p, sin_p = _rope_table(tp, 0)
    y_p, ret_p, conv_p, k_p, v_p, lf_p = _run_group(
        x_prompt.reshape(bp * tp, D_MODEL), p_prompt.reshape(depth, bp * tp, PLE_DIM), cos_p, sin_p, w, bm, tp, None)
    cos_s, sin_s = _rope_table(SUBLANES, past_len)
    cos_s = jnp.broadcast_to(cos_s[:1], (nb, ROPE_HALF))
    sin_s = jnp.broadcast_to(sin_s[:1], (nb, ROPE_HALF))
    sample = {"state_ret": state_ret[0], "state_conv": state_conv, "cache_k": cache_k, "cache_v": cache_v,
              "cache_logf": cache_logf, "page_table": page_table}
    y_s, ret_s, conv_s, k_s, v_s, lf_s = _run_group(
        x_sample.reshape(nb, D_MODEL), p_sample.reshape(depth, nb, PLE_DIM), cos_s, sin_s, w, nb, 1, sample)
    return (y_p.reshape(bp, tp, D_MODEL), y_s.reshape(nb, 1, D_MODEL),
            ret_p[None], ret_s[None], conv_p, conv_s,
            k_p.reshape(bp, tp, FOX_HEADS, FOX_DH), k_s.reshape(nb, 1, FOX_HEADS, FOX_DH),
            v_p.reshape(bp, tp, FOX_HEADS, FOX_DH), v_s.reshape(nb, 1, FOX_HEADS, FOX_DH),
            lf_p.reshape(bp, tp, FOX_HEADS), lf_s.reshape(nb, 1, FOX_HEADS))
```

```python
import functools
import math

import jax
import jax.numpy as jnp
from jax import lax
from jax.experimental import pallas as pl
from jax.experimental.pallas import tpu as pltpu

F32 = jnp.float32
BF16 = jnp.bfloat16

D_MODEL = 1024
RET_HEADS = 4
RET_DK = 256
RET_DV = 512
RET_QK = RET_HEADS * RET_DK
RET_VD = RET_HEADS * RET_DV
ROPE_BASE = 10000.0
ROPE_HALF = RET_DK // 2
FOX_HEADS = 16
FOX_DH = 64
FOX_HD = FOX_HEADS * FOX_DH
D_FF = 2816
PLE_DIM = 256
NORM_EPS = 1e-6

LANES = 128
SUBLANES = 8
VMEM_LIMIT = 56 * 1024 * 1024

ROW_BLOCK = 512
RET_CHUNK = 256
FFN_CHUNK = 256
ATT_BLOCK = 512
ATT_HEADS = 4
PAGES_PER_STEP = 16
VALUE_PAGES_PER_STEP = 32
BIAS_ONES = 48
EXP_UNDERFLOW = 110.0
NORM_PAD = 1.05


def _cparams(sem):
    return pltpu.CompilerParams(dimension_semantics=sem, vmem_limit_bytes=VMEM_LIMIT)


def _resident(shape):
    return pl.BlockSpec(shape, lambda *_: (0,) * len(shape), pipeline_mode=pl.Buffered(1))


def _dot(a, b):
    return jnp.dot(a, b, preferred_element_type=F32)


def _dot_nt(a, b):
    return lax.dot_general(a, b, (((1,), (1,)), ((), ())), preferred_element_type=F32)


def _dot_tn(a, b):
    return lax.dot_general(a, b, (((0,), (0,)), ((), ())), preferred_element_type=F32)


def _rms(x, g):
    return x * lax.rsqrt(jnp.mean(x * x, axis=-1, keepdims=True) + NORM_EPS) * g


def _sigmoid(x):
    return 1.0 / (1.0 + jnp.exp(-x))


def _gelu_tanh(x):
    return x * (0.5 * (1.0 + jnp.tanh(math.sqrt(2.0 / math.pi) * (x + 0.044715 * (x * x * x)))))


def _log_sigmoid(x):
    z = -x
    return -(jnp.maximum(z, 0.0) + jnp.log1p(jnp.exp(-jnp.abs(z))))


def _split3(x):
    hi = x.astype(BF16)
    r1 = x - hi.astype(F32)
    mid = r1.astype(BF16)
    lo = (r1 - mid.astype(F32)).astype(BF16)
    return hi, mid, lo


def _rope_kernel(cos_ref, sin_ref, *, base_pos):
    n = cos_ref.shape[0]
    i = pl.program_id(0)
    lane = lax.broadcasted_iota(jnp.int32, (1, ROPE_HALF), 1).astype(F32)
    inv = jnp.power(jnp.full((1, ROPE_HALF), ROPE_BASE, F32), -(lane / ROPE_HALF))
    pos = (lax.broadcasted_iota(jnp.int32, (n, ROPE_HALF), 0) + (i * n + base_pos)).astype(F32)
    ang = pos * inv
    cos_ref[...] = jnp.cos(ang)
    sin_ref[...] = jnp.sin(ang)


def _rope_table(n_rows, base_pos):
    bn = min(n_rows, 1024)
    spec = pl.BlockSpec((bn, ROPE_HALF), lambda i: (i, 0))
    return pl.pallas_call(
        functools.partial(_rope_kernel, base_pos=base_pos),
        grid=(n_rows // bn,),
        out_specs=[spec, spec],
        out_shape=[jax.ShapeDtypeStruct((n_rows, ROPE_HALF), F32)] * 2,
        compiler_params=_cparams(("arbitrary",)),
        name="rope_table",
    )()


def _ret_in_kernel(x_ref, g_ref, w_ref, cos_ref, sin_ref, q_ref, k_ref, v_ref, gate_ref):
    xn = _rms(x_ref[...], g_ref[...]).astype(BF16)
    cos = cos_ref[...]
    sin = sin_ref[...]
    for out_ref, base, scale in ((q_ref, 0, 1.0), (k_ref, RET_QK, RET_DK ** -0.5)):
        for h in range(RET_HEADS):
            lo = h * RET_DK
            u = _dot(xn, w_ref[:, base + lo:base + lo + RET_DK])
            x1, x2 = u[:, :ROPE_HALF], u[:, ROPE_HALF:]
            out_ref[:, lo:lo + ROPE_HALF] = ((x1 * cos - x2 * sin) * scale).astype(out_ref.dtype)
            out_ref[:, lo + ROPE_HALF:lo + RET_DK] = ((x1 * sin + x2 * cos) * scale).astype(out_ref.dtype)
    for h in range(RET_HEADS):
        lo = h * RET_DV
        v_ref[:, lo:lo + RET_DV] = _dot(xn, w_ref[:, 2 * RET_QK + lo:2 * RET_QK + lo + RET_DV]).astype(v_ref.dtype)
        gate_ref[:, lo:lo + RET_DV] = _dot(xn, w_ref[:, 2 * RET_QK + RET_VD + lo:2 * RET_QK + RET_VD + lo + RET_DV])


def _ret_in(x, g, w_bf, cos, sin, bm, qkv_dtype):
    rows = x.shape[0]
    n_tab = cos.shape[0] // bm
    row = lambda i: (i, 0)
    tab = pl.BlockSpec((bm, ROPE_HALF), lambda i: (i % n_tab, 0))
    return pl.pallas_call(
        _ret_in_kernel,
        grid=(rows // bm,),
        in_specs=[pl.BlockSpec((bm, D_MODEL), row), _resident((1, D_MODEL)), _resident(w_bf.shape), tab, tab],
        out_specs=[pl.BlockSpec((bm, RET_QK), row), pl.BlockSpec((bm, RET_QK), row),
                   pl.BlockSpec((bm, RET_VD), row), pl.BlockSpec((bm, RET_VD), row)],
        out_shape=[jax.ShapeDtypeStruct((rows, RET_QK), qkv_dtype), jax.ShapeDtypeStruct((rows, RET_QK), qkv_dtype),
                   jax.ShapeDtypeStruct((rows, RET_VD), qkv_dtype), jax.ShapeDtypeStruct((rows, RET_VD), F32)],
        compiler_params=_cparams(("arbitrary",)),
        name="ret_in",
    )(x, g, w_bf, cos, sin)


def _ret_chunk_kernel(lg_ref, q_ref, k_ref, v_ref, gate_ref, o_ref, s_ref, decay_sc):
    c = pl.program_id(1)
    C = q_ref.shape[0]

    @pl.when(jnp.logical_and(pl.program_id(0) == 0, c == 0))
    def _():
        rel = lax.broadcasted_iota(jnp.int32, (C, C), 0) - lax.broadcasted_iota(jnp.int32, (C, C), 1)
        dist = jnp.maximum(rel, 0).astype(F32)
        for h in range(RET_HEADS):
            decay_sc[h] = jnp.where(rel >= 0, jnp.exp(dist * lg_ref[h]), 0.0)

    @pl.when(c == 0)
    def _():
        s_ref[...] = jnp.zeros_like(s_ref)

    n = lax.broadcasted_iota(jnp.int32, (C, 1), 0).astype(F32)
    for h in range(RET_HEADS):
        lg = lg_ref[h]
        q = q_ref[:, h * RET_DK:(h + 1) * RET_DK]
        k = k_ref[:, h * RET_DK:(h + 1) * RET_DK]
        v = v_ref[:, h * RET_DV:(h + 1) * RET_DV]
        scores = _dot_nt(q, k) * decay_sc[h]
        inner = _dot(scores.astype(BF16), v)
        state = s_ref[0, h]
        cross = _dot(q, state.astype(BF16)) * jnp.exp((n + 1.0) * lg)
        kd = (k.astype(F32) * jnp.exp((C - 1.0 - n) * lg)).astype(BF16)
        chunk_decay = jnp.exp(jnp.full((1, RET_DV), float(C), F32) * lg)
        s_ref[0, h] = state * chunk_decay + _dot_tn(kd, v)
        o = inner + cross
        o = o * lax.rsqrt(jnp.mean(o * o, axis=-1, keepdims=True) + NORM_EPS)
        gate = gate_ref[:, h * RET_DV:(h + 1) * RET_DV]
        o_ref[:, h * RET_DV:(h + 1) * RET_DV] = (o * (gate * _sigmoid(gate))).astype(o_ref.dtype)


def _retention_prompt(q, k, v, gate, log_gamma, batch, seq):
    C = min(RET_CHUNK, seq)
    nc = seq // C
    blk = lambda b, c, lg: (b * nc + c, 0)
    return pl.pallas_call(
        _ret_chunk_kernel,
        grid_spec=pltpu.PrefetchScalarGridSpec(
            num_scalar_prefetch=1,
            grid=(batch, nc),
            in_specs=[pl.BlockSpec((C, RET_QK), blk), pl.BlockSpec((C, RET_QK), blk),
                      pl.BlockSpec((C, RET_VD), blk), pl.BlockSpec((C, RET_VD), blk)],
            out_specs=[pl.BlockSpec((C, RET_VD), blk),
                       pl.BlockSpec((1, RET_HEADS, RET_DK, RET_DV), lambda b, c, lg: (b, 0, 0, 0))],
            scratch_shapes=[pltpu.VMEM((RET_HEADS, C, C), F32)],
        ),
        out_shape=[jax.ShapeDtypeStruct((batch * seq, RET_VD), BF16),
                   jax.ShapeDtypeStruct((batch, RET_HEADS, RET_DK, RET_DV), F32)],
        compiler_params=_cparams(("arbitrary", "arbitrary")),
        name="retention_chunks",
    )(log_gamma, q, k, v, gate)


def _as_column(row):
    return jnp.transpose(jnp.broadcast_to(row, (LANES, LANES)))


def _ret_step_kernel(lg_ref, s_ref, q_ref, k_ref, v_ref, gate_ref, o_ref, so_ref):
    wide = lambda x: jnp.concatenate([x] * (RET_DV // LANES), axis=1)
    for h in range(RET_HEADS):
        gamma = jnp.exp(jnp.full((1, RET_DV), 1.0, F32) * lg_ref[h])
        state = s_ref[0, h]
        column = lambda ref: jnp.concatenate(
            [_as_column(ref[0, :, h * RET_DK + c * LANES:h * RET_DK + (c + 1) * LANES])
             for c in range(RET_DK // LANES)], axis=0)
        qc = column(q_ref)
        kc = column(k_ref)
        vr = v_ref[0, :, h * RET_DV:(h + 1) * RET_DV]
        qk = jnp.sum(qc * kc, axis=0, keepdims=True)
        qs = jnp.sum(wide(qc) * state, axis=0, keepdims=True)
        o = wide(qk) * vr + qs * gamma
        so_ref[0, h] = state * gamma + wide(kc) * vr
        o = o * lax.rsqrt(jnp.mean(o * o, axis=-1, keepdims=True) + NORM_EPS)
        gate = gate_ref[0, :, h * RET_DV:(h + 1) * RET_DV]
        o_ref[0, :, h * RET_DV:(h + 1) * RET_DV] = (o * (gate * _sigmoid(gate))).astype(o_ref.dtype)


def _retention_step(state, q, k, v, gate, log_gamma):
    nb = state.shape[0]
    st = pl.BlockSpec((1, RET_HEADS, RET_DK, RET_DV), lambda b, lg: (b, 0, 0, 0))
    rowk = pl.BlockSpec((1, 1, RET_QK), lambda b, lg: (b, 0, 0))
    rowv = pl.BlockSpec((1, 1, RET_VD), lambda b, lg: (b, 0, 0))
    o, s_new = pl.pallas_call(
        _ret_step_kernel,
        grid_spec=pltpu.PrefetchScalarGridSpec(
            num_scalar_prefetch=1, grid=(nb,),
            in_specs=[st, rowk, rowk, rowv, rowv], out_specs=[rowv, st]),
        out_shape=[jax.ShapeDtypeStruct((nb, 1, RET_VD), BF16), jax.ShapeDtypeStruct(state.shape, F32)],
        compiler_params=_cparams(("arbitrary",)),
        name="retention_step",
    )(log_gamma, state, q.reshape(nb, 1, RET_QK), k.reshape(nb, 1, RET_QK), v.reshape(nb, 1, RET_VD),
      gate.reshape(nb, 1, RET_VD))
    return o.reshape(nb, RET_VD), s_new


def _ffn_kernel(*refs, seq_mode, blocks_per_seq):
    if seq_mode:
        (x_ref, o_ref, wmix_ref, gmix_ref, gpre_ref, wup_ref, wdw_ref, bdw_ref, wdown_ref, gpost_ref, p_ref,
         wple_ref, wgate_ref, y_ref, tail_ref, h_sc, carry_sc) = refs
    else:
        (x_ref, o_ref, wmix_ref, gmix_ref, gpre_ref, wup_ref, wdw_ref, bdw_ref, wdown_ref, gpost_ref, p_ref,
         wple_ref, wgate_ref, prev2_ref, prev1_ref, y_ref, a_ref, h_sc) = refs
    x = x_ref[...] + _rms(_dot(o_ref[...], wmix_ref[...]), gmix_ref[...])
    bm = x.shape[0]
    xn = _rms(x, gpre_ref[...]).astype(BF16)
    if seq_mode:
        @pl.when(pl.program_id(0) % blocks_per_seq == 0)
        def _():
            carry_sc[...] = jnp.zeros_like(carry_sc)
        row = lax.broadcasted_iota(jnp.int32, (bm, FFN_CHUNK), 0)
    for c in range(D_FF // FFN_CHUNK):
        sl = slice(c * FFN_CHUNK, (c + 1) * FFN_CHUNK)
        a = _dot(xn, wup_ref[:, sl])
        b = _dot(xn, wup_ref[:, D_FF + c * FFN_CHUNK:D_FF + (c + 1) * FFN_CHUNK])
        if seq_mode:
            c2 = carry_sc[SUBLANES - 2:SUBLANES - 1, sl]
            c1 = carry_sc[SUBLANES - 1:SUBLANES, sl]
            a1 = jnp.where(row == 0, c1, pltpu.roll(a, 1, 0))
            a2 = jnp.where(row == 0, c2, jnp.where(row == 1, c1, pltpu.roll(a, 2, 0)))
            last = a[bm - SUBLANES:bm, :]
            carry_sc[:, sl] = last
            tail_ref[0, :, sl] = last
        else:
            a1 = prev1_ref[:, sl]
            a2 = prev2_ref[:, sl]
            a_ref[:, sl] = a
        conv = bdw_ref[:, sl] + a2 * wdw_ref[0:1, sl]
        conv = conv + a1 * wdw_ref[1:2, sl]
        conv = conv + a * wdw_ref[2:3, sl]
        h_sc[:, sl] = (_gelu_tanh(conv) * b).astype(BF16)
    f = _dot(h_sc[...], wdown_ref[...])
    x2 = x + _rms(f, gpost_ref[...])
    ple = _dot(p_ref[...].astype(BF16), wple_ref[...])
    gate = _dot(x2.astype(BF16), wgate_ref[...])
    y_ref[...] = x2 + ple * _sigmoid(gate)


def _ffn(x, o_bf, wmix_bf, gmix, gpre, wup_bf, wdw, bdw, wdown_bf, gpost, p, wple_bf, wgate_bf, bm, seq_len=None,
         prev=None):
    rows = x.shape[0]
    row = lambda i: (i, 0)
    seq_mode = prev is None
    in_specs = [pl.BlockSpec((bm, D_MODEL), row), pl.BlockSpec((bm, o_bf.shape[1]), row), _resident(wmix_bf.shape),
                _resident((1, D_MODEL)), _resident((1, D_MODEL)), _resident(wup_bf.shape),
                _resident(wdw.shape), _resident((1, D_FF)), _resident(wdown_bf.shape), _resident((1, D_MODEL)),
                pl.BlockSpec((bm, PLE_DIM), row), _resident(wple_bf.shape), _resident(wgate_bf.shape)]
    args = [x, o_bf, wmix_bf, gmix, gpre, wup_bf, wdw, bdw, wdown_bf, gpost, p, wple_bf, wgate_bf]
    scratch = [pltpu.VMEM((bm, D_FF), BF16)]
    if seq_mode:
        bps = seq_len // bm
        out_specs = [pl.BlockSpec((bm, D_MODEL), row), pl.BlockSpec((1, SUBLANES, D_FF), lambda i: (i // bps, 0, 0))]
        out_shape = [jax.ShapeDtypeStruct((rows, D_MODEL), F32),
                     jax.ShapeDtypeStruct((rows // seq_len, SUBLANES, D_FF), F32)]
        scratch.append(pltpu.VMEM((SUBLANES, D_FF), F32))
    else:
        bps = 1
        in_specs += [pl.BlockSpec((bm, D_FF), row), pl.BlockSpec((bm, D_FF), row)]
        args += list(prev)
        out_specs = [pl.BlockSpec((bm, D_MODEL), row), pl.BlockSpec((bm, D_FF), row)]
        out_shape = [jax.ShapeDtypeStruct((rows, D_MODEL), F32), jax.ShapeDtypeStruct((rows, D_FF), F32)]
    return pl.pallas_call(
        functools.partial(_ffn_kernel, seq_mode=seq_mode, blocks_per_seq=bps),
        grid=(rows // bm,),
        in_specs=in_specs, out_specs=out_specs, out_shape=out_shape, scratch_shapes=scratch,
        compiler_params=_cparams(("arbitrary",)),
        name="conv_ffn",
    )(*args)


def _kvq_kernel(x_ref, gkv_ref, wkv_ref, wf_ref, bf_ref, gq_ref, wq_ref,
                k_ref, v_ref, lf_ref, kb_ref, vb_ref, qb_ref, kn_ref, qn_ref):
    x = x_ref[...]
    xn = _rms(x, gkv_ref[...]).astype(BF16)
    k = _dot(xn, wkv_ref[:, :FOX_HD])
    v = _dot(xn, wkv_ref[:, FOX_HD:])
    k_ref[...] = k
    v_ref[...] = v
    kb = k.astype(BF16)
    kb_ref[...] = kb
    vb_ref[...] = v.astype(BF16)
    f = _dot(xn, wf_ref[...]) + bf_ref[...]
    lf_ref[...] = _log_sigmoid(f)[:, :FOX_HEADS]
    xq = _rms(x, gq_ref[...]).astype(BF16)
    qb = (_dot(xq, wq_ref[...]) * FOX_DH ** -0.5).astype(BF16)
    qb_ref[...] = qb
    head_sum = (lax.broadcasted_iota(jnp.int32, (FOX_HD, LANES), 0) // FOX_DH ==
                lax.broadcasted_iota(jnp.int32, (FOX_HD, LANES), 1))
    head_sum = jnp.where(head_sum, 1.0, 0.0).astype(BF16)
    for src, dst in ((kb, kn_ref), (qb, qn_ref)):
        sf = src.astype(F32)
        n2 = jnp.max(_dot((sf * sf).astype(BF16), head_sum), axis=0, keepdims=True)
        dst[0] = jnp.broadcast_to(n2, (SUBLANES, LANES))


def _kvq(x, gkv, wkv_bf, wf_bf, bf_pad, gq, wq_bf, bm):
    rows = x.shape[0]
    row = lambda i: (i, 0)
    wide = pl.BlockSpec((bm, FOX_HD), row)
    norm = pl.BlockSpec((1, SUBLANES, LANES), lambda i: (i, 0, 0))
    norm_shape = jax.ShapeDtypeStruct((rows // bm, SUBLANES, LANES), F32)
    return pl.pallas_call(
        _kvq_kernel,
        grid=(rows // bm,),
        in_specs=[pl.BlockSpec((bm, D_MODEL), row), _resident((1, D_MODEL)), _resident(wkv_bf.shape),
                  _resident(wf_bf.shape), _resident((1, LANES)), _resident((1, D_MODEL)), _resident(wq_bf.shape)],
        out_specs=[wide, wide, pl.BlockSpec((bm, FOX_HEADS), row), wide, wide, wide, norm, norm],
        out_shape=[jax.ShapeDtypeStruct((rows, FOX_HD), F32), jax.ShapeDtypeStruct((rows, FOX_HD), F32),
                   jax.ShapeDtypeStruct((rows, FOX_HEADS), F32), jax.ShapeDtypeStruct((rows, FOX_HD), BF16),
                   jax.ShapeDtypeStruct((rows, FOX_HD), BF16), jax.ShapeDtypeStruct((rows, FOX_HD), BF16),
                   norm_shape, norm_shape],
        compiler_params=_cparams(("arbitrary",)),
        name="kv_q_proj",
    )(x, gkv, wkv_bf, wf_bf, bf_pad, gq, wq_bf)


def _cumsum_kernel(lf_ref, cq_ref, ck_ref, first_ref, last_ref, carry_sc):
    bl = lf_ref.shape[0]

    @pl.when(pl.program_id(1) == 0)
    def _():
        carry_sc[...] = jnp.zeros_like(carry_sc)

    lf = lf_ref[...]
    tri = (lax.broadcasted_iota(jnp.int32, (bl, bl), 1) <= lax.broadcasted_iota(jnp.int32, (bl, bl), 0))
    tri = jnp.where(tri, 1.0, 0.0).astype(BF16)
    hi, mid, lo = _split3(lf)
    c = (_dot(tri, lo) + _dot(tri, mid)) + _dot(tri, hi) + carry_sc[...]
    carry_sc[...] = c[bl - 1:bl, :]
    first_ref[0] = c[0:1, :]
    last_ref[0] = c[bl - 1:bl, :]

    lane_of_head = lax.broadcasted_iota(jnp.int32, (FOX_HEADS, LANES), 1) - 3 * lax.broadcasted_iota(
        jnp.int32, (FOX_HEADS, LANES), 0)
    lane = lax.broadcasted_iota(jnp.int32, (1, LANES), 1)
    cq = jnp.where(lane < BIAS_ONES, 1.0, 0.0)
    ck = jnp.where(jnp.logical_and(lane >= BIAS_ONES, lane < 2 * BIAS_ONES), 1.0, 0.0)
    for piece, part in enumerate(_split3(c)):
        cq = cq + _dot(part, jnp.where(lane_of_head == BIAS_ONES + piece, 1.0, 0.0).astype(BF16))
        ck = ck + _dot(part, jnp.where(lane_of_head == piece, -1.0, 0.0).astype(BF16))
    cq_ref[...] = cq.astype(BF16)
    ck_ref[...] = ck.astype(BF16)


def _cumsum_time(lf, batch, seq):
    bl = min(ATT_BLOCK, seq)
    nb = seq // bl
    blk = lambda b, i: (b * nb + i, 0)
    wide = pl.BlockSpec((bl, LANES), blk)
    edge = pl.BlockSpec((1, 1, FOX_HEADS), lambda b, i: (b * nb + i, 0, 0))
    return pl.pallas_call(
        _cumsum_kernel,
        grid=(batch, nb),
        in_specs=[pl.BlockSpec((bl, FOX_HEADS), blk)], out_specs=[wide, wide, edge, edge],
        out_shape=[jax.ShapeDtypeStruct((lf.shape[0], LANES), BF16)] * 2
        + [jax.ShapeDtypeStruct((batch * nb, 1, FOX_HEADS), F32)] * 2,
        scratch_shapes=[pltpu.VMEM((1, FOX_HEADS), F32)],
        compiler_params=_cparams(("arbitrary", "arbitrary")),
        name="logf_cumsum",
    )(lf)


def _fox_prompt_kernel(cfirst_ref, clast_ref, qkb_ref, q_ref, k_ref, v_ref, cq_ref, ck_ref, o_ref, m_sc, acc_sc):
    b = pl.program_id(0)
    hg = pl.program_id(1)
    i = pl.program_id(2)

    def dead(j):
        worst = None
        for hh in range(ATT_HEADS):
            h = ATT_HEADS * hg + hh
            gap = cfirst_ref[b, i, h] - clast_ref[b, j, h] + 2.0 * qkb_ref[b, h]
            worst = gap if worst is None else jnp.maximum(worst, gap)
        return worst < -EXP_UNDERFLOW

    j_lo = lax.while_loop(lambda j: jnp.logical_and(j < i, dead(j)), lambda j: j + 1, jnp.int32(0))
    bq = q_ref.shape[0]
    pair = 2 * FOX_DH
    pair_lanes = lambda ref, rows, p: ref[rows, p * pair:(p + 1) * pair]
    low = lax.broadcasted_iota(jnp.int32, (1, pair), 1) < FOX_DH
    causal = (lax.broadcasted_iota(jnp.int32, (bq, bq), 1) <= lax.broadcasted_iota(jnp.int32, (bq, bq), 0))
    ones = jnp.ones((bq, pair), BF16)
    lanes = lambda x, n: jnp.concatenate([x] * n, axis=1)
    bias_lane = lax.broadcasted_iota(jnp.int32, (1, LANES), 1)
    cq = cq_ref[...]
    q_aug = []
    for hh in range(ATT_HEADS):
        q = pair_lanes(q_ref, slice(None), hh // 2)
        qm = jnp.where(low if hh % 2 == 0 else jnp.logical_not(low), q, jnp.zeros_like(q))
        first = 3 * (ATT_HEADS * hg + hh)
        own = jnp.logical_or(jnp.logical_and(bias_lane >= first, bias_lane < first + 3),
                             jnp.logical_and(bias_lane >= first + BIAS_ONES, bias_lane < first + BIAS_ONES + 3))
        q_aug.append(jnp.concatenate([qm, jnp.where(own, cq, jnp.zeros_like(cq))], axis=1))
        m_sc[hh] = jnp.full(m_sc.shape[1:], -jnp.inf, F32)
        acc_sc[hh] = jnp.zeros(acc_sc.shape[1:], F32)

    def block(j, masked):
        rows = pl.ds(pl.multiple_of(j * bq, bq), bq)
        ck = ck_ref[rows, :]
        k_aug = [jnp.concatenate([pair_lanes(k_ref, rows, p), ck], axis=1) for p in range(ATT_HEADS // 2)]
        v_aug = [jnp.concatenate([pair_lanes(v_ref, rows, p), ones], axis=1) for p in range(ATT_HEADS // 2)]
        for hh in range(ATT_HEADS):
            s = _dot_nt(q_aug[hh], k_aug[hh // 2])
            if masked:
                s = jnp.where(causal, s, -jnp.inf)
            m_prev = m_sc[hh]
            m_next = jnp.maximum(m_prev, jnp.max(s, axis=1, keepdims=True))
            alpha = jnp.exp(m_prev - m_next)
            p = jnp.exp(s - lanes(m_next, bq // LANES))
            acc_sc[hh] = lanes(alpha, 2) * acc_sc[hh] + _dot(p.astype(BF16), v_aug[hh // 2])
            m_sc[hh] = m_next

    n_live = i - j_lo + 1

    @pl.when(n_live == 1)
    def _():
        block(i, True)

    @pl.when(n_live == 2)
    def _():
        block(i, True)
        block(i - 1, False)

    @pl.when(n_live >= 3)
    def _():
        block(i, True)
        block(i - 1, False)
        block(i - 2, False)

    def body(t, carry):
        block(i - 3 - 2 * t, False)
        block(i - 4 - 2 * t, False)
        return carry

    rest = jnp.maximum(n_live - 3, 0)
    lax.fori_loop(0, rest // 2, body, 0)

    @pl.when(rest % 2 == 1)
    def _():
        block(j_lo, False)

    for p in range(ATT_HEADS // 2):
        o0 = acc_sc[2 * p, :, :pair] / acc_sc[2 * p, :, pair:]
        o1 = acc_sc[2 * p + 1, :, :pair] / acc_sc[2 * p + 1, :, pair:]
        o_ref[:, p * pair:(p + 1) * pair] = jnp.where(low, o0, o1).astype(o_ref.dtype)


def _fox_prompt(q_bf, k_bf, v_bf, cq, ck, c_first, c_last, qk_bound, batch, seq):
    bq = min(ATT_BLOCK, seq)
    nq = seq // bq
    pair = 2 * FOX_DH
    width = ATT_HEADS * FOX_DH
    qblk = pl.BlockSpec((bq, width), lambda b, hg, i, *_: (b * nq + i, hg))
    kvblk = pl.BlockSpec((seq, width), lambda b, hg, i, *_: (b, hg))
    return pl.pallas_call(
        _fox_prompt_kernel,
        grid_spec=pltpu.PrefetchScalarGridSpec(
            num_scalar_prefetch=3,
            grid=(batch, FOX_HEADS // ATT_HEADS, nq),
            in_specs=[qblk, kvblk, kvblk,
                      pl.BlockSpec((bq, LANES), lambda b, hg, i, *_: (b * nq + i, 0)),
                      pl.BlockSpec((seq, LANES), lambda b, hg, i, *_: (b, 0))],
            out_specs=qblk,
            scratch_shapes=[pltpu.VMEM((ATT_HEADS, bq, LANES), F32), pltpu.VMEM((ATT_HEADS, bq, 2 * pair), F32)]),
        out_shape=jax.ShapeDtypeStruct((batch * seq, FOX_HD), BF16),
        compiler_params=_cparams(("arbitrary", "arbitrary", "arbitrary")),
        name="fox_attention_prompt",
    )(c_first, c_last, qk_bound, q_bf, k_bf, v_bf, cq, ck)


def _fox_scores_kernel(pt_ref, q_ref, knew_ref, lfnew_ref, *refs, pp):
    k_refs, lf_refs = refs[:pp], refs[pp:2 * pp]
    s_ref, pmax_ref, m_ref, sself_ref, q_sc, carry_sc = refs[2 * pp:]
    step = pl.program_id(1)
    page = k_refs[0].shape[3]

    @pl.when(step == 0)
    def _():
        for hp in range(FOX_HEADS // 2):
            qc, kc = [_as_column(r[0, :, hp * LANES:(hp + 1) * LANES]) for r in (q_ref, knew_ref)]
            for hl in range(2):
                h = 2 * hp + hl
                qh = qc[hl * FOX_DH:(hl + 1) * FOX_DH]
                q_sc[h] = qh
                sself_ref[0, h:h + 1, :] = jnp.sum(qh * kc[hl * FOX_DH:(hl + 1) * FOX_DH], axis=0, keepdims=True)
        m_ref[0] = sself_ref[0]
        carry_sc[...] = _as_column(lfnew_ref[0])[:FOX_HEADS]

    q = q_sc[...]

    later = (lax.broadcasted_iota(jnp.int32, (page, page), 0) > lax.broadcasted_iota(jnp.int32, (page, page), 1))
    later = jnp.where(later, 1.0, 0.0).astype(BF16)
    carry = carry_sc[...]
    m = m_ref[0]
    for i in range(pp):
        lf = lf_refs[i][0]
        hi, mid, lo = _split3(lf)
        bias = (_dot(lo, later) + _dot(mid, later)) + _dot(hi, later) + carry
        carry = carry + jnp.sum(lf, axis=1, keepdims=True)
        s = jnp.sum(k_refs[i][0] * q, axis=1) + bias
        s_ref[0, :, i * page:(i + 1) * page] = s
        page_max = jnp.max(s, axis=1, keepdims=True)
        pmax_ref[0, i] = jnp.broadcast_to(page_max, (FOX_HEADS, LANES))
        m = jnp.maximum(m, page_max)
    carry_sc[...] = carry
    m_ref[0] = m


def _fox_values_kernel(page_ref, walk_ref, count_ref, s_ref, m_ref, sself_ref, vnew_ref, *refs, pp):
    v_refs = refs[:pp]
    o_ref, l_sc, acc_sc = refs[pp:]
    b = pl.program_id(0)
    step = pl.program_id(1)
    page = v_refs[0].shape[3]
    m = m_ref[0]

    @pl.when(step == 0)
    def _():
        p_self = jnp.exp(sself_ref[0] - m)
        l_sc[...] = p_self
        lane = lax.broadcasted_iota(jnp.int32, (FOX_DH, page), 1)
        for hp in range(FOX_HEADS // 2):
            vc = _as_column(vnew_ref[0, :, hp * LANES:(hp + 1) * LANES])
            for hl in range(2):
                h = 2 * hp + hl
                acc_sc[h] = jnp.where(lane == 0, vc[hl * FOX_DH:(hl + 1) * FOX_DH] * p_self[h:h + 1, :], 0.0)

    n_live = count_ref[b]
    for i in range(pp):
        slot = step * pp + i

        @pl.when(slot < n_live)
        def _():
            start = pl.multiple_of(walk_ref[b, slot] * page, page)
            p = jnp.exp(s_ref[0, :, pl.ds(start, page)] - m)
            l_sc[...] = l_sc[...] + jnp.sum(p, axis=1, keepdims=True)
            for h in range(FOX_HEADS):
                acc_sc[h] = acc_sc[h] + v_refs[i][0, h] * p[h:h + 1, :]

    @pl.when(step == pl.num_programs(1) - 1)
    def _():
        o_ref[0] = (jnp.sum(acc_sc[...], axis=2) / l_sc[:, :1]).astype(o_ref.dtype)


def _fox_decode(q, k_new, v_new, lf_new, cache_k, cache_v, cache_logf, page_table):
    nb, n_pages = page_table.shape
    page = cache_k.shape[1]
    assert page == LANES, "one cache page fills the lane axis"
    pp = min(PAGES_PER_STEP, n_pages)
    n_steps = n_pages // pp
    ck = jnp.transpose(cache_k, (0, 2, 3, 1))
    cv = jnp.transpose(cache_v, (0, 2, 3, 1))
    clf = jnp.transpose(cache_logf, (0, 2, 1))
    as_row = lambda a: a.reshape(nb, 1, FOX_HD)
    lf_row = jnp.pad(lf_new, ((0, 0), (0, LANES - FOX_HEADS))).reshape(nb, 1, LANES)
    heads_lanes = jax.ShapeDtypeStruct((nb, FOX_HEADS, LANES), F32)

    def k_page(i):
        return pl.BlockSpec((1, FOX_HEADS, FOX_DH, page),
                            lambda b, s, pt: (pt[b, n_pages - 1 - (s * pp + i)], 0, 0, 0))

    def lf_page(i):
        return pl.BlockSpec((1, FOX_HEADS, page), lambda b, s, pt: (pt[b, n_pages - 1 - (s * pp + i)], 0, 0))

    row = pl.BlockSpec((1, 1, FOX_HD), lambda b, s, *_: (b, 0, 0))
    per_seq = pl.BlockSpec((1, FOX_HEADS, LANES), lambda b, s, *_: (b, 0, 0))
    scores, page_max, m, s_self = pl.pallas_call(
        functools.partial(_fox_scores_kernel, pp=pp),
        grid_spec=pltpu.PrefetchScalarGridSpec(
            num_scalar_prefetch=1, grid=(nb, n_steps),
            in_specs=[row, row, pl.BlockSpec((1, 1, LANES), lambda b, s, pt: (b, 0, 0))]
            + [k_page(i) for i in range(pp)] + [lf_page(i) for i in range(pp)],
            out_specs=[pl.BlockSpec((1, FOX_HEADS, pp * page), lambda b, s, pt: (b, 0, s)),
                       pl.BlockSpec((1, pp, FOX_HEADS, LANES), lambda b, s, pt: (b, s, 0, 0)), per_seq, per_seq],
            scratch_shapes=[pltpu.VMEM((FOX_HEADS, FOX_DH, page), F32), pltpu.VMEM((FOX_HEADS, page), F32)]),
        out_shape=[jax.ShapeDtypeStruct((nb, FOX_HEADS, n_pages * page), F32),
                   jax.ShapeDtypeStruct((nb, n_pages, FOX_HEADS, LANES), F32), heads_lanes, heads_lanes],
        compiler_params=_cparams(("arbitrary", "arbitrary")),
        name="fox_decode_scores",
    )(page_table, as_row(q), as_row(k_new), lf_row, *([ck] * pp), *([clf] * pp))

    live = jnp.any(page_max[:, :, :, 0] - m[:, None, :, 0] >= -EXP_UNDERFLOW, axis=2)
    count = jnp.sum(live, axis=1).astype(jnp.int32)
    walk = jnp.argsort(jnp.logical_not(live), axis=1, stable=True).astype(jnp.int32)
    pages = jnp.take_along_axis(page_table[:, ::-1], walk, axis=1)
    pp = min(VALUE_PAGES_PER_STEP, n_pages)
    n_steps = n_pages // pp
    is_live = (jnp.arange(n_pages, dtype=jnp.int32)[None, :] < count[:, None]).reshape(nb * n_steps, pp)
    flat = pages.reshape(nb * n_steps, pp)
    last_live = lax.cummax(jnp.where(is_live, jnp.arange(nb * n_steps, dtype=jnp.int32)[:, None], 0), axis=0)
    pages = jnp.take_along_axis(flat, last_live, axis=0).reshape(nb, n_pages)

    def v_page(i):
        return pl.BlockSpec((1, FOX_HEADS, FOX_DH, page), lambda b, s, pg, wk, ct: (pg[b, s * pp + i], 0, 0, 0))

    out = pl.pallas_call(
        functools.partial(_fox_values_kernel, pp=pp),
        grid_spec=pltpu.PrefetchScalarGridSpec(
            num_scalar_prefetch=3, grid=(nb, n_steps),
            in_specs=[pl.BlockSpec((1, FOX_HEADS, n_pages * page), lambda b, s, *_: (b, 0, 0)), per_seq, per_seq, row]
            + [v_page(i) for i in range(pp)],
            out_specs=pl.BlockSpec((1, FOX_HEADS, FOX_DH), lambda b, s, *_: (b, 0, 0)),
            scratch_shapes=[pltpu.VMEM((FOX_HEADS, LANES), F32), pltpu.VMEM((FOX_HEADS, FOX_DH, page), F32)]),
        out_shape=jax.ShapeDtypeStruct((nb, FOX_HEADS, FOX_DH), BF16),
        compiler_params=_cparams(("arbitrary", "arbitrary")),
        name="fox_decode_values",
    )(pages, walk, count, scores, m, s_self, as_row(v_new), *([cv] * pp))
    return out.reshape(nb, FOX_HD)


def _run_group(x, p, cos, sin, w, bm, seq_len, sample):
    rows = x.shape[0]
    qkv_dtype = BF16 if sample is None else F32
    q, k, v, gate = _ret_in(x, w["g_pre_mix"][0], w["w_in"], cos, sin, bm, qkv_dtype)
    if sample is None:
        batch = rows // seq_len
        o, ret_state = _retention_prompt(q, k, v, gate, w["log_gamma"], batch, seq_len)
    else:
        o, ret_state = _retention_step(sample["state_ret"], q, k, v, gate, w["log_gamma"])

    conv_out = []
    ffn_args = lambda i: (w["g_post_mix"][i], w["g_pre_ffn"][i], w["w_up"][i], w["w_dw"][i], w["b_dw"][i],
                          w["w_down"][i], w["g_post_ffn"][i], p[i], w["w_ple"][i], w["w_gate"][i], bm)
    if sample is None:
        x, tail = _ffn(x, o, w["w_out_ret"], *ffn_args(0), seq_len=seq_len)
        conv_out.append(tail[:, SUBLANES - 2:, :])
    else:
        buf = sample["state_conv"][0]
        x, a = _ffn(x, o, w["w_out_ret"], *ffn_args(0), prev=(buf[:, 0], buf[:, 1]))
        conv_out.append(jnp.stack([buf[:, 1], a], axis=1))

    k_new, v_new, lf_new, k_bf, v_bf, q_bf, k_n2, q_n2 = _kvq(
        x, w["g_kv"], w["w_kv"], w["w_f"], w["b_f"], w["g_pre_mix"][1], w["w_q"], bm)
    if sample is None:
        cq, ck, c_first, c_last = _cumsum_time(lf_new, batch, seq_len)
        per_seq_max = lambda n2: jnp.max(n2.reshape(batch, -1, SUBLANES, LANES)[:, :, 0, :FOX_HEADS], axis=1)
        qk_bound = jnp.sqrt(per_seq_max(k_n2) * per_seq_max(q_n2)) * NORM_PAD
        edges = lambda e: e.reshape(batch, -1, FOX_HEADS)
        att = _fox_prompt(q_bf, k_bf, v_bf, cq, ck, edges(c_first), edges(c_last), qk_bound, batch, seq_len)
    else:
        att = _fox_decode(q_bf.astype(F32), k_new, v_new, lf_new, sample["cache_k"], sample["cache_v"],
                          sample["cache_logf"], sample["page_table"])

    if sample is None:
        x, tail = _ffn(x, att, w["w_out_fox"], *ffn_args(1), seq_len=seq_len)
        conv_out.append(tail[:, SUBLANES - 2:, :])
    else:
        buf = sample["state_conv"][1]
        x, a = _ffn(x, att, w["w_out_fox"], *ffn_args(1), prev=(buf[:, 0], buf[:, 1]))
        conv_out.append(jnp.stack([buf[:, 1], a], axis=1))
    return x, ret_state, jnp.stack(conv_out), k_new, v_new, lf_new


def kernel(x_prompt, x_sample, state_ret, state_conv, cache_k, cache_v, cache_logf, page_table, p_prompt, p_sample,
           norm_pre_mix, norm_post_mix, norm_pre_ffn, norm_post_ffn, w_in_ret, w_out_ret, norm_kv, w_kvf, b_f,
           w_q_fox, w_out_fox, w_up, w_dw, b_dw, w_down, w_ple, w_ple_gate):
    bp, tp, _ = x_prompt.shape
    nb, ts, _ = x_sample.shape
    assert ts == 1, "the sample group is one new token per sequence"
    past_len = page_table.shape[1] * cache_k.shape[1]
    depth = w_up.shape[0]
    row_vec = lambda a: a.reshape(a.shape[0], 1, a.shape[1])
    w = {
        "g_pre_mix": row_vec(norm_pre_mix), "g_post_mix": row_vec(norm_post_mix),
        "g_pre_ffn": row_vec(norm_pre_ffn), "g_post_ffn": row_vec(norm_post_ffn),
        "g_kv": norm_kv.reshape(1, D_MODEL),
        "w_in": w_in_ret[0].astype(BF16), "w_out_ret": w_out_ret[0].astype(BF16),
        "w_kv": w_kvf[:, :2 * FOX_HD].astype(BF16),
        "w_f": jnp.pad(w_kvf[:, 2 * FOX_HD:], ((0, 0), (0, LANES - FOX_HEADS))).astype(BF16),
        "b_f": jnp.pad(b_f, (0, LANES - FOX_HEADS)).reshape(1, LANES),
        "w_q": w_q_fox[0].astype(BF16), "w_out_fox": w_out_fox[0].astype(BF16),
        "w_up": w_up.astype(BF16), "w_dw": w_dw, "b_dw": row_vec(b_dw), "w_down": w_down.astype(BF16),
        "w_ple": w_ple.astype(BF16), "w_gate": w_ple_gate.astype(BF16),
        "log_gamma": jnp.log1p(-jnp.exp2(-5.0 - jnp.arange(RET_HEADS, dtype=F32))),
    }
    bm = min(ROW_BLOCK, tp)
    cos_p, sin_p = _rope_table(tp, 0)
    y_p, ret_p, conv_p, k_p, v_p, lf_p = _run_group(
        x_prompt.reshape(bp * tp, D_MODEL), p_prompt.reshape(depth, bp * tp, PLE_DIM), cos_p, sin_p, w, bm, tp, None)
    cos_s, sin_s = _rope_table(SUBLANES, past_len)
    cos_s = jnp.broadcast_to(cos_s[:1], (nb, ROPE_HALF))
    sin_s = jnp.broadcast_to(sin_s[:1], (nb, ROPE_HALF))
    sample = {"state_ret": state_ret[0], "state_conv": state_conv, "cache_k": cache_k, "cache_v": cache_v,
              "cache_logf": cache_logf, "page_table": page_table}
    y_s, ret_s, conv_s, k_s, v_s, lf_s = _run_group(
        x_sample.reshape(nb, D_MODEL), p_sample.reshape(depth, nb, PLE_DIM), cos_s, sin_s, w, nb, 1, sample)
    return (y_p.reshape(bp, tp, D_MODEL), y_s.reshape(nb, 1, D_MODEL),
            ret_p[None], ret_s[None], conv_p, conv_s,
            k_p.reshape(bp, tp, FOX_HEADS, FOX_DH), k_s.reshape(nb, 1, FOX_HEADS, FOX_DH),
            v_p.reshape(bp, tp, FOX_HEADS, FOX_DH), v_s.reshape(nb, 1, FOX_HEADS, FOX_DH),
            lf_p.reshape(bp, tp, FOX_HEADS), lf_s.reshape(nb, 1, FOX_HEADS))
```

```python
import functools
import math

import jax
import jax.numpy as jnp
from jax import lax
from jax.experimental import pallas as pl
from jax.experimental.pallas import tpu as pltpu

F32 = jnp.float32
BF16 = jnp.bfloat16

D_MODEL = 1024
RET_HEADS = 4
RET_DK = 256
RET_DV = 512
RET_QK = RET_HEADS * RET_DK
RET_VD = RET_HEADS * RET_DV
ROPE_BASE = 10000.0
ROPE_HALF = RET_DK // 2
FOX_HEADS = 16
FOX_DH = 64
FOX_HD = FOX_HEADS * FOX_DH
D_FF = 2816
PLE_DIM = 256
NORM_EPS = 1e-6

LANES = 128
SUBLANES = 8
VMEM_LIMIT = 56 * 1024 * 1024

ROW_BLOCK = 512
RET_CHUNK = 256
FFN_CHUNK = 256
ATT_BLOCK = 512
ATT_HEADS = 8
PAGES_PER_STEP = 16
VALUE_PAGES_PER_STEP = 32
BIAS_ONES = 48
EXP_UNDERFLOW = 110.0
NORM_PAD = 1.05


def _cparams(sem):
    return pltpu.CompilerParams(dimension_semantics=sem, vmem_limit_bytes=VMEM_LIMIT)


def _resident(shape):
    return pl.BlockSpec(shape, lambda *_: (0,) * len(shape), pipeline_mode=pl.Buffered(1))


def _dot(a, b):
    return jnp.dot(a, b, preferred_element_type=F32)


def _dot_nt(a, b):
    return lax.dot_general(a, b, (((1,), (1,)), ((), ())), preferred_element_type=F32)


def _dot_tn(a, b):
    return lax.dot_general(a, b, (((0,), (0,)), ((), ())), preferred_element_type=F32)


def _rms(x, g):
    return x * lax.rsqrt(jnp.mean(x * x, axis=-1, keepdims=True) + NORM_EPS) * g


def _sigmoid(x):
    return 1.0 / (1.0 + jnp.exp(-x))


def _gelu_tanh(x):
    return x * (0.5 * (1.0 + jnp.tanh(math.sqrt(2.0 / math.pi) * (x + 0.044715 * (x * x * x)))))


def _log_sigmoid(x):
    z = -x
    return -(jnp.maximum(z, 0.0) + jnp.log1p(jnp.exp(-jnp.abs(z))))


def _split3(x):
    hi = x.astype(BF16)
    r1 = x - hi.astype(F32)
    mid = r1.astype(BF16)
    lo = (r1 - mid.astype(F32)).astype(BF16)
    return hi, mid, lo


def _rope_kernel(cos_ref, sin_ref, *, base_pos):
    n = cos_ref.shape[0]
    i = pl.program_id(0)
    lane = lax.broadcasted_iota(jnp.int32, (1, ROPE_HALF), 1).astype(F32)
    inv = jnp.power(jnp.full((1, ROPE_HALF), ROPE_BASE, F32), -(lane / ROPE_HALF))
    pos = (lax.broadcasted_iota(jnp.int32, (n, ROPE_HALF), 0) + (i * n + base_pos)).astype(F32)
    ang = pos * inv
    cos_ref[...] = jnp.cos(ang)
    sin_ref[...] = jnp.sin(ang)


def _rope_table(n_rows, base_pos):
    bn = min(n_rows, 1024)
    spec = pl.BlockSpec((bn, ROPE_HALF), lambda i: (i, 0))
    return pl.pallas_call(
        functools.partial(_rope_kernel, base_pos=base_pos),
        grid=(n_rows // bn,),
        out_specs=[spec, spec],
        out_shape=[jax.ShapeDtypeStruct((n_rows, ROPE_HALF), F32)] * 2,
        compiler_params=_cparams(("arbitrary",)),
        name="rope_table",
    )()


def _ret_in_kernel(x_ref, g_ref, w_ref, cos_ref, sin_ref, q_ref, k_ref, v_ref, gate_ref):
    xn = _rms(x_ref[...], g_ref[...]).astype(BF16)
    cos = cos_ref[...]
    sin = sin_ref[...]
    for out_ref, base, scale in ((q_ref, 0, 1.0), (k_ref, RET_QK, RET_DK ** -0.5)):
        for h in range(RET_HEADS):
            lo = h * RET_DK
            u = _dot(xn, w_ref[:, base + lo:base + lo + RET_DK])
            x1, x2 = u[:, :ROPE_HALF], u[:, ROPE_HALF:]
            out_ref[:, lo:lo + ROPE_HALF] = ((x1 * cos - x2 * sin) * scale).astype(out_ref.dtype)
            out_ref[:, lo + ROPE_HALF:lo + RET_DK] = ((x1 * sin + x2 * cos) * scale).astype(out_ref.dtype)
    for h in range(RET_HEADS):
        lo = h * RET_DV
        v_ref[:, lo:lo + RET_DV] = _dot(xn, w_ref[:, 2 * RET_QK + lo:2 * RET_QK + lo + RET_DV]).astype(v_ref.dtype)
        gate_ref[:, lo:lo + RET_DV] = _dot(xn, w_ref[:, 2 * RET_QK + RET_VD + lo:2 * RET_QK + RET_VD + lo + RET_DV])


def _ret_in(x, g, w_bf, cos, sin, bm, qkv_dtype):
    rows = x.shape[0]
    n_tab = cos.shape[0] // bm
    row = lambda i: (i, 0)
    tab = pl.BlockSpec((bm, ROPE_HALF), lambda i: (i % n_tab, 0))
    return pl.pallas_call(
        _ret_in_kernel,
        grid=(rows // bm,),
        in_specs=[pl.BlockSpec((bm, D_MODEL), row), _resident((1, D_MODEL)), _resident(w_bf.shape), tab, tab],
        out_specs=[pl.BlockSpec((bm, RET_QK), row), pl.BlockSpec((bm, RET_QK), row),
                   pl.BlockSpec((bm, RET_VD), row), pl.BlockSpec((bm, RET_VD), row)],
        out_shape=[jax.ShapeDtypeStruct((rows, RET_QK), qkv_dtype), jax.ShapeDtypeStruct((rows, RET_QK), qkv_dtype),
                   jax.ShapeDtypeStruct((rows, RET_VD), qkv_dtype), jax.ShapeDtypeStruct((rows, RET_VD), F32)],
        compiler_params=_cparams(("arbitrary",)),
        name="ret_in",
    )(x, g, w_bf, cos, sin)


def _ret_chunk_kernel(lg_ref, q_ref, k_ref, v_ref, gate_ref, o_ref, s_ref, decay_sc):
    c = pl.program_id(1)
    C = q_ref.shape[0]

    @pl.when(jnp.logical_and(pl.program_id(0) == 0, c == 0))
    def _():
        rel = lax.broadcasted_iota(jnp.int32, (C, C), 0) - lax.broadcasted_iota(jnp.int32, (C, C), 1)
        dist = jnp.maximum(rel, 0).astype(F32)
        for h in range(RET_HEADS):
            decay_sc[h] = jnp.where(rel >= 0, jnp.exp(dist * lg_ref[h]), 0.0)

    @pl.when(c == 0)
    def _():
        s_ref[...] = jnp.zeros_like(s_ref)

    n = lax.broadcasted_iota(jnp.int32, (C, 1), 0).astype(F32)
    for h in range(RET_HEADS):
        lg = lg_ref[h]
        q = q_ref[:, h * RET_DK:(h + 1) * RET_DK]
        k = k_ref[:, h * RET_DK:(h + 1) * RET_DK]
        v = v_ref[:, h * RET_DV:(h + 1) * RET_DV]
        scores = _dot_nt(q, k) * decay_sc[h]
        inner = _dot(scores.astype(BF16), v)
        state = s_ref[0, h]
        cross = _dot(q, state.astype(BF16)) * jnp.exp((n + 1.0) * lg)
        kd = (k.astype(F32) * jnp.exp((C - 1.0 - n) * lg)).astype(BF16)
        chunk_decay = jnp.exp(jnp.full((1, RET_DV), float(C), F32) * lg)
        s_ref[0, h] = state * chunk_decay + _dot_tn(kd, v)
        o = inner + cross
        o = o * lax.rsqrt(jnp.mean(o * o, axis=-1, keepdims=True) + NORM_EPS)
        gate = gate_ref[:, h * RET_DV:(h + 1) * RET_DV]
        o_ref[:, h * RET_DV:(h + 1) * RET_DV] = (o * (gate * _sigmoid(gate))).astype(o_ref.dtype)


def _retention_prompt(q, k, v, gate, log_gamma, batch, seq):
    C = min(RET_CHUNK, seq)
    nc = seq // C
    blk = lambda b, c, lg: (b * nc + c, 0)
    return pl.pallas_call(
        _ret_chunk_kernel,
        grid_spec=pltpu.PrefetchScalarGridSpec(
            num_scalar_prefetch=1,
            grid=(batch, nc),
            in_specs=[pl.BlockSpec((C, RET_QK), blk), pl.BlockSpec((C, RET_QK), blk),
                      pl.BlockSpec((C, RET_VD), blk), pl.BlockSpec((C, RET_VD), blk)],
            out_specs=[pl.BlockSpec((C, RET_VD), blk),
                       pl.BlockSpec((1, RET_HEADS, RET_DK, RET_DV), lambda b, c, lg: (b, 0, 0, 0))],
            scratch_shapes=[pltpu.VMEM((RET_HEADS, C, C), F32)],
        ),
        out_shape=[jax.ShapeDtypeStruct((batch * seq, RET_VD), BF16),
                   jax.ShapeDtypeStruct((batch, RET_HEADS, RET_DK, RET_DV), F32)],
        compiler_params=_cparams(("arbitrary", "arbitrary")),
        name="retention_chunks",
    )(log_gamma, q, k, v, gate)


def _as_column(row):
    return jnp.transpose(jnp.broadcast_to(row, (LANES, LANES)))


def _ret_step_kernel(lg_ref, s_ref, q_ref, k_ref, v_ref, gate_ref, o_ref, so_ref):
    wide = lambda x: jnp.concatenate([x] * (RET_DV // LANES), axis=1)
    for h in range(RET_HEADS):
        gamma = jnp.exp(jnp.full((1, RET_DV), 1.0, F32) * lg_ref[h])
        state = s_ref[0, h]
        column = lambda ref: jnp.concatenate(
            [_as_column(ref[0, :, h * RET_DK + c * LANES:h * RET_DK + (c + 1) * LANES])
             for c in range(RET_DK // LANES)], axis=0)
        qc = column(q_ref)
        kc = column(k_ref)
        vr = v_ref[0, :, h * RET_DV:(h + 1) * RET_DV]
        qk = jnp.sum(qc * kc, axis=0, keepdims=True)
        qs = jnp.sum(wide(qc) * state, axis=0, keepdims=True)
        o = wide(qk) * vr + qs * gamma
        so_ref[0, h] = state * gamma + wide(kc) * vr
        o = o * lax.rsqrt(jnp.mean(o * o, axis=-1, keepdims=True) + NORM_EPS)
        gate = gate_ref[0, :, h * RET_DV:(h + 1) * RET_DV]
        o_ref[0, :, h * RET_DV:(h + 1) * RET_DV] = (o * (gate * _sigmoid(gate))).astype(o_ref.dtype)


def _retention_step(state, q, k, v, gate, log_gamma):
    nb = state.shape[0]
    st = pl.BlockSpec((1, RET_HEADS, RET_DK, RET_DV), lambda b, lg: (b, 0, 0, 0))
    rowk = pl.BlockSpec((1, 1, RET_QK), lambda b, lg: (b, 0, 0))
    rowv = pl.BlockSpec((1, 1, RET_VD), lambda b, lg: (b, 0, 0))
    o, s_new = pl.pallas_call(
        _ret_step_kernel,
        grid_spec=pltpu.PrefetchScalarGridSpec(
            num_scalar_prefetch=1, grid=(nb,),
            in_specs=[st, rowk, rowk, rowv, rowv], out_specs=[rowv, st]),
        out_shape=[jax.ShapeDtypeStruct((nb, 1, RET_VD), BF16), jax.ShapeDtypeStruct(state.shape, F32)],
        compiler_params=_cparams(("arbitrary",)),
        name="retention_step",
    )(log_gamma, state, q.reshape(nb, 1, RET_QK), k.reshape(nb, 1, RET_QK), v.reshape(nb, 1, RET_VD),
      gate.reshape(nb, 1, RET_VD))
    return o.reshape(nb, RET_VD), s_new


def _ffn_kernel(*refs, seq_mode, blocks_per_seq):
    if seq_mode:
        (x_ref, o_ref, wmix_ref, gmix_ref, gpre_ref, wup_ref, wdw_ref, bdw_ref, wdown_ref, gpost_ref, p_ref,
         wple_ref, wgate_ref, y_ref, tail_ref, h_sc, carry_sc) = refs
    else:
        (x_ref, o_ref, wmix_ref, gmix_ref, gpre_ref, wup_ref, wdw_ref, bdw_ref, wdown_ref, gpost_ref, p_ref,
         wple_ref, wgate_ref, prev2_ref, prev1_ref, y_ref, a_ref, h_sc) = refs
    x = x_ref[...] + _rms(_dot(o_ref[...], wmix_ref[...]), gmix_ref[...])
    bm = x.shape[0]
    xn = _rms(x, gpre_ref[...]).astype(BF16)
    if seq_mode:
        @pl.when(pl.program_id(0) % blocks_per_seq == 0)
        def _():
            carry_sc[...] = jnp.zeros_like(carry_sc)
        row = lax.broadcasted_iota(jnp.int32, (bm, FFN_CHUNK), 0)
    for c in range(D_FF // FFN_CHUNK):
        sl = slice(c * FFN_CHUNK, (c + 1) * FFN_CHUNK)
        a = _dot(xn, wup_ref[:, sl])
        b = _dot(xn, wup_ref[:, D_FF + c * FFN_CHUNK:D_FF + (c + 1) * FFN_CHUNK])
        if seq_mode:
            c2 = carry_sc[SUBLANES - 2:SUBLANES - 1, sl]
            c1 = carry_sc[SUBLANES - 1:SUBLANES, sl]
            a1 = jnp.where(row == 0, c1, pltpu.roll(a, 1, 0))
            a2 = jnp.where(row == 0, c2, jnp.where(row == 1, c1, pltpu.roll(a, 2, 0)))
            last = a[bm - SUBLANES:bm, :]
            carry_sc[:, sl] = last
            tail_ref[0, :, sl] = last
        else:
            a1 = prev1_ref[:, sl]
            a2 = prev2_ref[:, sl]
            a_ref[:, sl] = a
        conv = bdw_ref[:, sl] + a2 * wdw_ref[0:1, sl]
        conv = conv + a1 * wdw_ref[1:2, sl]
        conv = conv + a * wdw_ref[2:3, sl]
        h_sc[:, sl] = (_gelu_tanh(conv) * b).astype(BF16)
    f = _dot(h_sc[...], wdown_ref[...])
    x2 = x + _rms(f, gpost_ref[...])
    ple = _dot(p_ref[...].astype(BF16), wple_ref[...])
    gate = _dot(x2.astype(BF16), wgate_ref[...])
    y_ref[...] = x2 + ple * _sigmoid(gate)


def _ffn(x, o_bf, wmix_bf, gmix, gpre, wup_bf, wdw, bdw, wdown_bf, gpost, p, wple_bf, wgate_bf, bm, seq_len=None,
         prev=None):
    rows = x.shape[0]
    row = lambda i: (i, 0)
    seq_mode = prev is None
    in_specs = [pl.BlockSpec((bm, D_MODEL), row), pl.BlockSpec((bm, o_bf.shape[1]), row), _resident(wmix_bf.shape),
                _resident((1, D_MODEL)), _resident((1, D_MODEL)), _resident(wup_bf.shape),
                _resident(wdw.shape), _resident((1, D_FF)), _resident(wdown_bf.shape), _resident((1, D_MODEL)),
                pl.BlockSpec((bm, PLE_DIM), row), _resident(wple_bf.shape), _resident(wgate_bf.shape)]
    args = [x, o_bf, wmix_bf, gmix, gpre, wup_bf, wdw, bdw, wdown_bf, gpost, p, wple_bf, wgate_bf]
    scratch = [pltpu.VMEM((bm, D_FF), BF16)]
    if seq_mode:
        bps = seq_len // bm
        out_specs = [pl.BlockSpec((bm, D_MODEL), row), pl.BlockSpec((1, SUBLANES, D_FF), lambda i: (i // bps, 0, 0))]
        out_shape = [jax.ShapeDtypeStruct((rows, D_MODEL), F32),
                     jax.ShapeDtypeStruct((rows // seq_len, SUBLANES, D_FF), F32)]
        scratch.append(pltpu.VMEM((SUBLANES, D_FF), F32))
    else:
        bps = 1
        in_specs += [pl.BlockSpec((bm, D_FF), row), pl.BlockSpec((bm, D_FF), row)]
        args += list(prev)
        out_specs = [pl.BlockSpec((bm, D_MODEL), row), pl.BlockSpec((bm, D_FF), row)]
        out_shape = [jax.ShapeDtypeStruct((rows, D_MODEL), F32), jax.ShapeDtypeStruct((rows, D_FF), F32)]
    return pl.pallas_call(
        functools.partial(_ffn_kernel, seq_mode=seq_mode, blocks_per_seq=bps),
        grid=(rows // bm,),
        in_specs=in_specs, out_specs=out_specs, out_shape=out_shape, scratch_shapes=scratch,
        compiler_params=_cparams(("arbitrary",)),
        name="conv_ffn",
    )(*args)


def _kvq_kernel(x_ref, gkv_ref, wkv_ref, wf_ref, bf_ref, gq_ref, wq_ref,
                k_ref, v_ref, lf_ref, kb_ref, vb_ref, qb_ref, kn_ref, qn_ref, *, per_head_out):
    x = x_ref[...]
    xn = _rms(x, gkv_ref[...]).astype(BF16)
    k = _dot(xn, wkv_ref[:, :FOX_HD])
    v = _dot(xn, wkv_ref[:, FOX_HD:])
    if per_head_out:
        k_ref[...] = pltpu.einshape("m(hd)->mhd", k, h=FOX_HEADS)
        v_ref[...] = pltpu.einshape("m(hd)->mhd", v, h=FOX_HEADS)
    else:
        k_ref[...] = k
        v_ref[...] = v
    kb = k.astype(BF16)
    kb_ref[...] = kb
    vb_ref[...] = v.astype(BF16)
    f = _dot(xn, wf_ref[...]) + bf_ref[...]
    lf_ref[...] = _log_sigmoid(f)[:, :FOX_HEADS]
    xq = _rms(x, gq_ref[...]).astype(BF16)
    qb = (_dot(xq, wq_ref[...]) * FOX_DH ** -0.5).astype(BF16)
    qb_ref[...] = qb
    head_sum = (lax.broadcasted_iota(jnp.int32, (FOX_HD, LANES), 0) // FOX_DH ==
                lax.broadcasted_iota(jnp.int32, (FOX_HD, LANES), 1))
    head_sum = jnp.where(head_sum, 1.0, 0.0).astype(BF16)
    for src, dst in ((kb, kn_ref), (qb, qn_ref)):
        sf = src.astype(F32)
        n2 = jnp.max(_dot((sf * sf).astype(BF16), head_sum), axis=0, keepdims=True)
        dst[0] = jnp.broadcast_to(n2, (SUBLANES, LANES))


def _kvq(x, gkv, wkv_bf, wf_bf, bf_pad, gq, wq_bf, bm, per_head_out):
    rows = x.shape[0]
    row = lambda i: (i, 0)
    wide = pl.BlockSpec((bm, FOX_HD), row)
    norm = pl.BlockSpec((1, SUBLANES, LANES), lambda i: (i, 0, 0))
    norm_shape = jax.ShapeDtypeStruct((rows // bm, SUBLANES, LANES), F32)
    if per_head_out:
        kv_spec = pl.BlockSpec((bm, FOX_HEADS, FOX_DH), lambda i: (i, 0, 0))
        kv_shape = jax.ShapeDtypeStruct((rows, FOX_HEADS, FOX_DH), F32)
    else:
        kv_spec, kv_shape = wide, jax.ShapeDtypeStruct((rows, FOX_HD), F32)
    return pl.pallas_call(
        functools.partial(_kvq_kernel, per_head_out=per_head_out),
        grid=(rows // bm,),
        in_specs=[pl.BlockSpec((bm, D_MODEL), row), _resident((1, D_MODEL)), _resident(wkv_bf.shape),
                  _resident(wf_bf.shape), _resident((1, LANES)), _resident((1, D_MODEL)), _resident(wq_bf.shape)],
        out_specs=[kv_spec, kv_spec, pl.BlockSpec((bm, FOX_HEADS), row), wide, wide, wide, norm, norm],
        out_shape=[kv_shape, kv_shape,
                   jax.ShapeDtypeStruct((rows, FOX_HEADS), F32), jax.ShapeDtypeStruct((rows, FOX_HD), BF16),
                   jax.ShapeDtypeStruct((rows, FOX_HD), BF16), jax.ShapeDtypeStruct((rows, FOX_HD), BF16),
                   norm_shape, norm_shape],
        compiler_params=_cparams(("arbitrary",)),
        name="kv_q_proj",
    )(x, gkv, wkv_bf, wf_bf, bf_pad, gq, wq_bf)


def _cumsum_kernel(lf_ref, cq_ref, ck_ref, first_ref, last_ref, carry_sc):
    bl = lf_ref.shape[0]

    @pl.when(pl.program_id(1) == 0)
    def _():
        carry_sc[...] = jnp.zeros_like(carry_sc)

    lf = lf_ref[...]
    tri = (lax.broadcasted_iota(jnp.int32, (bl, bl), 1) <= lax.broadcasted_iota(jnp.int32, (bl, bl), 0))
    tri = jnp.where(tri, 1.0, 0.0).astype(BF16)
    hi, mid, lo = _split3(lf)
    c = (_dot(tri, lo) + _dot(tri, mid)) + _dot(tri, hi) + carry_sc[...]
    carry_sc[...] = c[bl - 1:bl, :]
    first_ref[0] = c[0:1, :]
    last_ref[0] = c[bl - 1:bl, :]

    lane_of_head = lax.broadcasted_iota(jnp.int32, (FOX_HEADS, LANES), 1) - 3 * lax.broadcasted_iota(
        jnp.int32, (FOX_HEADS, LANES), 0)
    lane = lax.broadcasted_iota(jnp.int32, (1, LANES), 1)
    cq = jnp.where(lane < BIAS_ONES, 1.0, 0.0)
    ck = jnp.where(jnp.logical_and(lane >= BIAS_ONES, lane < 2 * BIAS_ONES), 1.0, 0.0)
    for piece, part in enumerate(_split3(c)):
        cq = cq + _dot(part, jnp.where(lane_of_head == BIAS_ONES + piece, 1.0, 0.0).astype(BF16))
        ck = ck + _dot(part, jnp.where(lane_of_head == piece, -1.0, 0.0).astype(BF16))
    cq_ref[...] = cq.astype(BF16)
    ck_ref[...] = ck.astype(BF16)


def _cumsum_time(lf, batch, seq):
    bl = min(ATT_BLOCK, seq)
    nb = seq // bl
    blk = lambda b, i: (b * nb + i, 0)
    wide = pl.BlockSpec((bl, LANES), blk)
    edge = pl.BlockSpec((1, 1, FOX_HEADS), lambda b, i: (b * nb + i, 0, 0))
    return pl.pallas_call(
        _cumsum_kernel,
        grid=(batch, nb),
        in_specs=[pl.BlockSpec((bl, FOX_HEADS), blk)], out_specs=[wide, wide, edge, edge],
        out_shape=[jax.ShapeDtypeStruct((lf.shape[0], LANES), BF16)] * 2
        + [jax.ShapeDtypeStruct((batch * nb, 1, FOX_HEADS), F32)] * 2,
        scratch_shapes=[pltpu.VMEM((1, FOX_HEADS), F32)],
        compiler_params=_cparams(("arbitrary", "arbitrary")),
        name="logf_cumsum",
    )(lf)


def _fox_prompt_kernel(cfirst_ref, clast_ref, qkb_ref, q_ref, k_ref, v_ref, cq_ref, ck_ref, o_ref, m_sc, acc_sc):
    b = pl.program_id(0)
    hg = pl.program_id(1)
    i = pl.program_id(2)

    def dead(j):
        worst = None
        for hh in range(ATT_HEADS):
            h = ATT_HEADS * hg + hh
            gap = cfirst_ref[b, i, h] - clast_ref[b, j, h] + 2.0 * qkb_ref[b, h]
            worst = gap if worst is None else jnp.maximum(worst, gap)
        return worst < -EXP_UNDERFLOW

    j_lo = lax.while_loop(lambda j: jnp.logical_and(j < i, dead(j)), lambda j: j + 1, jnp.int32(0))
    bq = q_ref.shape[0]
    pair = 2 * FOX_DH
    pair_lanes = lambda ref, rows, p: ref[rows, p * pair:(p + 1) * pair]
    low = lax.broadcasted_iota(jnp.int32, (1, pair), 1) < FOX_DH
    causal = (lax.broadcasted_iota(jnp.int32, (bq, bq), 1) <= lax.broadcasted_iota(jnp.int32, (bq, bq), 0))
    ones = jnp.ones((bq, pair), BF16)
    lanes = lambda x, n: jnp.concatenate([x] * n, axis=1)
    bias_lane = lax.broadcasted_iota(jnp.int32, (1, LANES), 1)
    cq = cq_ref[...]
    q_aug = []
    for hh in range(ATT_HEADS):
        q = pair_lanes(q_ref, slice(None), hh // 2)
        qm = jnp.where(low if hh % 2 == 0 else jnp.logical_not(low), q, jnp.zeros_like(q))
        first = 3 * (ATT_HEADS * hg + hh)
        own = jnp.logical_or(jnp.logical_and(bias_lane >= first, bias_lane < first + 3),
                             jnp.logical_and(bias_lane >= first + BIAS_ONES, bias_lane < first + BIAS_ONES + 3))
        q_aug.append(jnp.concatenate([qm, jnp.where(own, cq, jnp.zeros_like(cq))], axis=1))
        m_sc[hh] = jnp.full(m_sc.shape[1:], -jnp.inf, F32)
        acc_sc[hh] = jnp.zeros(acc_sc.shape[1:], F32)

    def block(j, masked):
        rows = pl.ds(pl.multiple_of(j * bq, bq), bq)
        ck = ck_ref[rows, :]
        k_aug = [jnp.concatenate([pair_lanes(k_ref, rows, p), ck], axis=1) for p in range(ATT_HEADS // 2)]
        v_aug = [jnp.concatenate([pair_lanes(v_ref, rows, p), ones], axis=1) for p in range(ATT_HEADS // 2)]
        for hh in range(ATT_HEADS):
            s = _dot_nt(q_aug[hh], k_aug[hh // 2])
            if masked:
                s = jnp.where(causal, s, -jnp.inf)
            m_prev = m_sc[hh]
            m_next = jnp.maximum(m_prev, jnp.max(s, axis=1, keepdims=True))
            alpha = jnp.exp(m_prev - m_next)
            p = jnp.exp(s - lanes(m_next, bq // LANES))
            acc_sc[hh] = lanes(alpha, 2) * acc_sc[hh] + _dot(p.astype(BF16), v_aug[hh // 2])
            m_sc[hh] = m_next

    n_live = i - j_lo + 1

    @pl.when(n_live == 1)
    def _():
        block(i, True)

    @pl.when(n_live == 2)
    def _():
        block(i, True)
        block(i - 1, False)

    @pl.when(n_live >= 3)
    def _():
        block(i, True)
        block(i - 1, False)
        block(i - 2, False)

    def body(t, carry):
        block(i - 3 - 2 * t, False)
        block(i - 4 - 2 * t, False)
        return carry

    rest = jnp.maximum(n_live - 3, 0)
    lax.fori_loop(0, rest // 2, body, 0)

    @pl.when(rest % 2 == 1)
    def _():
        block(j_lo, False)

    for p in range(ATT_HEADS // 2):
        o0 = acc_sc[2 * p, :, :pair] / acc_sc[2 * p, :, pair:]
        o1 = acc_sc[2 * p + 1, :, :pair] / acc_sc[2 * p + 1, :, pair:]
        o_ref[:, p * pair:(p + 1) * pair] = jnp.where(low, o0, o1).astype(o_ref.dtype)


def _fox_prompt(q_bf, k_bf, v_bf, cq, ck, c_first, c_last, qk_bound, batch, seq):
    bq = min(ATT_BLOCK, seq)
    nq = seq // bq
    pair = 2 * FOX_DH
    width = ATT_HEADS * FOX_DH
    qblk = pl.BlockSpec((bq, width), lambda b, hg, i, *_: (b * nq + i, hg))
    kvblk = pl.BlockSpec((seq, width), lambda b, hg, i, *_: (b, hg))
    return pl.pallas_call(
        _fox_prompt_kernel,
        grid_spec=pltpu.PrefetchScalarGridSpec(
            num_scalar_prefetch=3,
            grid=(batch, FOX_HEADS // ATT_HEADS, nq),
            in_specs=[qblk, kvblk, kvblk,
                      pl.BlockSpec((bq, LANES), lambda b, hg, i, *_: (b * nq + i, 0)),
                      pl.BlockSpec((seq, LANES), lambda b, hg, i, *_: (b, 0))],
            out_specs=qblk,
            scratch_shapes=[pltpu.VMEM((ATT_HEADS, bq, LANES), F32), pltpu.VMEM((ATT_HEADS, bq, 2 * pair), F32)]),
        out_shape=jax.ShapeDtypeStruct((batch * seq, FOX_HD), BF16),
        compiler_params=_cparams(("arbitrary", "arbitrary", "arbitrary")),
        name="fox_attention_prompt",
    )(c_first, c_last, qk_bound, q_bf, k_bf, v_bf, cq, ck)


def _fox_scores_kernel(pt_ref, q_ref, knew_ref, lfnew_ref, *refs, pp):
    k_refs, lf_refs = refs[:pp], refs[pp:2 * pp]
    s_ref, pmax_ref, m_ref, sself_ref, q_sc, carry_sc = refs[2 * pp:]
    step = pl.program_id(1)
    page = k_refs[0].shape[3]

    @pl.when(step == 0)
    def _():
        for hp in range(FOX_HEADS // 2):
            qc, kc = [_as_column(r[0, :, hp * LANES:(hp + 1) * LANES]) for r in (q_ref, knew_ref)]
            for hl in range(2):
                h = 2 * hp + hl
                qh = qc[hl * FOX_DH:(hl + 1) * FOX_DH]
                q_sc[h] = qh
                sself_ref[0, h:h + 1, :] = jnp.sum(qh * kc[hl * FOX_DH:(hl + 1) * FOX_DH], axis=0, keepdims=True)
        m_ref[0] = sself_ref[0]
        carry_sc[...] = _as_column(lfnew_ref[0])[:FOX_HEADS]

    later = (lax.broadcasted_iota(jnp.int32, (page, page), 0) > lax.broadcasted_iota(jnp.int32, (page, page), 1))
    later = jnp.where(later, 1.0, 0.0).astype(BF16)
    carry = carry_sc[...]
    biases = []
    for i in range(pp):
        lf = lf_refs[i][0]
        hi, mid, lo = _split3(lf)
        biases.append((_dot(lo, later) + _dot(mid, later)) + _dot(hi, later) + carry)
        carry = carry + jnp.sum(lf, axis=1, keepdims=True)
    carry_sc[...] = carry
    for h in range(FOX_HEADS):
        qh = q_sc[h]
        for i in range(pp):
            s_ref[0, h:h + 1, i * page:(i + 1) * page] = (
                jnp.sum(k_refs[i][0, h] * qh, axis=0, keepdims=True) + biases[i][h:h + 1, :])
    m = m_ref[0]
    for i in range(pp):
        page_max = jnp.max(s_ref[0, :, i * page:(i + 1) * page], axis=1, keepdims=True)
        pmax_ref[0, i] = jnp.broadcast_to(page_max, (FOX_HEADS, LANES))
        m = jnp.maximum(m, page_max)
    m_ref[0] = m


def _fox_values_kernel(page_ref, walk_ref, count_ref, s_ref, m_ref, sself_ref, vnew_ref, *refs, pp):
    v_refs = refs[:pp]
    o_ref, l_sc, acc_sc = refs[pp:]
    b = pl.program_id(0)
    step = pl.program_id(1)
    page = v_refs[0].shape[3]
    m = m_ref[0]

    @pl.when(step == 0)
    def _():
        p_self = jnp.exp(sself_ref[0] - m)
        l_sc[...] = p_self
        lane = lax.broadcasted_iota(jnp.int32, (FOX_DH, page), 1)
        for hp in range(FOX_HEADS // 2):
            vc = _as_column(vnew_ref[0, :, hp * LANES:(hp + 1) * LANES])
            for hl in range(2):
                h = 2 * hp + hl
                acc_sc[h] = jnp.where(lane == 0, vc[hl * FOX_DH:(hl + 1) * FOX_DH] * p_self[h:h + 1, :], 0.0)

    n_live = count_ref[b]
    for i in range(pp):
        slot = step * pp + i

        @pl.when(slot < n_live)
        def _():
            start = pl.multiple_of(walk_ref[b, slot] * page, page)
            p = jnp.exp(s_ref[0, :, pl.ds(start, page)] - m)
            l_sc[...] = l_sc[...] + jnp.sum(p, axis=1, keepdims=True)
            for h in range(FOX_HEADS):
                acc_sc[h] = acc_sc[h] + v_refs[i][0, h] * p[h:h + 1, :]

    @pl.when(step == pl.num_programs(1) - 1)
    def _():
        o_ref[0] = (jnp.sum(acc_sc[...], axis=2) / l_sc[:, :1]).astype(o_ref.dtype)


def _fox_decode(q, k_new, v_new, lf_new, cache_k, cache_v, cache_logf, page_table):
    nb, n_pages = page_table.shape
    page = cache_k.shape[1]
    assert page == LANES, "one cache page fills the lane axis"
    pp = min(PAGES_PER_STEP, n_pages)
    n_steps = n_pages // pp
    ck = jnp.transpose(cache_k, (0, 2, 3, 1))
    cv = jnp.transpose(cache_v, (0, 2, 3, 1))
    clf = jnp.transpose(cache_logf, (0, 2, 1))
    as_row = lambda a: a.reshape(nb, 1, FOX_HD)
    lf_row = jnp.pad(lf_new, ((0, 0), (0, LANES - FOX_HEADS))).reshape(nb, 1, LANES)
    heads_lanes = jax.ShapeDtypeStruct((nb, FOX_HEADS, LANES), F32)

    def k_page(i):
        return pl.BlockSpec((1, FOX_HEADS, FOX_DH, page),
                            lambda b, s, pt: (pt[b, n_pages - 1 - (s * pp + i)], 0, 0, 0))

    def lf_page(i):
        return pl.BlockSpec((1, FOX_HEADS, page), lambda b, s, pt: (pt[b, n_pages - 1 - (s * pp + i)], 0, 0))

    row = pl.BlockSpec((1, 1, FOX_HD), lambda b, s, *_: (b, 0, 0))
    per_seq = pl.BlockSpec((1, FOX_HEADS, LANES), lambda b, s, *_: (b, 0, 0))
    scores, page_max, m, s_self = pl.pallas_call(
        functools.partial(_fox_scores_kernel, pp=pp),
        grid_spec=pltpu.PrefetchScalarGridSpec(
            num_scalar_prefetch=1, grid=(nb, n_steps),
            in_specs=[row, row, pl.BlockSpec((1, 1, LANES), lambda b, s, pt: (b, 0, 0))]
            + [k_page(i) for i in range(pp)] + [lf_page(i) for i in range(pp)],
            out_specs=[pl.BlockSpec((1, FOX_HEADS, pp * page), lambda b, s, pt: (b, 0, s)),
                       pl.BlockSpec((1, pp, FOX_HEADS, LANES), lambda b, s, pt: (b, s, 0, 0)), per_seq, per_seq],
            scratch_shapes=[pltpu.VMEM((FOX_HEADS, FOX_DH, page), F32), pltpu.VMEM((FOX_HEADS, page), F32)]),
        out_shape=[jax.ShapeDtypeStruct((nb, FOX_HEADS, n_pages * page), F32),
                   jax.ShapeDtypeStruct((nb, n_pages, FOX_HEADS, LANES), F32), heads_lanes, heads_lanes],
        compiler_params=_cparams(("arbitrary", "arbitrary")),
        name="fox_decode_scores",
    )(page_table, as_row(q), as_row(k_new), lf_row, *([ck] * pp), *([clf] * pp))

    live = jnp.any(page_max[:, :, :, 0] - m[:, None, :, 0] >= -EXP_UNDERFLOW, axis=2)
    count = jnp.sum(live, axis=1).astype(jnp.int32)
    walk = jnp.argsort(jnp.logical_not(live), axis=1, stable=True).astype(jnp.int32)
    pages = jnp.take_along_axis(page_table[:, ::-1], walk, axis=1)
    pp = min(VALUE_PAGES_PER_STEP, n_pages)
    n_steps = n_pages // pp
    is_live = (jnp.arange(n_pages, dtype=jnp.int32)[None, :] < count[:, None]).reshape(nb * n_steps, pp)
    flat = pages.reshape(nb * n_steps, pp)
    last_live = lax.cummax(jnp.where(is_live, jnp.arange(nb * n_steps, dtype=jnp.int32)[:, None], 0), axis=0)
    pages = jnp.take_along_axis(flat, last_live, axis=0).reshape(nb, n_pages)

    def v_page(i):
        return pl.BlockSpec((1, FOX_HEADS, FOX_DH, page), lambda b, s, pg, wk, ct: (pg[b, s * pp + i], 0, 0, 0))

    out = pl.pallas_call(
        functools.partial(_fox_values_kernel, pp=pp),
        grid_spec=pltpu.PrefetchScalarGridSpec(
            num_scalar_prefetch=3, grid=(nb, n_steps),
            in_specs=[pl.BlockSpec((1, FOX_HEADS, n_pages * page), lambda b, s, *_: (b, 0, 0)), per_seq, per_seq, row]
            + [v_page(i) for i in range(pp)],
            out_specs=pl.BlockSpec((1, FOX_HEADS, FOX_DH), lambda b, s, *_: (b, 0, 0)),
            scratch_shapes=[pltpu.VMEM((FOX_HEADS, LANES), F32), pltpu.VMEM((FOX_HEADS, FOX_DH, page), F32)]),
        out_shape=jax.ShapeDtypeStruct((nb, FOX_HEADS, FOX_DH), BF16),
        compiler_params=_cparams(("arbitrary", "arbitrary")),
        name="fox_decode_values",
    )(pages, walk, count, scores, m, s_self, as_row(v_new), *([cv] * pp))
    return out.reshape(nb, FOX_HD)


def _run_group(x, p, cos, sin, w, bm, seq_len, sample):
    rows = x.shape[0]
    qkv_dtype = BF16 if sample is None else F32
    q, k, v, gate = _ret_in(x, w["g_pre_mix"][0], w["w_in"], cos, sin, bm, qkv_dtype)
    if sample is None:
        batch = rows // seq_len
        o, ret_state = _retention_prompt(q, k, v, gate, w["log_gamma"], batch, seq_len)
    else:
        o, ret_state = _retention_step(sample["state_ret"], q, k, v, gate, w["log_gamma"])

    conv_out = []
    ffn_args = lambda i: (w["g_post_mix"][i], w["g_pre_ffn"][i], w["w_up"][i], w["w_dw"][i], w["b_dw"][i],
                          w["w_down"][i], w["g_post_ffn"][i], p[i], w["w_ple"][i], w["w_gate"][i], bm)
    if sample is None:
        x, tail = _ffn(x, o, w["w_out_ret"], *ffn_args(0), seq_len=seq_len)
        conv_out.append(tail[:, SUBLANES - 2:, :])
    else:
        buf = sample["state_conv"][0]
        x, a = _ffn(x, o, w["w_out_ret"], *ffn_args(0), prev=(buf[:, 0], buf[:, 1]))
        conv_out.append(jnp.stack([buf[:, 1], a], axis=1))

    k_new, v_new, lf_new, k_bf, v_bf, q_bf, k_n2, q_n2 = _kvq(
        x, w["g_kv"], w["w_kv"], w["w_f"], w["b_f"], w["g_pre_mix"][1], w["w_q"], bm, per_head_out=sample is None)
    if sample is None:
        cq, ck, c_first, c_last = _cumsum_time(lf_new, batch, seq_len)
        per_seq_max = lambda n2: jnp.max(n2.reshape(batch, -1, SUBLANES, LANES)[:, :, 0, :FOX_HEADS], axis=1)
        qk_bound = jnp.sqrt(per_seq_max(k_n2) * per_seq_max(q_n2)) * NORM_PAD
        edges = lambda e: e.reshape(batch, -1, FOX_HEADS)
        att = _fox_prompt(q_bf, k_bf, v_bf, cq, ck, edges(c_first), edges(c_last), qk_bound, batch, seq_len)
    else:
        att = _fox_decode(q_bf.astype(F32), k_new, v_new, lf_new, sample["cache_k"], sample["cache_v"],
                          sample["cache_logf"], sample["page_table"])

    if sample is None:
        x, tail = _ffn(x, att, w["w_out_fox"], *ffn_args(1), seq_len=seq_len)
        conv_out.append(tail[:, SUBLANES - 2:, :])
    else:
        buf = sample["state_conv"][1]
        x, a = _ffn(x, att, w["w_out_fox"], *ffn_args(1), prev=(buf[:, 0], buf[:, 1]))
        conv_out.append(jnp.stack([buf[:, 1], a], axis=1))
    return x, ret_state, jnp.stack(conv_out), k_new, v_new, lf_new


def kernel(x_prompt, x_sample, state_ret, state_conv, cache_k, cache_v, cache_logf, page_table, p_prompt, p_sample,
           norm_pre_mix, norm_post_mix, norm_pre_ffn, norm_post_ffn, w_in_ret, w_out_ret, norm_kv, w_kvf, b_f,
           w_q_fox, w_out_fox, w_up, w_dw, b_dw, w_down, w_ple, w_ple_gate):
    bp, tp, _ = x_prompt.shape
    nb, ts, _ = x_sample.shape
    assert ts == 1, "the sample group is one new token per sequence"
    past_len = page_table.shape[1] * cache_k.shape[1]
    depth = w_up.shape[0]
    row_vec = lambda a: a.reshape(a.shape[0], 1, a.shape[1])
    w = {
        "g_pre_mix": row_vec(norm_pre_mix), "g_post_mix": row_vec(norm_post_mix),
        "g_pre_ffn": row_vec(norm_pre_ffn), "g_post_ffn": row_vec(norm_post_ffn),
        "g_kv": norm_kv.reshape(1, D_MODEL),
        "w_in": w_in_ret[0].astype(BF16), "w_out_ret": w_out_ret[0].astype(BF16),
        "w_kv": w_kvf[:, :2 * FOX_HD].astype(BF16),
        "w_f": jnp.pad(w_kvf[:, 2 * FOX_HD:], ((0, 0), (0, LANES - FOX_HEADS))).astype(BF16),
        "b_f": jnp.pad(b_f, (0, LANES - FOX_HEADS)).reshape(1, LANES),
        "w_q": w_q_fox[0].astype(BF16), "w_out_fox": w_out_fox[0].astype(BF16),
        "w_up": w_up.astype(BF16), "w_dw": w_dw, "b_dw": row_vec(b_dw), "w_down": w_down.astype(BF16),
        "w_ple": w_ple.astype(BF16), "w_gate": w_ple_gate.astype(BF16),
        "log_gamma": jnp.log1p(-jnp.exp2(-5.0 - jnp.arange(RET_HEADS, dtype=F32))),
    }
    bm = min(ROW_BLOCK, tp)
    cos_p, sin_p = _rope_table(tp, 0)
    y_p, ret_p, conv_p, k_p, v_p, lf_p = _run_group(
        x_prompt.reshape(bp * tp, D_MODEL), p_prompt.reshape(depth, bp * tp, PLE_DIM), cos_p, sin_p, w, bm, tp, None)
    cos_s, sin_s = _rope_table(SUBLANES, past_len)
    cos_s = jnp.broadcast_to(cos_s[:1], (nb, ROPE_HALF))
    sin_s = jnp.broadcast_to(sin_s[:1], (nb, ROPE_HALF))
    sample = {"state_ret": state_ret[0], "state_conv": state_conv, "cache_k": cache_k, "cache_v": cache_v,
              "cache_logf": cache_logf, "page_table": page_table}
    y_s, ret_s, conv_s, k_s, v_s, lf_s = _run_group(
        x_sample.reshape(nb, D_MODEL), p_sample.reshape(depth, nb, PLE_DIM), cos_s, sin_s, w, nb, 1, sample)
    return (y_p.reshape(bp, tp, D_MODEL), y_s.reshape(nb, 1, D_MODEL),
            ret_p[None], ret_s[None], conv_p, conv_s,
            k_p.reshape(bp, tp, FOX_HEADS, FOX_DH), k_s.reshape(nb, 1, FOX_HEADS, FOX_DH),
            v_p.reshape(bp, tp, FOX_HEADS, FOX_DH), v_s.reshape(nb, 1, FOX_HEADS, FOX_DH),
            lf_p.reshape(bp, tp, FOX_HEADS), lf_s.reshape(nb, 1, FOX_HEADS))
```

```python
import functools
import math

import jax
import jax.numpy as jnp
from jax import lax
from jax.experimental import pallas as pl
from jax.experimental.pallas import tpu as pltpu

F32 = jnp.float32
BF16 = jnp.bfloat16

D_MODEL = 1024
RET_HEADS = 4
RET_DK = 256
RET_DV = 512
RET_QK = RET_HEADS * RET_DK
RET_VD = RET_HEADS * RET_DV
ROPE_BASE = 10000.0
ROPE_HALF = RET_DK // 2
FOX_HEADS = 16
FOX_DH = 64
FOX_HD = FOX_HEADS * FOX_DH
D_FF = 2816
PLE_DIM = 256
NORM_EPS = 1e-6

LANES = 128
SUBLANES = 8
VMEM_LIMIT = 56 * 1024 * 1024

ROW_BLOCK = 512
RET_CHUNK = 256
FFN_CHUNK = 256
ATT_BLOCK = 512
ATT_HEADS = 4
PAGES_PER_STEP = 16
VALUE_PAGES_PER_STEP = 32
BIAS_ONES = 48
EXP_UNDERFLOW = 110.0
NORM_PAD = 1.05


def _cparams(sem):
    return pltpu.CompilerParams(dimension_semantics=sem, vmem_limit_bytes=VMEM_LIMIT)


def _resident(shape):
    return pl.BlockSpec(shape, lambda *_: (0,) * len(shape), pipeline_mode=pl.Buffered(1))


def _dot(a, b):
    return jnp.dot(a, b, preferred_element_type=F32)


def _dot_nt(a, b):
    return lax.dot_general(a, b, (((1,), (1,)), ((), ())), preferred_element_type=F32)


def _dot_tn(a, b):
    return lax.dot_general(a, b, (((0,), (0,)), ((), ())), preferred_element_type=F32)


def _rms(x, g):
    return x * lax.rsqrt(jnp.mean(x * x, axis=-1, keepdims=True) + NORM_EPS) * g


def _sigmoid(x):
    return 1.0 / (1.0 + jnp.exp(-x))


def _gelu_tanh(x):
    return x * (0.5 * (1.0 + jnp.tanh(math.sqrt(2.0 / math.pi) * (x + 0.044715 * (x * x * x)))))


def _log_sigmoid(x):
    z = -x
    return -(jnp.maximum(z, 0.0) + jnp.log1p(jnp.exp(-jnp.abs(z))))


def _split3(x):
    hi = x.astype(BF16)
    r1 = x - hi.astype(F32)
    mid = r1.astype(BF16)
    lo = (r1 - mid.astype(F32)).astype(BF16)
    return hi, mid, lo


def _rope_kernel(cos_ref, sin_ref, *, base_pos):
    n = cos_ref.shape[0]
    i = pl.program_id(0)
    lane = lax.broadcasted_iota(jnp.int32, (1, ROPE_HALF), 1).astype(F32)
    inv = jnp.power(jnp.full((1, ROPE_HALF), ROPE_BASE, F32), -(lane / ROPE_HALF))
    pos = (lax.broadcasted_iota(jnp.int32, (n, ROPE_HALF), 0) + (i * n + base_pos)).astype(F32)
    ang = pos * inv
    cos_ref[...] = jnp.cos(ang)
    sin_ref[...] = jnp.sin(ang)


def _rope_table(n_rows, base_pos):
    bn = min(n_rows, 1024)
    spec = pl.BlockSpec((bn, ROPE_HALF), lambda i: (i, 0))
    return pl.pallas_call(
        functools.partial(_rope_kernel, base_pos=base_pos),
        grid=(n_rows // bn,),
        out_specs=[spec, spec],
        out_shape=[jax.ShapeDtypeStruct((n_rows, ROPE_HALF), F32)] * 2,
        compiler_params=_cparams(("arbitrary",)),
        name="rope_table",
    )()


def _ret_in_kernel(x_ref, g_ref, w_ref, cos_ref, sin_ref, q_ref, k_ref, v_ref, gate_ref):
    xn = _rms(x_ref[...], g_ref[...]).astype(BF16)
    cos = cos_ref[...]
    sin = sin_ref[...]
    for out_ref, base, scale in ((q_ref, 0, 1.0), (k_ref, RET_QK, RET_DK ** -0.5)):
        for h in range(RET_HEADS):
            lo = h * RET_DK
            u = _dot(xn, w_ref[:, base + lo:base + lo + RET_DK])
            x1, x2 = u[:, :ROPE_HALF], u[:, ROPE_HALF:]
            out_ref[:, lo:lo + ROPE_HALF] = ((x1 * cos - x2 * sin) * scale).astype(out_ref.dtype)
            out_ref[:, lo + ROPE_HALF:lo + RET_DK] = ((x1 * sin + x2 * cos) * scale).astype(out_ref.dtype)
    for h in range(RET_HEADS):
        lo = h * RET_DV
        v_ref[:, lo:lo + RET_DV] = _dot(xn, w_ref[:, 2 * RET_QK + lo:2 * RET_QK + lo + RET_DV]).astype(v_ref.dtype)
        gate_ref[:, lo:lo + RET_DV] = _dot(xn, w_ref[:, 2 * RET_QK + RET_VD + lo:2 * RET_QK + RET_VD + lo + RET_DV])


def _ret_in(x, g, w_bf, cos, sin, bm, qkv_dtype):
    rows = x.shape[0]
    n_tab = cos.shape[0] // bm
    row = lambda i: (i, 0)
    tab = pl.BlockSpec((bm, ROPE_HALF), lambda i: (i % n_tab, 0))
    return pl.pallas_call(
        _ret_in_kernel,
        grid=(rows // bm,),
        in_specs=[pl.BlockSpec((bm, D_MODEL), row), _resident((1, D_MODEL)), _resident(w_bf.shape), tab, tab],
        out_specs=[pl.BlockSpec((bm, RET_QK), row), pl.BlockSpec((bm, RET_QK), row),
                   pl.BlockSpec((bm, RET_VD), row), pl.BlockSpec((bm, RET_VD), row)],
        out_shape=[jax.ShapeDtypeStruct((rows, RET_QK), qkv_dtype), jax.ShapeDtypeStruct((rows, RET_QK), qkv_dtype),
                   jax.ShapeDtypeStruct((rows, RET_VD), qkv_dtype), jax.ShapeDtypeStruct((rows, RET_VD), F32)],
        compiler_params=_cparams(("arbitrary",)),
        name="ret_in",
    )(x, g, w_bf, cos, sin)


def _ret_chunk_kernel(lg_ref, q_ref, k_ref, v_ref, gate_ref, o_ref, s_ref, decay_sc):
    c = pl.program_id(1)
    C = q_ref.shape[0]

    @pl.when(jnp.logical_and(pl.program_id(0) == 0, c == 0))
    def _():
        rel = lax.broadcasted_iota(jnp.int32, (C, C), 0) - lax.broadcasted_iota(jnp.int32, (C, C), 1)
        dist = jnp.maximum(rel, 0).astype(F32)
        for h in range(RET_HEADS):
            decay_sc[h] = jnp.where(rel >= 0, jnp.exp(dist * lg_ref[h]), 0.0)

    @pl.when(c == 0)
    def _():
        s_ref[...] = jnp.zeros_like(s_ref)

    n = lax.broadcasted_iota(jnp.int32, (C, 1), 0).astype(F32)
    for h in range(RET_HEADS):
        lg = lg_ref[h]
        q = q_ref[:, h * RET_DK:(h + 1) * RET_DK]
        k = k_ref[:, h * RET_DK:(h + 1) * RET_DK]
        v = v_ref[:, h * RET_DV:(h + 1) * RET_DV]
        scores = _dot_nt(q, k) * decay_sc[h]
        inner = _dot(scores.astype(BF16), v)
        state = s_ref[0, h]
        cross = _dot(q, state.astype(BF16)) * jnp.exp((n + 1.0) * lg)
        kd = (k.astype(F32) * jnp.exp((C - 1.0 - n) * lg)).astype(BF16)
        chunk_decay = jnp.exp(jnp.full((1, RET_DV), float(C), F32) * lg)
        s_ref[0, h] = state * chunk_decay + _dot_tn(kd, v)
        o = inner + cross
        o = o * lax.rsqrt(jnp.mean(o * o, axis=-1, keepdims=True) + NORM_EPS)
        gate = gate_ref[:, h * RET_DV:(h + 1) * RET_DV]
        o_ref[:, h * RET_DV:(h + 1) * RET_DV] = (o * (gate * _sigmoid(gate))).astype(o_ref.dtype)


def _retention_prompt(q, k, v, gate, log_gamma, batch, seq):
    C = min(RET_CHUNK, seq)
    nc = seq // C
    blk = lambda b, c, lg: (b * nc + c, 0)
    return pl.pallas_call(
        _ret_chunk_kernel,
        grid_spec=pltpu.PrefetchScalarGridSpec(
            num_scalar_prefetch=1,
            grid=(batch, nc),
            in_specs=[pl.BlockSpec((C, RET_QK), blk), pl.BlockSpec((C, RET_QK), blk),
                      pl.BlockSpec((C, RET_VD), blk), pl.BlockSpec((C, RET_VD), blk)],
            out_specs=[pl.BlockSpec((C, RET_VD), blk),
                       pl.BlockSpec((1, RET_HEADS, RET_DK, RET_DV), lambda b, c, lg: (b, 0, 0, 0))],
            scratch_shapes=[pltpu.VMEM((RET_HEADS, C, C), F32)],
        ),
        out_shape=[jax.ShapeDtypeStruct((batch * seq, RET_VD), BF16),
                   jax.ShapeDtypeStruct((batch, RET_HEADS, RET_DK, RET_DV), F32)],
        compiler_params=_cparams(("arbitrary", "arbitrary")),
        name="retention_chunks",
    )(log_gamma, q, k, v, gate)


def _as_column(row):
    return jnp.transpose(jnp.broadcast_to(row, (LANES, LANES)))


def _ret_step_kernel(lg_ref, s_ref, q_ref, k_ref, v_ref, gate_ref, o_ref, so_ref):
    wide = lambda x: jnp.concatenate([x] * (RET_DV // LANES), axis=1)
    for h in range(RET_HEADS):
        gamma = jnp.exp(jnp.full((1, RET_DV), 1.0, F32) * lg_ref[h])
        state = s_ref[0, h]
        column = lambda ref: jnp.concatenate(
            [_as_column(ref[0, :, h * RET_DK + c * LANES:h * RET_DK + (c + 1) * LANES])
             for c in range(RET_DK // LANES)], axis=0)
        qc = column(q_ref)
        kc = column(k_ref)
        vr = v_ref[0, :, h * RET_DV:(h + 1) * RET_DV]
        qk = jnp.sum(qc * kc, axis=0, keepdims=True)
        qs = jnp.sum(wide(qc) * state, axis=0, keepdims=True)
        o = wide(qk) * vr + qs * gamma
        so_ref[0, h] = state * gamma + wide(kc) * vr
        o = o * lax.rsqrt(jnp.mean(o * o, axis=-1, keepdims=True) + NORM_EPS)
        gate = gate_ref[0, :, h * RET_DV:(h + 1) * RET_DV]
        o_ref[0, :, h * RET_DV:(h + 1) * RET_DV] = (o * (gate * _sigmoid(gate))).astype(o_ref.dtype)


def _retention_step(state, q, k, v, gate, log_gamma):
    nb = state.shape[0]
    st = pl.BlockSpec((1, RET_HEADS, RET_DK, RET_DV), lambda b, lg: (b, 0, 0, 0))
    rowk = pl.BlockSpec((1, 1, RET_QK), lambda b, lg: (b, 0, 0))
    rowv = pl.BlockSpec((1, 1, RET_VD), lambda b, lg: (b, 0, 0))
    o, s_new = pl.pallas_call(
        _ret_step_kernel,
        grid_spec=pltpu.PrefetchScalarGridSpec(
            num_scalar_prefetch=1, grid=(nb,),
            in_specs=[st, rowk, rowk, rowv, rowv], out_specs=[rowv, st]),
        out_shape=[jax.ShapeDtypeStruct((nb, 1, RET_VD), BF16), jax.ShapeDtypeStruct(state.shape, F32)],
        compiler_params=_cparams(("arbitrary",)),
        name="retention_step",
    )(log_gamma, state, q.reshape(nb, 1, RET_QK), k.reshape(nb, 1, RET_QK), v.reshape(nb, 1, RET_VD),
      gate.reshape(nb, 1, RET_VD))
    return o.reshape(nb, RET_VD), s_new


def _ffn_kernel(*refs, seq_mode, blocks_per_seq):
    if seq_mode:
        (x_ref, o_ref, wmix_ref, gmix_ref, gpre_ref, wup_ref, wdw_ref, bdw_ref, wdown_ref, gpost_ref, p_ref,
         wple_ref, wgate_ref, y_ref, tail_ref, h_sc, carry_sc) = refs
    else:
        (x_ref, o_ref, wmix_ref, gmix_ref, gpre_ref, wup_ref, wdw_ref, bdw_ref, wdown_ref, gpost_ref, p_ref,
         wple_ref, wgate_ref, prev2_ref, prev1_ref, y_ref, a_ref, h_sc) = refs
    x = x_ref[...] + _rms(_dot(o_ref[...], wmix_ref[...]), gmix_ref[...])
    bm = x.shape[0]
    xn = _rms(x, gpre_ref[...]).astype(BF16)
    if seq_mode:
        @pl.when(pl.program_id(0) % blocks_per_seq == 0)
        def _():
            carry_sc[...] = jnp.zeros_like(carry_sc)
        row = lax.broadcasted_iota(jnp.int32, (bm, FFN_CHUNK), 0)
    for c in range(D_FF // FFN_CHUNK):
        sl = slice(c * FFN_CHUNK, (c + 1) * FFN_CHUNK)
        a = _dot(xn, wup_ref[:, sl])
        b = _dot(xn, wup_ref[:, D_FF + c * FFN_CHUNK:D_FF + (c + 1) * FFN_CHUNK])
        if seq_mode:
            c2 = carry_sc[SUBLANES - 2:SUBLANES - 1, sl]
            c1 = carry_sc[SUBLANES - 1:SUBLANES, sl]
            a1 = jnp.where(row == 0, c1, pltpu.roll(a, 1, 0))
            a2 = jnp.where(row == 0, c2, jnp.where(row == 1, c1, pltpu.roll(a, 2, 0)))
            last = a[bm - SUBLANES:bm, :]
            carry_sc[:, sl] = last
            tail_ref[0, :, sl] = last
        else:
            a1 = prev1_ref[:, sl]
            a2 = prev2_ref[:, sl]
            a_ref[:, sl] = a
        conv = bdw_ref[:, sl] + a2 * wdw_ref[0:1, sl]
        conv = conv + a1 * wdw_ref[1:2, sl]
        conv = conv + a * wdw_ref[2:3, sl]
        h_sc[:, sl] = (_gelu_tanh(conv) * b).astype(BF16)
    f = _dot(h_sc[...], wdown_ref[...])
    x2 = x + _rms(f, gpost_ref[...])
    ple = _dot(p_ref[...].astype(BF16), wple_ref[...])
    gate = _dot(x2.astype(BF16), wgate_ref[...])
    y_ref[...] = x2 + ple * _sigmoid(gate)


def _ffn(x, o_bf, wmix_bf, gmix, gpre, wup_bf, wdw, bdw, wdown_bf, gpost, p, wple_bf, wgate_bf, bm, seq_len=None,
         prev=None):
    rows = x.shape[0]
    row = lambda i: (i, 0)
    seq_mode = prev is None
    in_specs = [pl.BlockSpec((bm, D_MODEL), row), pl.BlockSpec((bm, o_bf.shape[1]), row), _resident(wmix_bf.shape),
                _resident((1, D_MODEL)), _resident((1, D_MODEL)), _resident(wup_bf.shape),
                _resident(wdw.shape), _resident((1, D_FF)), _resident(wdown_bf.shape), _resident((1, D_MODEL)),
                pl.BlockSpec((bm, PLE_DIM), row), _resident(wple_bf.shape), _resident(wgate_bf.shape)]
    args = [x, o_bf, wmix_bf, gmix, gpre, wup_bf, wdw, bdw, wdown_bf, gpost, p, wple_bf, wgate_bf]
    scratch = [pltpu.VMEM((bm, D_FF), BF16)]
    if seq_mode:
        bps = seq_len // bm
        out_specs = [pl.BlockSpec((bm, D_MODEL), row), pl.BlockSpec((1, SUBLANES, D_FF), lambda i: (i // bps, 0, 0))]
        out_shape = [jax.ShapeDtypeStruct((rows, D_MODEL), F32),
                     jax.ShapeDtypeStruct((rows // seq_len, SUBLANES, D_FF), F32)]
        scratch.append(pltpu.VMEM((SUBLANES, D_FF), F32))
    else:
        bps = 1
        in_specs += [pl.BlockSpec((bm, D_FF), row), pl.BlockSpec((bm, D_FF), row)]
        args += list(prev)
        out_specs = [pl.BlockSpec((bm, D_MODEL), row), pl.BlockSpec((bm, D_FF), row)]
        out_shape = [jax.ShapeDtypeStruct((rows, D_MODEL), F32), jax.ShapeDtypeStruct((rows, D_FF), F32)]
    return pl.pallas_call(
        functools.partial(_ffn_kernel, seq_mode=seq_mode, blocks_per_seq=bps),
        grid=(rows // bm,),
        in_specs=in_specs, out_specs=out_specs, out_shape=out_shape, scratch_shapes=scratch,
        compiler_params=_cparams(("arbitrary",)),
        name="conv_ffn",
    )(*args)


def _kvq_kernel(x_ref, gkv_ref, wkv_ref, wf_ref, bf_ref, gq_ref, wq_ref,
                k_ref, v_ref, lf_ref, kb_ref, vb_ref, qb_ref, kn_ref, qn_ref, *, per_head_out):
    x = x_ref[...]
    xn = _rms(x, gkv_ref[...]).astype(BF16)
    k = _dot(xn, wkv_ref[:, :FOX_HD])
    v = _dot(xn, wkv_ref[:, FOX_HD:])
    if per_head_out:
        k_ref[...] = k.reshape(k.shape[0], FOX_HEADS, FOX_DH)
        v_ref[...] = v.reshape(v.shape[0], FOX_HEADS, FOX_DH)
    else:
        k_ref[...] = k
        v_ref[...] = v
    kb = k.astype(BF16)
    kb_ref[...] = kb
    vb_ref[...] = v.astype(BF16)
    f = _dot(xn, wf_ref[...]) + bf_ref[...]
    lf_ref[...] = _log_sigmoid(f)[:, :FOX_HEADS]
    xq = _rms(x, gq_ref[...]).astype(BF16)
    qb = (_dot(xq, wq_ref[...]) * FOX_DH ** -0.5).astype(BF16)
    qb_ref[...] = qb
    head_sum = (lax.broadcasted_iota(jnp.int32, (FOX_HD, LANES), 0) // FOX_DH ==
                lax.broadcasted_iota(jnp.int32, (FOX_HD, LANES), 1))
    head_sum = jnp.where(head_sum, 1.0, 0.0).astype(BF16)
    for src, dst in ((kb, kn_ref), (qb, qn_ref)):
        sf = src.astype(F32)
        n2 = jnp.max(_dot((sf * sf).astype(BF16), head_sum), axis=0, keepdims=True)
        dst[0] = jnp.broadcast_to(n2, (SUBLANES, LANES))


def _kvq(x, gkv, wkv_bf, wf_bf, bf_pad, gq, wq_bf, bm, per_head_out):
    rows = x.shape[0]
    row = lambda i: (i, 0)
    wide = pl.BlockSpec((bm, FOX_HD), row)
    norm = pl.BlockSpec((1, SUBLANES, LANES), lambda i: (i, 0, 0))
    norm_shape = jax.ShapeDtypeStruct((rows // bm, SUBLANES, LANES), F32)
    if per_head_out:
        kv_spec = pl.BlockSpec((bm, FOX_HEADS, FOX_DH), lambda i: (i, 0, 0))
        kv_shape = jax.ShapeDtypeStruct((rows, FOX_HEADS, FOX_DH), F32)
    else:
        kv_spec, kv_shape = wide, jax.ShapeDtypeStruct((rows, FOX_HD), F32)
    return pl.pallas_call(
        functools.partial(_kvq_kernel, per_head_out=per_head_out),
        grid=(rows // bm,),
        in_specs=[pl.BlockSpec((bm, D_MODEL), row), _resident((1, D_MODEL)), _resident(wkv_bf.shape),
                  _resident(wf_bf.shape), _resident((1, LANES)), _resident((1, D_MODEL)), _resident(wq_bf.shape)],
        out_specs=[kv_spec, kv_spec, pl.BlockSpec((bm, FOX_HEADS), row), wide, wide, wide, norm, norm],
        out_shape=[kv_shape, kv_shape,
                   jax.ShapeDtypeStruct((rows, FOX_HEADS), F32), jax.ShapeDtypeStruct((rows, FOX_HD), BF16),
                   jax.ShapeDtypeStruct((rows, FOX_HD), BF16), jax.ShapeDtypeStruct((rows, FOX_HD), BF16),
                   norm_shape, norm_shape],
        compiler_params=_cparams(("arbitrary",)),
        name="kv_q_proj",
    )(x, gkv, wkv_bf, wf_bf, bf_pad, gq, wq_bf)


def _cumsum_kernel(lf_ref, cq_ref, ck_ref, first_ref, last_ref, carry_sc):
    bl = lf_ref.shape[0]

    @pl.when(pl.program_id(1) == 0)
    def _():
        carry_sc[...] = jnp.zeros_like(carry_sc)

    lf = lf_ref[...]
    tri = (lax.broadcasted_iota(jnp.int32, (bl, bl), 1) <= lax.broadcasted_iota(jnp.int32, (bl, bl), 0))
    tri = jnp.where(tri, 1.0, 0.0).astype(BF16)
    hi, mid, lo = _split3(lf)
    c = (_dot(tri, lo) + _dot(tri, mid)) + _dot(tri, hi) + carry_sc[...]
    carry_sc[...] = c[bl - 1:bl, :]
    first_ref[0] = c[0:1, :]
    last_ref[0] = c[bl - 1:bl, :]

    lane_of_head = lax.broadcasted_iota(jnp.int32, (FOX_HEADS, LANES), 1) - 3 * lax.broadcasted_iota(
        jnp.int32, (FOX_HEADS, LANES), 0)
    lane = lax.broadcasted_iota(jnp.int32, (1, LANES), 1)
    cq = jnp.where(lane < BIAS_ONES, 1.0, 0.0)
    ck = jnp.where(jnp.logical_and(lane >= BIAS_ONES, lane < 2 * BIAS_ONES), 1.0, 0.0)
    for piece, part in enumerate(_split3(c)):
        cq = cq + _dot(part, jnp.where(lane_of_head == BIAS_ONES + piece, 1.0, 0.0).astype(BF16))
        ck = ck + _dot(part, jnp.where(lane_of_head == piece, -1.0, 0.0).astype(BF16))
    cq_ref[...] = cq.astype(BF16)
    ck_ref[...] = ck.astype(BF16)


def _cumsum_time(lf, batch, seq):
    bl = min(ATT_BLOCK, seq)
    nb = seq // bl
    blk = lambda b, i: (b * nb + i, 0)
    wide = pl.BlockSpec((bl, LANES), blk)
    edge = pl.BlockSpec((1, 1, FOX_HEADS), lambda b, i: (b * nb + i, 0, 0))
    return pl.pallas_call(
        _cumsum_kernel,
        grid=(batch, nb),
        in_specs=[pl.BlockSpec((bl, FOX_HEADS), blk)], out_specs=[wide, wide, edge, edge],
        out_shape=[jax.ShapeDtypeStruct((lf.shape[0], LANES), BF16)] * 2
        + [jax.ShapeDtypeStruct((batch * nb, 1, FOX_HEADS), F32)] * 2,
        scratch_shapes=[pltpu.VMEM((1, FOX_HEADS), F32)],
        compiler_params=_cparams(("arbitrary", "arbitrary")),
        name="logf_cumsum",
    )(lf)


def _fox_prompt_kernel(cfirst_ref, clast_ref, qkb_ref, q_ref, k_ref, v_ref, cq_ref, ck_ref, o_ref, m_sc, acc_sc):
    b = pl.program_id(0)
    hg = pl.program_id(1)
    i = pl.program_id(2)

    def dead(j):
        worst = None
        for hh in range(ATT_HEADS):
            h = ATT_HEADS * hg + hh
            gap = cfirst_ref[b, i, h] - clast_ref[b, j, h] + 2.0 * qkb_ref[b, h]
            worst = gap if worst is None else jnp.maximum(worst, gap)
        return worst < -EXP_UNDERFLOW

    j_lo = lax.while_loop(lambda j: jnp.logical_and(j < i, dead(j)), lambda j: j + 1, jnp.int32(0))
    bq = q_ref.shape[0]
    pair = 2 * FOX_DH
    pair_lanes = lambda ref, rows, p: ref[rows, p * pair:(p + 1) * pair]
    low = lax.broadcasted_iota(jnp.int32, (1, pair), 1) < FOX_DH
    causal = (lax.broadcasted_iota(jnp.int32, (bq, bq), 1) <= lax.broadcasted_iota(jnp.int32, (bq, bq), 0))
    ones = jnp.ones((bq, pair), BF16)
    lanes = lambda x, n: jnp.concatenate([x] * n, axis=1)
    bias_lane = lax.broadcasted_iota(jnp.int32, (1, LANES), 1)
    cq = cq_ref[...]
    q_aug = []
    for hh in range(ATT_HEADS):
        q = pair_lanes(q_ref, slice(None), hh // 2)
        qm = jnp.where(low if hh % 2 == 0 else jnp.logical_not(low), q, jnp.zeros_like(q))
        first = 3 * (ATT_HEADS * hg + hh)
        own = jnp.logical_or(jnp.logical_and(bias_lane >= first, bias_lane < first + 3),
                             jnp.logical_and(bias_lane >= first + BIAS_ONES, bias_lane < first + BIAS_ONES + 3))
        q_aug.append(jnp.concatenate([qm, jnp.where(own, cq, jnp.zeros_like(cq))], axis=1))

    def block(j, masked):
        rows = pl.ds(pl.multiple_of(j * bq, bq), bq)
        ck = ck_ref[rows, :]
        k_aug = [jnp.concatenate([pair_lanes(k_ref, rows, p), ck], axis=1) for p in range(ATT_HEADS // 2)]
        v_aug = [jnp.concatenate([pair_lanes(v_ref, rows, p), ones], axis=1) for p in range(ATT_HEADS // 2)]
        for hh in range(ATT_HEADS):
            s = _dot_nt(q_aug[hh], k_aug[hh // 2])
            if masked:
                s = jnp.where(causal, s, -jnp.inf)
                m_next = jnp.broadcast_to(jnp.max(s, axis=1, keepdims=True), (bq, LANES))
                p = jnp.exp(s - lanes(m_next, bq // LANES))
                acc_sc[hh] = _dot(p.astype(BF16), v_aug[hh // 2])
            else:
                m_prev = m_sc[hh]
                m_next = jnp.maximum(m_prev, jnp.max(s, axis=1, keepdims=True))
                alpha = jnp.exp(m_prev - m_next)
                p = jnp.exp(s - lanes(m_next, bq // LANES))
                acc_sc[hh] = lanes(alpha, 2) * acc_sc[hh] + _dot(p.astype(BF16), v_aug[hh // 2])
            m_sc[hh] = m_next

    n_live = i - j_lo + 1

    @pl.when(n_live == 1)
    def _():
        block(i, True)

    @pl.when(n_live == 2)
    def _():
        block(i, True)
        block(i - 1, False)

    @pl.when(n_live >= 3)
    def _():
        block(i, True)
        block(i - 1, False)
        block(i - 2, False)

    def body(t, carry):
        block(i - 3 - 2 * t, False)
        block(i - 4 - 2 * t, False)
        return carry

    rest = jnp.maximum(n_live - 3, 0)
    lax.fori_loop(0, rest // 2, body, 0)

    @pl.when(rest % 2 == 1)
    def _():
        block(j_lo, False)

    for p in range(ATT_HEADS // 2):
        o0 = acc_sc[2 * p, :, :pair] / acc_sc[2 * p, :, pair:]
        o1 = acc_sc[2 * p + 1, :, :pair] / acc_sc[2 * p + 1, :, pair:]
        o_ref[:, p * pair:(p + 1) * pair] = jnp.where(low, o0, o1).astype(o_ref.dtype)


def _fox_prompt(q_bf, k_bf, v_bf, cq, ck, c_first, c_last, qk_bound, batch, seq):
    bq = min(ATT_BLOCK, seq)
    nq = seq // bq
    pair = 2 * FOX_DH
    width = ATT_HEADS * FOX_DH
    qblk = pl.BlockSpec((bq, width), lambda b, hg, i, *_: (b * nq + i, hg))
    kvblk = pl.BlockSpec((seq, width), lambda b, hg, i, *_: (b, hg))
    return pl.pallas_call(
        _fox_prompt_kernel,
        grid_spec=pltpu.PrefetchScalarGridSpec(
            num_scalar_prefetch=3,
            grid=(batch, FOX_HEADS // ATT_HEADS, nq),
            in_specs=[qblk, kvblk, kvblk,
                      pl.BlockSpec((bq, LANES), lambda b, hg, i, *_: (b * nq + i, 0)),
                      pl.BlockSpec((seq, LANES), lambda b, hg, i, *_: (b, 0))],
            out_specs=qblk,
            scratch_shapes=[pltpu.VMEM((ATT_HEADS, bq, LANES), F32), pltpu.VMEM((ATT_HEADS, bq, 2 * pair), F32)]),
        out_shape=jax.ShapeDtypeStruct((batch * seq, FOX_HD), BF16),
        compiler_params=_cparams(("arbitrary", "arbitrary", "arbitrary")),
        name="fox_attention_prompt",
    )(c_first, c_last, qk_bound, q_bf, k_bf, v_bf, cq, ck)


def _fox_scores_kernel(pt_ref, q_ref, knew_ref, lfnew_ref, *refs, pp):
    k_refs, lf_refs = refs[:pp], refs[pp:2 * pp]
    s_ref, pmax_ref, m_ref, sself_ref, q_sc, carry_sc = refs[2 * pp:]
    step = pl.program_id(1)
    page = k_refs[0].shape[3]

    @pl.when(step == 0)
    def _():
        for hp in range(FOX_HEADS // 2):
            qc, kc = [_as_column(r[0, :, hp * LANES:(hp + 1) * LANES]) for r in (q_ref, knew_ref)]
            for hl in range(2):
                h = 2 * hp + hl
                qh = qc[hl * FOX_DH:(hl + 1) * FOX_DH]
                q_sc[h] = qh
                sself_ref[0, h:h + 1, :] = jnp.sum(qh * kc[hl * FOX_DH:(hl + 1) * FOX_DH], axis=0, keepdims=True)
        m_ref[0] = sself_ref[0]
        carry_sc[...] = _as_column(lfnew_ref[0])[:FOX_HEADS]

    later = (lax.broadcasted_iota(jnp.int32, (page, page), 0) > lax.broadcasted_iota(jnp.int32, (page, page), 1))
    later = jnp.where(later, 1.0, 0.0).astype(BF16)
    carry = carry_sc[...]
    biases = []
    for i in range(pp):
        lf = lf_refs[i][0]
        hi, mid, lo = _split3(lf)
        biases.append((_dot(lo, later) + _dot(mid, later)) + _dot(hi, later) + carry)
        carry = carry + jnp.sum(lf, axis=1, keepdims=True)
    carry_sc[...] = carry
    for h in range(FOX_HEADS):
        qh = q_sc[h]
        for i in range(pp):
            s_ref[0, h:h + 1, i * page:(i + 1) * page] = (
                jnp.sum(k_refs[i][0, h] * qh, axis=0, keepdims=True) + biases[i][h:h + 1, :])
    m = m_ref[0]
    for i in range(pp):
        page_max = jnp.max(s_ref[0, :, i * page:(i + 1) * page], axis=1, keepdims=True)
        pmax_ref[0, i] = jnp.broadcast_to(page_max, (FOX_HEADS, LANES))
        m = jnp.maximum(m, page_max)
    m_ref[0] = m


def _fox_values_kernel(page_ref, walk_ref, count_ref, s_ref, m_ref, sself_ref, vnew_ref, *refs, pp):
    v_refs = refs[:pp]
    o_ref, l_sc, acc_sc = refs[pp:]
    b = pl.program_id(0)
    step = pl.program_id(1)
    page = v_refs[0].shape[3]
    m = m_ref[0]

    @pl.when(step == 0)
    def _():
        p_self = jnp.exp(sself_ref[0] - m)
        l_sc[...] = p_self
        lane = lax.broadcasted_iota(jnp.int32, (FOX_DH, page), 1)
        for hp in range(FOX_HEADS // 2):
            vc = _as_column(vnew_ref[0, :, hp * LANES:(hp + 1) * LANES])
            for hl in range(2):
                h = 2 * hp + hl
                acc_sc[h] = jnp.where(lane == 0, vc[hl * FOX_DH:(hl + 1) * FOX_DH] * p_self[h:h + 1, :], 0.0)

    n_live = count_ref[b]
    for i in range(pp):
        slot = step * pp + i

        @pl.when(slot < n_live)
        def _():
            start = pl.multiple_of(walk_ref[b, slot] * page, page)
            p = jnp.exp(s_ref[0, :, pl.ds(start, page)] - m)
            l_sc[...] = l_sc[...] + jnp.sum(p, axis=1, keepdims=True)
            for h in range(FOX_HEADS):
                acc_sc[h] = acc_sc[h] + v_refs[i][0, h] * p[h:h + 1, :]

    @pl.when(step == pl.num_programs(1) - 1)
    def _():
        o_ref[0] = (jnp.sum(acc_sc[...], axis=2) / l_sc[:, :1]).astype(o_ref.dtype)


def _fox_decode(q, k_new, v_new, lf_new, cache_k, cache_v, cache_logf, page_table):
    nb, n_pages = page_table.shape
    page = cache_k.shape[1]
    assert page == LANES, "one cache page fills the lane axis"
    pp = min(PAGES_PER_STEP, n_pages)
    n_steps = n_pages // pp
    ck = jnp.transpose(cache_k, (0, 2, 3, 1))
    cv = jnp.transpose(cache_v, (0, 2, 3, 1))
    clf = jnp.transpose(cache_logf, (0, 2, 1))
    as_row = lambda a: a.reshape(nb, 1, FOX_HD)
    lf_row = jnp.pad(lf_new, ((0, 0), (0, LANES - FOX_HEADS))).reshape(nb, 1, LANES)
    heads_lanes = jax.ShapeDtypeStruct((nb, FOX_HEADS, LANES), F32)

    def k_page(i):
        return pl.BlockSpec((1, FOX_HEADS, FOX_DH, page),
                            lambda b, s, pt: (pt[b, n_pages - 1 - (s * pp + i)], 0, 0, 0))

    def lf_page(i):
        return pl.BlockSpec((1, FOX_HEADS, page), lambda b, s, pt: (pt[b, n_pages - 1 - (s * pp + i)], 0, 0))

    row = pl.BlockSpec((1, 1, FOX_HD), lambda b, s, *_: (b, 0, 0))
    per_seq = pl.BlockSpec((1, FOX_HEADS, LANES), lambda b, s, *_: (b, 0, 0))
    scores, page_max, m, s_self = pl.pallas_call(
        functools.partial(_fox_scores_kernel, pp=pp),
        grid_spec=pltpu.PrefetchScalarGridSpec(
            num_scalar_prefetch=1, grid=(nb, n_steps),
            in_specs=[row, row, pl.BlockSpec((1, 1, LANES), lambda b, s, pt: (b, 0, 0))]
            + [k_page(i) for i in range(pp)] + [lf_page(i) for i in range(pp)],
            out_specs=[pl.BlockSpec((1, FOX_HEADS, pp * page), lambda b, s, pt: (b, 0, s)),
                       pl.BlockSpec((1, pp, FOX_HEADS, LANES), lambda b, s, pt: (b, s, 0, 0)), per_seq, per_seq],
            scratch_shapes=[pltpu.VMEM((FOX_HEADS, FOX_DH, page), F32), pltpu.VMEM((FOX_HEADS, page), F32)]),
        out_shape=[jax.ShapeDtypeStruct((nb, FOX_HEADS, n_pages * page), F32),
                   jax.ShapeDtypeStruct((nb, n_pages, FOX_HEADS, LANES), F32), heads_lanes, heads_lanes],
        compiler_params=_cparams(("arbitrary", "arbitrary")),
        name="fox_decode_scores",
    )(page_table, as_row(q), as_row(k_new), lf_row, *([ck] * pp), *([clf] * pp))

    live = jnp.any(page_max[:, :, :, 0] - m[:, None, :, 0] >= -EXP_UNDERFLOW, axis=2)
    count = jnp.sum(live, axis=1).astype(jnp.int32)
    walk = jnp.argsort(jnp.logical_not(live), axis=1, stable=True).astype(jnp.int32)
    pages = jnp.take_along_axis(page_table[:, ::-1], walk, axis=1)
    pp = min(VALUE_PAGES_PER_STEP, n_pages)
    n_steps = n_pages // pp
    is_live = (jnp.arange(n_pages, dtype=jnp.int32)[None, :] < count[:, None]).reshape(nb * n_steps, pp)
    flat = pages.reshape(nb * n_steps, pp)
    last_live = lax.cummax(jnp.where(is_live, jnp.arange(nb * n_steps, dtype=jnp.int32)[:, None], 0), axis=0)
    pages = jnp.take_along_axis(flat, last_live, axis=0).reshape(nb, n_pages)

    def v_page(i):
        return pl.BlockSpec((1, FOX_HEADS, FOX_DH, page), lambda b, s, pg, wk, ct: (pg[b, s * pp + i], 0, 0, 0))

    out = pl.pallas_call(
        functools.partial(_fox_values_kernel, pp=pp),
        grid_spec=pltpu.PrefetchScalarGridSpec(
            num_scalar_prefetch=3, grid=(nb, n_steps),
            in_specs=[pl.BlockSpec((1, FOX_HEADS, n_pages * page), lambda b, s, *_: (b, 0, 0)), per_seq, per_seq, row]
            + [v_page(i) for i in range(pp)],
            out_specs=pl.BlockSpec((1, FOX_HEADS, FOX_DH), lambda b, s, *_: (b, 0, 0)),
            scratch_shapes=[pltpu.VMEM((FOX_HEADS, LANES), F32), pltpu.VMEM((FOX_HEADS, FOX_DH, page), F32)]),
        out_shape=jax.ShapeDtypeStruct((nb, FOX_HEADS, FOX_DH), BF16),
        compiler_params=_cparams(("arbitrary", "arbitrary")),
        name="fox_decode_values",
    )(pages, walk, count, scores, m, s_self, as_row(v_new), *([cv] * pp))
    return out.reshape(nb, FOX_HD)


def _run_group(x, p, cos, sin, w, bm, seq_len, sample):
    rows = x.shape[0]
    qkv_dtype = BF16 if sample is None else F32
    q, k, v, gate = _ret_in(x, w["g_pre_mix"][0], w["w_in"], cos, sin, bm, qkv_dtype)
    if sample is None:
        batch = rows // seq_len
        o, ret_state = _retention_prompt(q, k, v, gate, w["log_gamma"], batch, seq_len)
    else:
        o, ret_state = _retention_step(sample["state_ret"], q, k, v, gate, w["log_gamma"])

    conv_out = []
    ffn_args = lambda i: (w["g_post_mix"][i], w["g_pre_ffn"][i], w["w_up"][i], w["w_dw"][i], w["b_dw"][i],
                          w["w_down"][i], w["g_post_ffn"][i], p[i], w["w_ple"][i], w["w_gate"][i], bm)
    if sample is None:
        x, tail = _ffn(x, o, w["w_out_ret"], *ffn_args(0), seq_len=seq_len)
        conv_out.append(tail[:, SUBLANES - 2:, :])
    else:
        buf = sample["state_conv"][0]
        x, a = _ffn(x, o, w["w_out_ret"], *ffn_args(0), prev=(buf[:, 0], buf[:, 1]))
        conv_out.append(jnp.stack([buf[:, 1], a], axis=1))

    k_new, v_new, lf_new, k_bf, v_bf, q_bf, k_n2, q_n2 = _kvq(
        x, w["g_kv"], w["w_kv"], w["w_f"], w["b_f"], w["g_pre_mix"][1], w["w_q"], bm, per_head_out=sample is None)
    if sample is None:
        cq, ck, c_first, c_last = _cumsum_time(lf_new, batch, seq_len)
        per_seq_max = lambda n2: jnp.max(n2.reshape(batch, -1, SUBLANES, LANES)[:, :, 0, :FOX_HEADS], axis=1)
        qk_bound = jnp.sqrt(per_seq_max(k_n2) * per_seq_max(q_n2)) * NORM_PAD
        edges = lambda e: e.reshape(batch, -1, FOX_HEADS)
        att = _fox_prompt(q_bf, k_bf, v_bf, cq, ck, edges(c_first), edges(c_last), qk_bound, batch, seq_len)
    else:
        att = _fox_decode(q_bf.astype(F32), k_new, v_new, lf_new, sample["cache_k"], sample["cache_v"],
                          sample["cache_logf"], sample["page_table"])

    if sample is None:
        x, tail = _ffn(x, att, w["w_out_fox"], *ffn_args(1), seq_len=seq_len)
        conv_out.append(tail[:, SUBLANES - 2:, :])
    else:
        buf = sample["state_conv"][1]
        x, a = _ffn(x, att, w["w_out_fox"], *ffn_args(1), prev=(buf[:, 0], buf[:, 1]))
        conv_out.append(jnp.stack([buf[:, 1], a], axis=1))
    return x, ret_state, jnp.stack(conv_out), k_new, v_new, lf_new


def kernel(x_prompt, x_sample, state_ret, state_conv, cache_k, cache_v, cache_logf, page_table, p_prompt, p_sample,
           norm_pre_mix, norm_post_mix, norm_pre_ffn, norm_post_ffn, w_in_ret, w_out_ret, norm_kv, w_kvf, b_f,
           w_q_fox, w_out_fox, w_up, w_dw, b_dw, w_down, w_ple, w_ple_gate):
    bp, tp, _ = x_prompt.shape
    nb, ts, _ = x_sample.shape
    assert ts == 1, "the sample group is one new token per sequence"
    past_len = page_table.shape[1] * cache_k.shape[1]
    depth = w_up.shape[0]
    row_vec = lambda a: a.reshape(a.shape[0], 1, a.shape[1])
    w = {
        "g_pre_mix": row_vec(norm_pre_mix), "g_post_mix": row_vec(norm_post_mix),
        "g_pre_ffn": row_vec(norm_pre_ffn), "g_post_ffn": row_vec(norm_post_ffn),
        "g_kv": norm_kv.reshape(1, D_MODEL),
        "w_in": w_in_ret[0].astype(BF16), "w_out_ret": w_out_ret[0].astype(BF16),
        "w_kv": w_kvf[:, :2 * FOX_HD].astype(BF16),
        "w_f": jnp.pad(w_kvf[:, 2 * FOX_HD:], ((0, 0), (0, LANES - FOX_HEADS))).astype(BF16),
        "b_f": jnp.pad(b_f, (0, LANES - FOX_HEADS)).reshape(1, LANES),
        "w_q": w_q_fox[0].astype(BF16), "w_out_fox": w_out_fox[0].astype(BF16),
        "w_up": w_up.astype(BF16), "w_dw": w_dw, "b_dw": row_vec(b_dw), "w_down": w_down.astype(BF16),
        "w_ple": w_ple.astype(BF16), "w_gate": w_ple_gate.astype(BF16),
        "log_gamma": jnp.log1p(-jnp.exp2(-5.0 - jnp.arange(RET_HEADS, dtype=F32))),
    }
    bm = min(ROW_BLOCK, tp)
    cos_p, sin_p = _rope_table(tp, 0)
    y_p, ret_p, conv_p, k_p, v_p, lf_p = _run_group(
        x_prompt.reshape(bp * tp, D_MODEL), p_prompt.reshape(depth, bp * tp, PLE_DIM), cos_p, sin_p, w, bm, tp, None)
    cos_s, sin_s = _rope_table(SUBLANES, past_len)
    cos_s = jnp.broadcast_to(cos_s[:1], (nb, ROPE_HALF))
    sin_s = jnp.broadcast_to(sin_s[:1], (nb, ROPE_HALF))
    sample = {"state_ret": state_ret[0], "state_conv": state_conv, "cache_k": cache_k, "cache_v": cache_v,
              "cache_logf": cache_logf, "page_table": page_table}
    y_s, ret_s, conv_s, k_s, v_s, lf_s = _run_group(
        x_sample.reshape(nb, D_MODEL), p_sample.reshape(depth, nb, PLE_DIM), cos_s, sin_s, w, nb, 1, sample)
    return (y_p.reshape(bp, tp, D_MODEL), y_s.reshape(nb, 1, D_MODEL),
            ret_p[None], ret_s[None], conv_p, conv_s,
            k_p.reshape(bp, tp, FOX_HEADS, FOX_DH), k_s.reshape(nb, 1, FOX_HEADS, FOX_DH),
            v_p.reshape(bp, tp, FOX_HEADS, FOX_DH), v_s.reshape(nb, 1, FOX_HEADS, FOX_DH),
            lf_p.reshape(bp, tp, FOX_HEADS), lf_s.reshape(nb, 1, FOX_HEADS))
```

```python
import functools
import math

import jax
import jax.numpy as jnp
from jax import lax
from jax.experimental import pallas as pl
from jax.experimental.pallas import tpu as pltpu

F32 = jnp.float32
BF16 = jnp.bfloat16

D_MODEL = 1024
RET_HEADS = 4
RET_DK = 256
RET_DV = 512
RET_QK = RET_HEADS * RET_DK
RET_VD = RET_HEADS * RET_DV
ROPE_BASE = 10000.0
ROPE_HALF = RET_DK // 2
FOX_HEADS = 16
FOX_DH = 64
FOX_HD = FOX_HEADS * FOX_DH
D_FF = 2816
PLE_DIM = 256
NORM_EPS = 1e-6

LANES = 128
SUBLANES = 8
VMEM_LIMIT = 56 * 1024 * 1024

ROW_BLOCK = 512
RET_CHUNK = 256
FFN_CHUNK = 256
ATT_BLOCK = 512
ATT_HEADS = 4
PAGES_PER_STEP = 32
VALUE_PAGES_PER_STEP = 32
BIAS_ONES = 48
EXP_UNDERFLOW = 110.0
NORM_PAD = 1.05


def _cparams(sem):
    return pltpu.CompilerParams(dimension_semantics=sem, vmem_limit_bytes=VMEM_LIMIT)


def _resident(shape):
    return pl.BlockSpec(shape, lambda *_: (0,) * len(shape), pipeline_mode=pl.Buffered(1))


def _dot(a, b):
    return jnp.dot(a, b, preferred_element_type=F32)


def _dot_nt(a, b):
    return lax.dot_general(a, b, (((1,), (1,)), ((), ())), preferred_element_type=F32)


def _dot_tn(a, b):
    return lax.dot_general(a, b, (((0,), (0,)), ((), ())), preferred_element_type=F32)


def _rms(x, g):
    return x * lax.rsqrt(jnp.mean(x * x, axis=-1, keepdims=True) + NORM_EPS) * g


def _sigmoid(x):
    return 1.0 / (1.0 + jnp.exp(-x))


def _gelu_tanh(x):
    return x * (0.5 * (1.0 + jnp.tanh(math.sqrt(2.0 / math.pi) * (x + 0.044715 * (x * x * x)))))


def _log_sigmoid(x):
    z = -x
    return -(jnp.maximum(z, 0.0) + jnp.log1p(jnp.exp(-jnp.abs(z))))


def _split3(x):
    hi = x.astype(BF16)
    r1 = x - hi.astype(F32)
    mid = r1.astype(BF16)
    lo = (r1 - mid.astype(F32)).astype(BF16)
    return hi, mid, lo


def _rope_kernel(cos_ref, sin_ref, *, base_pos):
    n = cos_ref.shape[0]
    i = pl.program_id(0)
    lane = lax.broadcasted_iota(jnp.int32, (1, ROPE_HALF), 1).astype(F32)
    inv = jnp.power(jnp.full((1, ROPE_HALF), ROPE_BASE, F32), -(lane / ROPE_HALF))
    pos = (lax.broadcasted_iota(jnp.int32, (n, ROPE_HALF), 0) + (i * n + base_pos)).astype(F32)
    ang = pos * inv
    cos_ref[...] = jnp.cos(ang)
    sin_ref[...] = jnp.sin(ang)


def _rope_table(n_rows, base_pos):
    bn = min(n_rows, 1024)
    spec = pl.BlockSpec((bn, ROPE_HALF), lambda i: (i, 0))
    return pl.pallas_call(
        functools.partial(_rope_kernel, base_pos=base_pos),
        grid=(n_rows // bn,),
        out_specs=[spec, spec],
        out_shape=[jax.ShapeDtypeStruct((n_rows, ROPE_HALF), F32)] * 2,
        compiler_params=_cparams(("arbitrary",)),
        name="rope_table",
    )()


def _ret_in_kernel(x_ref, g_ref, w_ref, cos_ref, sin_ref, q_ref, k_ref, v_ref, gate_ref):
    xn = _rms(x_ref[...], g_ref[...]).astype(BF16)
    cos = cos_ref[...]
    sin = sin_ref[...]
    for out_ref, base, scale in ((q_ref, 0, 1.0), (k_ref, RET_QK, RET_DK ** -0.5)):
        for h in range(RET_HEADS):
            lo = h * RET_DK
            u = _dot(xn, w_ref[:, base + lo:base + lo + RET_DK])
            x1, x2 = u[:, :ROPE_HALF], u[:, ROPE_HALF:]
            out_ref[:, lo:lo + ROPE_HALF] = ((x1 * cos - x2 * sin) * scale).astype(out_ref.dtype)
            out_ref[:, lo + ROPE_HALF:lo + RET_DK] = ((x1 * sin + x2 * cos) * scale).astype(out_ref.dtype)
    for h in range(RET_HEADS):
        lo = h * RET_DV
        v_ref[:, lo:lo + RET_DV] = _dot(xn, w_ref[:, 2 * RET_QK + lo:2 * RET_QK + lo + RET_DV]).astype(v_ref.dtype)
        gate_ref[:, lo:lo + RET_DV] = _dot(xn, w_ref[:, 2 * RET_QK + RET_VD + lo:2 * RET_QK + RET_VD + lo + RET_DV])


def _ret_in(x, g, w_bf, cos, sin, bm, qkv_dtype):
    rows = x.shape[0]
    n_tab = cos.shape[0] // bm
    row = lambda i: (i, 0)
    tab = pl.BlockSpec((bm, ROPE_HALF), lambda i: (i % n_tab, 0))
    return pl.pallas_call(
        _ret_in_kernel,
        grid=(rows // bm,),
        in_specs=[pl.BlockSpec((bm, D_MODEL), row), _resident((1, D_MODEL)), _resident(w_bf.shape), tab, tab],
        out_specs=[pl.BlockSpec((bm, RET_QK), row), pl.BlockSpec((bm, RET_QK), row),
                   pl.BlockSpec((bm, RET_VD), row), pl.BlockSpec((bm, RET_VD), row)],
        out_shape=[jax.ShapeDtypeStruct((rows, RET_QK), qkv_dtype), jax.ShapeDtypeStruct((rows, RET_QK), qkv_dtype),
                   jax.ShapeDtypeStruct((rows, RET_VD), qkv_dtype), jax.ShapeDtypeStruct((rows, RET_VD), F32)],
        compiler_params=_cparams(("arbitrary",)),
        name="ret_in",
    )(x, g, w_bf, cos, sin)


def _ret_chunk_kernel(lg_ref, q_ref, k_ref, v_ref, gate_ref, o_ref, s_ref, decay_sc):
    c = pl.program_id(1)
    C = q_ref.shape[0]

    @pl.when(jnp.logical_and(pl.program_id(0) == 0, c == 0))
    def _():
        rel = lax.broadcasted_iota(jnp.int32, (C, C), 0) - lax.broadcasted_iota(jnp.int32, (C, C), 1)
        dist = jnp.maximum(rel, 0).astype(F32)
        for h in range(RET_HEADS):
            decay_sc[h] = jnp.where(rel >= 0, jnp.exp(dist * lg_ref[h]), 0.0)

    @pl.when(c == 0)
    def _():
        s_ref[...] = jnp.zeros_like(s_ref)

    n = lax.broadcasted_iota(jnp.int32, (C, 1), 0).astype(F32)
    for h in range(RET_HEADS):
        lg = lg_ref[h]
        q = q_ref[:, h * RET_DK:(h + 1) * RET_DK]
        k = k_ref[:, h * RET_DK:(h + 1) * RET_DK]
        v = v_ref[:, h * RET_DV:(h + 1) * RET_DV]
        scores = _dot_nt(q, k) * decay_sc[h]
        inner = _dot(scores.astype(BF16), v)
        state = s_ref[0, h]
        cross = _dot(q, state.astype(BF16)) * jnp.exp((n + 1.0) * lg)
        kd = (k.astype(F32) * jnp.exp((C - 1.0 - n) * lg)).astype(BF16)
        chunk_decay = jnp.exp(jnp.full((1, RET_DV), float(C), F32) * lg)
        s_ref[0, h] = state * chunk_decay + _dot_tn(kd, v)
        o = inner + cross
        o = o * lax.rsqrt(jnp.mean(o * o, axis=-1, keepdims=True) + NORM_EPS)
        gate = gate_ref[:, h * RET_DV:(h + 1) * RET_DV]
        o_ref[:, h * RET_DV:(h + 1) * RET_DV] = (o * (gate * _sigmoid(gate))).astype(o_ref.dtype)


def _retention_prompt(q, k, v, gate, log_gamma, batch, seq):
    C = min(RET_CHUNK, seq)
    nc = seq // C
    blk = lambda b, c, lg: (b * nc + c, 0)
    return pl.pallas_call(
        _ret_chunk_kernel,
        grid_spec=pltpu.PrefetchScalarGridSpec(
            num_scalar_prefetch=1,
            grid=(batch, nc),
            in_specs=[pl.BlockSpec((C, RET_QK), blk), pl.BlockSpec((C, RET_QK), blk),
                      pl.BlockSpec((C, RET_VD), blk), pl.BlockSpec((C, RET_VD), blk)],
            out_specs=[pl.BlockSpec((C, RET_VD), blk),
                       pl.BlockSpec((1, RET_HEADS, RET_DK, RET_DV), lambda b, c, lg: (b, 0, 0, 0))],
            scratch_shapes=[pltpu.VMEM((RET_HEADS, C, C), F32)],
        ),
        out_shape=[jax.ShapeDtypeStruct((batch * seq, RET_VD), BF16),
                   jax.ShapeDtypeStruct((batch, RET_HEADS, RET_DK, RET_DV), F32)],
        compiler_params=_cparams(("arbitrary", "arbitrary")),
        name="retention_chunks",
    )(log_gamma, q, k, v, gate)


def _as_column(row):
    return jnp.transpose(jnp.broadcast_to(row, (LANES, LANES)))


def _ret_step_kernel(lg_ref, s_ref, q_ref, k_ref, v_ref, gate_ref, o_ref, so_ref):
    wide = lambda x: jnp.concatenate([x] * (RET_DV // LANES), axis=1)
    for h in range(RET_HEADS):
        gamma = jnp.exp(jnp.full((1, RET_DV), 1.0, F32) * lg_ref[h])
        state = s_ref[0, h]
        column = lambda ref: jnp.concatenate(
            [_as_column(ref[0, :, h * RET_DK + c * LANES:h * RET_DK + (c + 1) * LANES])
             for c in range(RET_DK // LANES)], axis=0)
        qc = column(q_ref)
        kc = column(k_ref)
        vr = v_ref[0, :, h * RET_DV:(h + 1) * RET_DV]
        qk = jnp.sum(qc * kc, axis=0, keepdims=True)
        qs = jnp.sum(wide(qc) * state, axis=0, keepdims=True)
        o = wide(qk) * vr + qs * gamma
        so_ref[0, h] = state * gamma + wide(kc) * vr
        o = o * lax.rsqrt(jnp.mean(o * o, axis=-1, keepdims=True) + NORM_EPS)
        gate = gate_ref[0, :, h * RET_DV:(h + 1) * RET_DV]
        o_ref[0, :, h * RET_DV:(h + 1) * RET_DV] = (o * (gate * _sigmoid(gate))).astype(o_ref.dtype)


def _retention_step(state, q, k, v, gate, log_gamma):
    nb = state.shape[0]
    st = pl.BlockSpec((1, RET_HEADS, RET_DK, RET_DV), lambda b, lg: (b, 0, 0, 0))
    rowk = pl.BlockSpec((1, 1, RET_QK), lambda b, lg: (b, 0, 0))
    rowv = pl.BlockSpec((1, 1, RET_VD), lambda b, lg: (b, 0, 0))
    o, s_new = pl.pallas_call(
        _ret_step_kernel,
        grid_spec=pltpu.PrefetchScalarGridSpec(
            num_scalar_prefetch=1, grid=(nb,),
            in_specs=[st, rowk, rowk, rowv, rowv], out_specs=[rowv, st]),
        out_shape=[jax.ShapeDtypeStruct((nb, 1, RET_VD), BF16), jax.ShapeDtypeStruct(state.shape, F32)],
        compiler_params=_cparams(("arbitrary",)),
        name="retention_step",
    )(log_gamma, state, q.reshape(nb, 1, RET_QK), k.reshape(nb, 1, RET_QK), v.reshape(nb, 1, RET_VD),
      gate.reshape(nb, 1, RET_VD))
    return o.reshape(nb, RET_VD), s_new


def _ffn_kernel(*refs, seq_mode, blocks_per_seq):
    if seq_mode:
        (x_ref, o_ref, wmix_ref, gmix_ref, gpre_ref, wup_ref, wdw_ref, bdw_ref, wdown_ref, gpost_ref, p_ref,
         wple_ref, wgate_ref, y_ref, tail_ref, h_sc, carry_sc) = refs
    else:
        (x_ref, o_ref, wmix_ref, gmix_ref, gpre_ref, wup_ref, wdw_ref, bdw_ref, wdown_ref, gpost_ref, p_ref,
         wple_ref, wgate_ref, prev2_ref, prev1_ref, y_ref, a_ref, h_sc) = refs
    x = x_ref[...] + _rms(_dot(o_ref[...], wmix_ref[...]), gmix_ref[...])
    bm = x.shape[0]
    xn = _rms(x, gpre_ref[...]).astype(BF16)
    if seq_mode:
        @pl.when(pl.program_id(0) % blocks_per_seq == 0)
        def _():
            carry_sc[...] = jnp.zeros_like(carry_sc)
        row = lax.broadcasted_iota(jnp.int32, (bm, FFN_CHUNK), 0)
    for c in range(D_FF // FFN_CHUNK):
        sl = slice(c * FFN_CHUNK, (c + 1) * FFN_CHUNK)
        a = _dot(xn, wup_ref[:, sl])
        b = _dot(xn, wup_ref[:, D_FF + c * FFN_CHUNK:D_FF + (c + 1) * FFN_CHUNK])
        if seq_mode:
            c2 = carry_sc[SUBLANES - 2:SUBLANES - 1, sl]
            c1 = carry_sc[SUBLANES - 1:SUBLANES, sl]
            a1 = jnp.where(row == 0, c1, pltpu.roll(a, 1, 0))
            a2 = jnp.where(row == 0, c2, jnp.where(row == 1, c1, pltpu.roll(a, 2, 0)))
            last = a[bm - SUBLANES:bm, :]
            carry_sc[:, sl] = last
            tail_ref[0, :, sl] = last
        else:
            a1 = prev1_ref[:, sl]
            a2 = prev2_ref[:, sl]
            a_ref[:, sl] = a
        conv = bdw_ref[:, sl] + a2 * wdw_ref[0:1, sl]
        conv = conv + a1 * wdw_ref[1:2, sl]
        conv = conv + a * wdw_ref[2:3, sl]
        h_sc[:, sl] = (_gelu_tanh(conv) * b).astype(BF16)
    f = _dot(h_sc[...], wdown_ref[...])
    x2 = x + _rms(f, gpost_ref[...])
    ple = _dot(p_ref[...].astype(BF16), wple_ref[...])
    gate = _dot(x2.astype(BF16), wgate_ref[...])
    y_ref[...] = x2 + ple * _sigmoid(gate)


def _ffn(x, o_bf, wmix_bf, gmix, gpre, wup_bf, wdw, bdw, wdown_bf, gpost, p, wple_bf, wgate_bf, bm, layer,
         seq_len=None, prev=None):
    rows = x.shape[0]
    row = lambda i: (i, 0)
    seq_mode = prev is None

    def of_layer(stacked):
        shape = stacked.shape[1:]
        return pl.BlockSpec((None,) + shape, lambda *_: (layer,) + (0,) * len(shape), pipeline_mode=pl.Buffered(1))

    in_specs = [pl.BlockSpec((bm, D_MODEL), row), pl.BlockSpec((bm, o_bf.shape[1]), row), _resident(wmix_bf.shape),
                _resident((1, D_MODEL)), _resident((1, D_MODEL)), of_layer(wup_bf),
                of_layer(wdw), of_layer(bdw), of_layer(wdown_bf), _resident((1, D_MODEL)),
                pl.BlockSpec((None, bm, PLE_DIM), lambda i: (layer, i, 0)), of_layer(wple_bf), of_layer(wgate_bf)]
    args = [x, o_bf, wmix_bf, gmix, gpre, wup_bf, wdw, bdw, wdown_bf, gpost, p, wple_bf, wgate_bf]
    scratch = [pltpu.VMEM((bm, D_FF), BF16)]
    if seq_mode:
        bps = seq_len // bm
        out_specs = [pl.BlockSpec((bm, D_MODEL), row), pl.BlockSpec((1, SUBLANES, D_FF), lambda i: (i // bps, 0, 0))]
        out_shape = [jax.ShapeDtypeStruct((rows, D_MODEL), F32),
                     jax.ShapeDtypeStruct((rows // seq_len, SUBLANES, D_FF), F32)]
        scratch.append(pltpu.VMEM((SUBLANES, D_FF), F32))
    else:
        bps = 1
        in_specs += [pl.BlockSpec((bm, D_FF), row), pl.BlockSpec((bm, D_FF), row)]
        args += list(prev)
        out_specs = [pl.BlockSpec((bm, D_MODEL), row), pl.BlockSpec((bm, D_FF), row)]
        out_shape = [jax.ShapeDtypeStruct((rows, D_MODEL), F32), jax.ShapeDtypeStruct((rows, D_FF), F32)]
    return pl.pallas_call(
        functools.partial(_ffn_kernel, seq_mode=seq_mode, blocks_per_seq=bps),
        grid=(rows // bm,),
        in_specs=in_specs, out_specs=out_specs, out_shape=out_shape, scratch_shapes=scratch,
        compiler_params=_cparams(("arbitrary",)),
        name="conv_ffn",
    )(*args)


def _kvq_kernel(x_ref, gkv_ref, wkv_ref, wf_ref, bf_ref, gq_ref, wq_ref,
                k_ref, v_ref, lf_ref, kb_ref, vb_ref, qb_ref, kn_ref, qn_ref, *, per_head_out):
    x = x_ref[...]
    xn = _rms(x, gkv_ref[...]).astype(BF16)
    k = _dot(xn, wkv_ref[:, :FOX_HD])
    v = _dot(xn, wkv_ref[:, FOX_HD:])
    if per_head_out:
        k_ref[...] = k.reshape(k.shape[0], FOX_HEADS, FOX_DH)
        v_ref[...] = v.reshape(v.shape[0], FOX_HEADS, FOX_DH)
    else:
        k_ref[...] = k
        v_ref[...] = v
    kb = k.astype(BF16)
    kb_ref[...] = kb
    vb_ref[...] = v.astype(BF16)
    f = _dot(xn, wf_ref[...]) + bf_ref[...]
    lf_ref[...] = _log_sigmoid(f)[:, :FOX_HEADS]
    xq = _rms(x, gq_ref[...]).astype(BF16)
    qb = (_dot(xq, wq_ref[...]) * FOX_DH ** -0.5).astype(BF16)
    qb_ref[...] = qb
    head_sum = (lax.broadcasted_iota(jnp.int32, (FOX_HD, LANES), 0) // FOX_DH ==
                lax.broadcasted_iota(jnp.int32, (FOX_HD, LANES), 1))
    head_sum = jnp.where(head_sum, 1.0, 0.0).astype(BF16)
    for src, dst in ((kb, kn_ref), (qb, qn_ref)):
        sf = src.astype(F32)
        n2 = jnp.max(_dot((sf * sf).astype(BF16), head_sum), axis=0, keepdims=True)
        dst[0] = jnp.broadcast_to(n2, (SUBLANES, LANES))


def _kvq(x, gkv, wkv_bf, wf_bf, bf_pad, gq, wq_bf, bm, per_head_out):
    rows = x.shape[0]
    row = lambda i: (i, 0)
    wide = pl.BlockSpec((bm, FOX_HD), row)
    norm = pl.BlockSpec((1, SUBLANES, LANES), lambda i: (i, 0, 0))
    norm_shape = jax.ShapeDtypeStruct((rows // bm, SUBLANES, LANES), F32)
    if per_head_out:
        kv_spec = pl.BlockSpec((bm, FOX_HEADS, FOX_DH), lambda i: (i, 0, 0))
        kv_shape = jax.ShapeDtypeStruct((rows, FOX_HEADS, FOX_DH), F32)
    else:
        kv_spec, kv_shape = wide, jax.ShapeDtypeStruct((rows, FOX_HD), F32)
    return pl.pallas_call(
        functools.partial(_kvq_kernel, per_head_out=per_head_out),
        grid=(rows // bm,),
        in_specs=[pl.BlockSpec((bm, D_MODEL), row), _resident((1, D_MODEL)), _resident(wkv_bf.shape),
                  _resident(wf_bf.shape), _resident((1, LANES)), _resident((1, D_MODEL)), _resident(wq_bf.shape)],
        out_specs=[kv_spec, kv_spec, pl.BlockSpec((bm, FOX_HEADS), row), wide, wide, wide, norm, norm],
        out_shape=[kv_shape, kv_shape,
                   jax.ShapeDtypeStruct((rows, FOX_HEADS), F32), jax.ShapeDtypeStruct((rows, FOX_HD), BF16),
                   jax.ShapeDtypeStruct((rows, FOX_HD), BF16), jax.ShapeDtypeStruct((rows, FOX_HD), BF16),
                   norm_shape, norm_shape],
        compiler_params=_cparams(("arbitrary",)),
        name="kv_q_proj",
    )(x, gkv, wkv_bf, wf_bf, bf_pad, gq, wq_bf)


def _cumsum_kernel(lf_ref, cq_ref, ck_ref, first_ref, last_ref, carry_sc):
    bl = lf_ref.shape[0]

    @pl.when(pl.program_id(1) == 0)
    def _():
        carry_sc[...] = jnp.zeros_like(carry_sc)

    lf = lf_ref[...]
    tri = (lax.broadcasted_iota(jnp.int32, (bl, bl), 1) <= lax.broadcasted_iota(jnp.int32, (bl, bl), 0))
    tri = jnp.where(tri, 1.0, 0.0).astype(BF16)
    hi, mid, lo = _split3(lf)
    c = (_dot(tri, lo) + _dot(tri, mid)) + _dot(tri, hi) + carry_sc[...]
    carry_sc[...] = c[bl - 1:bl, :]
    first_ref[0] = c[0:1, :]
    last_ref[0] = c[bl - 1:bl, :]

    lane_of_head = lax.broadcasted_iota(jnp.int32, (FOX_HEADS, LANES), 1) - 3 * lax.broadcasted_iota(
        jnp.int32, (FOX_HEADS, LANES), 0)
    lane = lax.broadcasted_iota(jnp.int32, (1, LANES), 1)
    cq = jnp.where(lane < BIAS_ONES, 1.0, 0.0)
    ck = jnp.where(jnp.logical_and(lane >= BIAS_ONES, lane < 2 * BIAS_ONES), 1.0, 0.0)
    for piece, part in enumerate(_split3(c)):
        cq = cq + _dot(part, jnp.where(lane_of_head == BIAS_ONES + piece, 1.0, 0.0).astype(BF16))
        ck = ck + _dot(part, jnp.where(lane_of_head == piece, -1.0, 0.0).astype(BF16))
    cq_ref[...] = cq.astype(BF16)
    ck_ref[...] = ck.astype(BF16)


def _cumsum_time(lf, batch, seq):
    bl = min(ATT_BLOCK, seq)
    nb = seq // bl
    blk = lambda b, i: (b * nb + i, 0)
    wide = pl.BlockSpec((bl, LANES), blk)
    edge = pl.BlockSpec((1, 1, FOX_HEADS), lambda b, i: (b * nb + i, 0, 0))
    return pl.pallas_call(
        _cumsum_kernel,
        grid=(batch, nb),
        in_specs=[pl.BlockSpec((bl, FOX_HEADS), blk)], out_specs=[wide, wide, edge, edge],
        out_shape=[jax.ShapeDtypeStruct((lf.shape[0], LANES), BF16)] * 2
        + [jax.ShapeDtypeStruct((batch * nb, 1, FOX_HEADS), F32)] * 2,
        scratch_shapes=[pltpu.VMEM((1, FOX_HEADS), F32)],
        compiler_params=_cparams(("arbitrary", "arbitrary")),
        name="logf_cumsum",
    )(lf)


def _fox_prompt_kernel(cfirst_ref, clast_ref, qkb_ref, q_ref, k_ref, v_ref, cq_ref, ck_ref, o_ref, m_sc, acc_sc):
    b = pl.program_id(0)
    hg = pl.program_id(1)
    i = pl.program_id(2)

    def dead(j):
        worst = None
        for hh in range(ATT_HEADS):
            h = ATT_HEADS * hg + hh
            gap = cfirst_ref[b, i, h] - clast_ref[b, j, h] + 2.0 * qkb_ref[b, h]
            worst = gap if worst is None else jnp.maximum(worst, gap)
        return worst < -EXP_UNDERFLOW

    j_lo = lax.while_loop(lambda j: jnp.logical_and(j < i, dead(j)), lambda j: j + 1, jnp.int32(0))
    bq = q_ref.shape[0]
    pair = 2 * FOX_DH
    pair_lanes = lambda ref, rows, p: ref[rows, p * pair:(p + 1) * pair]
    low = lax.broadcasted_iota(jnp.int32, (1, pair), 1) < FOX_DH
    causal = (lax.broadcasted_iota(jnp.int32, (bq, bq), 1) <= lax.broadcasted_iota(jnp.int32, (bq, bq), 0))
    ones = jnp.ones((bq, pair), BF16)
    lanes = lambda x, n: jnp.concatenate([x] * n, axis=1)
    bias_lane = lax.broadcasted_iota(jnp.int32, (1, LANES), 1)
    cq = cq_ref[...]
    q_aug = []
    for hh in range(ATT_HEADS):
        q = pair_lanes(q_ref, slice(None), hh // 2)
        qm = jnp.where(low if hh % 2 == 0 else jnp.logical_not(low), q, jnp.zeros_like(q))
        first = 3 * (ATT_HEADS * hg + hh)
        own = jnp.logical_or(jnp.logical_and(bias_lane >= first, bias_lane < first + 3),
                             jnp.logical_and(bias_lane >= first + BIAS_ONES, bias_lane < first + BIAS_ONES + 3))
        q_aug.append(jnp.concatenate([qm, jnp.where(own, cq, jnp.zeros_like(cq))], axis=1))

    def block(j, masked):
        rows = pl.ds(pl.multiple_of(j * bq, bq), bq)
        ck = ck_ref[rows, :]
        k_aug = [jnp.concatenate([pair_lanes(k_ref, rows, p), ck], axis=1) for p in range(ATT_HEADS // 2)]
        v_aug = [jnp.concatenate([pair_lanes(v_ref, rows, p), ones], axis=1) for p in range(ATT_HEADS // 2)]
        for hh in range(ATT_HEADS):
            s = _dot_nt(q_aug[hh], k_aug[hh // 2])
            if masked:
                s = jnp.where(causal, s, -jnp.inf)
                m_next = jnp.broadcast_to(jnp.max(s, axis=1, keepdims=True), (bq, LANES))
                p = jnp.exp(s - lanes(m_next, bq // LANES))
                acc_sc[hh] = _dot(p.astype(BF16), v_aug[hh // 2])
                m_sc[hh] = m_next
            else:
                m_prev = m_sc[hh]
                m_next = jnp.maximum(m_prev, jnp.max(s, axis=1, keepdims=True))
                alpha = jnp.exp(m_prev - m_next)
                p = jnp.exp(s - lanes(m_next, bq // LANES))
                acc_sc[hh] = lanes(alpha, 2) * acc_sc[hh] + _dot(p.astype(BF16), v_aug[hh // 2])
                m_sc[hh] = m_next

    n_live = i - j_lo + 1

    @pl.when(n_live == 1)
    def _():
        block(i, True)

    @pl.when(n_live == 2)
    def _():
        block(i, True)
        block(i - 1, False)

    @pl.when(n_live >= 3)
    def _():
        block(i, True)
        block(i - 1, False)
        block(i - 2, False)

    def body(t, carry):
        block(i - 3 - 2 * t, False)
        block(i - 4 - 2 * t, False)
        return carry

    rest = jnp.maximum(n_live - 3, 0)
    lax.fori_loop(0, rest // 2, body, 0)

    @pl.when(rest % 2 == 1)
    def _():
        block(j_lo, False)

    for p in range(ATT_HEADS // 2):
        o0 = acc_sc[2 * p, :, :pair] / acc_sc[2 * p, :, pair:]
        o1 = acc_sc[2 * p + 1, :, :pair] / acc_sc[2 * p + 1, :, pair:]
        o_ref[:, p * pair:(p + 1) * pair] = jnp.where(low, o0, o1).astype(o_ref.dtype)


def _fox_prompt(q_bf, k_bf, v_bf, cq, ck, c_first, c_last, qk_bound, batch, seq):
    bq = min(ATT_BLOCK, seq)
    nq = seq // bq
    pair = 2 * FOX_DH
    width = ATT_HEADS * FOX_DH
    qblk = pl.BlockSpec((bq, width), lambda b, hg, i, *_: (b * nq + i, hg))
    kvblk = pl.BlockSpec((seq, width), lambda b, hg, i, *_: (b, hg))
    return pl.pallas_call(
        _fox_prompt_kernel,
        grid_spec=pltpu.PrefetchScalarGridSpec(
            num_scalar_prefetch=3,
            grid=(batch, FOX_HEADS // ATT_HEADS, nq),
            in_specs=[qblk, kvblk, kvblk,
                      pl.BlockSpec((bq, LANES), lambda b, hg, i, *_: (b * nq + i, 0)),
                      pl.BlockSpec((seq, LANES), lambda b, hg, i, *_: (b, 0))],
            out_specs=qblk,
            scratch_shapes=[pltpu.VMEM((ATT_HEADS, bq, LANES), F32), pltpu.VMEM((ATT_HEADS, bq, 2 * pair), F32)]),
        out_shape=jax.ShapeDtypeStruct((batch * seq, FOX_HD), BF16),
        compiler_params=_cparams(("arbitrary", "arbitrary", "arbitrary")),
        name="fox_attention_prompt",
    )(c_first, c_last, qk_bound, q_bf, k_bf, v_bf, cq, ck)


def _fox_scores_kernel(pt_ref, q_ref, knew_ref, lfnew_ref, *refs, pp):
    k_refs, lf_refs = refs[:pp], refs[pp:2 * pp]
    s_ref, pmax_ref, m_ref, sself_ref, q_sc, carry_sc = refs[2 * pp:]
    step = pl.program_id(1)
    page = k_refs[0].shape[3]

    @pl.when(step == 0)
    def _():
        for hp in range(FOX_HEADS // 2):
            qc, kc = [_as_column(r[0, :, hp * LANES:(hp + 1) * LANES]) for r in (q_ref, knew_ref)]
            for hl in range(2):
                h = 2 * hp + hl
                qh = qc[hl * FOX_DH:(hl + 1) * FOX_DH]
                q_sc[h] = qh
                sself_ref[0, h:h + 1, :] = jnp.sum(qh * kc[hl * FOX_DH:(hl + 1) * FOX_DH], axis=0, keepdims=True)
        m_ref[0] = sself_ref[0]
        carry_sc[...] = _as_column(lfnew_ref[0])[:FOX_HEADS]

    later = (lax.broadcasted_iota(jnp.int32, (page, page), 0) > lax.broadcasted_iota(jnp.int32, (page, page), 1))
    later = jnp.where(later, 1.0, 0.0).astype(BF16)
    carry = carry_sc[...]
    biases = []
    for i in range(pp):
        lf = lf_refs[i][0]
        hi, mid, lo = _split3(lf)
        biases.append((_dot(lo, later) + _dot(mid, later)) + _dot(hi, later) + carry)
        carry = carry + jnp.sum(lf, axis=1, keepdims=True)
    carry_sc[...] = carry
    for h in range(FOX_HEADS):
        qh = q_sc[h]
        for i in range(pp):
            s_ref[0, h:h + 1, i * page:(i + 1) * page] = (
                jnp.sum(k_refs[i][0, h] * qh, axis=0, keepdims=True) + biases[i][h:h + 1, :])
    m = m_ref[0]
    for i in range(pp):
        page_max = jnp.max(s_ref[0, :, i * page:(i + 1) * page], axis=1, keepdims=True)
        pmax_ref[0, i] = jnp.broadcast_to(page_max, (FOX_HEADS, LANES))
        m = jnp.maximum(m, page_max)
    m_ref[0] = m


def _fox_values_kernel(page_ref, walk_ref, count_ref, s_ref, m_ref, sself_ref, vnew_ref, *refs, pp):
    v_refs = refs[:pp]
    o_ref, l_sc, acc_sc = refs[pp:]
    b = pl.program_id(0)
    step = pl.program_id(1)
    page = v_refs[0].shape[3]
    m = m_ref[0]

    @pl.when(step == 0)
    def _():
        p_self = jnp.exp(sself_ref[0] - m)
        l_sc[...] = p_self
        lane = lax.broadcasted_iota(jnp.int32, (FOX_DH, page), 1)
        for hp in range(FOX_HEADS // 2):
            vc = _as_column(vnew_ref[0, :, hp * LANES:(hp + 1) * LANES])
            for hl in range(2):
                h = 2 * hp + hl
                acc_sc[h] = jnp.where(lane == 0, vc[hl * FOX_DH:(hl + 1) * FOX_DH] * p_self[h:h + 1, :], 0.0)

    n_live = count_ref[b]
    for i in range(pp):
        slot = step * pp + i

        @pl.when(slot < n_live)
        def _():
            start = pl.multiple_of(walk_ref[b, slot] * page, page)
            p = jnp.exp(s_ref[0, :, pl.ds(start, page)] - m)
            l_sc[...] = l_sc[...] + jnp.sum(p, axis=1, keepdims=True)
            for h in range(FOX_HEADS):
                acc_sc[h] = acc_sc[h] + v_refs[i][0, h] * p[h:h + 1, :]

    @pl.when(step == pl.num_programs(1) - 1)
    def _():
        o_ref[0] = (jnp.sum(acc_sc[...], axis=2) / l_sc[:, :1]).astype(o_ref.dtype)


def _fox_decode(q, k_new, v_new, lf_new, cache_k, cache_v, cache_logf, page_table):
    nb, n_pages = page_table.shape
    page = cache_k.shape[1]
    assert page == LANES, "one cache page fills the lane axis"
    pp = min(PAGES_PER_STEP, n_pages)
    n_steps = n_pages // pp
    ck = jnp.transpose(cache_k, (0, 2, 3, 1))
    cv = jnp.transpose(cache_v, (0, 2, 3, 1))
    clf = jnp.transpose(cache_logf, (0, 2, 1))
    as_row = lambda a: a.reshape(nb, 1, FOX_HD)
    lf_row = jnp.pad(lf_new, ((0, 0), (0, LANES - FOX_HEADS))).reshape(nb, 1, LANES)
    heads_lanes = jax.ShapeDtypeStruct((nb, FOX_HEADS, LANES), F32)

    def k_page(i):
        return pl.BlockSpec((1, FOX_HEADS, FOX_DH, page),
                            lambda b, s, pt: (pt[b, n_pages - 1 - (s * pp + i)], 0, 0, 0))

    def lf_page(i):
        return pl.BlockSpec((1, FOX_HEADS, page), lambda b, s, pt: (pt[b, n_pages - 1 - (s * pp + i)], 0, 0))

    row = pl.BlockSpec((1, 1, FOX_HD), lambda b, s, *_: (b, 0, 0))
    per_seq = pl.BlockSpec((1, FOX_HEADS, LANES), lambda b, s, *_: (b, 0, 0))
    scores, page_max, m, s_self = pl.pallas_call(
        functools.partial(_fox_scores_kernel, pp=pp),
        grid_spec=pltpu.PrefetchScalarGridSpec(
            num_scalar_prefetch=1, grid=(nb, n_steps),
            in_specs=[row, row, pl.BlockSpec((1, 1, LANES), lambda b, s, pt: (b, 0, 0))]
            + [k_page(i) for i in range(pp)] + [lf_page(i) for i in range(pp)],
            out_specs=[pl.BlockSpec((1, FOX_HEADS, pp * page), lambda b, s, pt: (b, 0, s)),
                       pl.BlockSpec((1, pp, FOX_HEADS, LANES), lambda b, s, pt: (b, s, 0, 0)), per_seq, per_seq],
            scratch_shapes=[pltpu.VMEM((FOX_HEADS, FOX_DH, page), F32), pltpu.VMEM((FOX_HEADS, page), F32)]),
        out_shape=[jax.ShapeDtypeStruct((nb, FOX_HEADS, n_pages * page), F32),
                   jax.ShapeDtypeStruct((nb, n_pages, FOX_HEADS, LANES), F32), heads_lanes, heads_lanes],
        compiler_params=_cparams(("arbitrary", "arbitrary")),
        name="fox_decode_scores",
    )(page_table, as_row(q), as_row(k_new), lf_row, *([ck] * pp), *([clf] * pp))

    live = jnp.any(page_max[:, :, :, 0] - m[:, None, :, 0] >= -EXP_UNDERFLOW, axis=2)
    count = jnp.sum(live, axis=1).astype(jnp.int32)
    walk = jnp.argsort(jnp.logical_not(live), axis=1, stable=True).astype(jnp.int32)
    pages = jnp.take_along_axis(page_table[:, ::-1], walk, axis=1)
    pp = min(VALUE_PAGES_PER_STEP, n_pages)
    n_steps = n_pages // pp
    is_live = (jnp.arange(n_pages, dtype=jnp.int32)[None, :] < count[:, None]).reshape(nb * n_steps, pp)
    flat = pages.reshape(nb * n_steps, pp)
    last_live = lax.cummax(jnp.where(is_live, jnp.arange(nb * n_steps, dtype=jnp.int32)[:, None], 0), axis=0)
    pages = jnp.take_along_axis(flat, last_live, axis=0).reshape(nb, n_pages)

    def v_page(i):
        return pl.BlockSpec((1, FOX_HEADS, FOX_DH, page), lambda b, s, pg, wk, ct: (pg[b, s * pp + i], 0, 0, 0))

    out = pl.pallas_call(
        functools.partial(_fox_values_kernel, pp=pp),
        grid_spec=pltpu.PrefetchScalarGridSpec(
            num_scalar_prefetch=3, grid=(nb, n_steps),
            in_specs=[pl.BlockSpec((1, FOX_HEADS, n_pages * page), lambda b, s, *_: (b, 0, 0)), per_seq, per_seq, row]
            + [v_page(i) for i in range(pp)],
            out_specs=pl.BlockSpec((1, FOX_HEADS, FOX_DH), lambda b, s, *_: (b, 0, 0)),
            scratch_shapes=[pltpu.VMEM((FOX_HEADS, LANES), F32), pltpu.VMEM((FOX_HEADS, FOX_DH, page), F32)]),
        out_shape=jax.ShapeDtypeStruct((nb, FOX_HEADS, FOX_DH), BF16),
        compiler_params=_cparams(("arbitrary", "arbitrary")),
        name="fox_decode_values",
    )(pages, walk, count, scores, m, s_self, as_row(v_new), *([cv] * pp))
    return out.reshape(nb, FOX_HD)


def _run_group(x, p, cos, sin, w, bm, seq_len, sample):
    rows = x.shape[0]
    qkv_dtype = BF16 if sample is None else F32
    q, k, v, gate = _ret_in(x, w["g_pre_mix"][0], w["w_in"], cos, sin, bm, qkv_dtype)
    if sample is None:
        batch = rows // seq_len
        o, ret_state = _retention_prompt(q, k, v, gate, w["log_gamma"], batch, seq_len)
    else:
        o, ret_state = _retention_step(sample["state_ret"], q, k, v, gate, w["log_gamma"])

    conv_out = []
    ffn_args = lambda i: (w["g_post_mix"][i], w["g_pre_ffn"][i], w["w_up"], w["w_dw"], w["b_dw"],
                          w["w_down"], w["g_post_ffn"][i], p, w["w_ple"], w["w_gate"], bm, i)
    if sample is None:
        x, tail = _ffn(x, o, w["w_out_ret"], *ffn_args(0), seq_len=seq_len)
        conv_out.append(tail[:, SUBLANES - 2:, :])
    else:
        buf = sample["state_conv"][0]
        x, a = _ffn(x, o, w["w_out_ret"], *ffn_args(0), prev=(buf[:, 0], buf[:, 1]))
        conv_out.append(jnp.stack([buf[:, 1], a], axis=1))

    k_new, v_new, lf_new, k_bf, v_bf, q_bf, k_n2, q_n2 = _kvq(
        x, w["g_kv"], w["w_kv"], w["w_f"], w["b_f"], w["g_pre_mix"][1], w["w_q"], bm, per_head_out=sample is None)
    if sample is None:
        cq, ck, c_first, c_last = _cumsum_time(lf_new, batch, seq_len)
        per_seq_max = lambda n2: jnp.max(n2.reshape(batch, -1, SUBLANES, LANES)[:, :, 0, :FOX_HEADS], axis=1)
        qk_bound = jnp.sqrt(per_seq_max(k_n2) * per_seq_max(q_n2)) * NORM_PAD
        edges = lambda e: e.reshape(batch, -1, FOX_HEADS)
        att = _fox_prompt(q_bf, k_bf, v_bf, cq, ck, edges(c_first), edges(c_last), qk_bound, batch, seq_len)
    else:
        att = _fox_decode(q_bf.astype(F32), k_new, v_new, lf_new, sample["cache_k"], sample["cache_v"],
                          sample["cache_logf"], sample["page_table"])

    if sample is None:
        x, tail = _ffn(x, att, w["w_out_fox"], *ffn_args(1), seq_len=seq_len)
        conv_out.append(tail[:, SUBLANES - 2:, :])
    else:
        buf = sample["state_conv"][1]
        x, a = _ffn(x, att, w["w_out_fox"], *ffn_args(1), prev=(buf[:, 0], buf[:, 1]))
        conv_out.append(jnp.stack([buf[:, 1], a], axis=1))
    return x, ret_state, jnp.stack(conv_out), k_new, v_new, lf_new


def kernel(x_prompt, x_sample, state_ret, state_conv, cache_k, cache_v, cache_logf, page_table, p_prompt, p_sample,
           norm_pre_mix, norm_post_mix, norm_pre_ffn, norm_post_ffn, w_in_ret, w_out_ret, norm_kv, w_kvf, b_f,
           w_q_fox, w_out_fox, w_up, w_dw, b_dw, w_down, w_ple, w_ple_gate):
    bp, tp, _ = x_prompt.shape
    nb, ts, _ = x_sample.shape
    assert ts == 1, "the sample group is one new token per sequence"
    past_len = page_table.shape[1] * cache_k.shape[1]
    depth = w_up.shape[0]
    row_vec = lambda a: a.reshape(a.shape[0], 1, a.shape[1])
    w = {
        "g_pre_mix": row_vec(norm_pre_mix), "g_post_mix": row_vec(norm_post_mix),
        "g_pre_ffn": row_vec(norm_pre_ffn), "g_post_ffn": row_vec(norm_post_ffn),
        "g_kv": norm_kv.reshape(1, D_MODEL),
        "w_in": w_in_ret[0].astype(BF16), "w_out_ret": w_out_ret[0].astype(BF16),
        "w_kv": w_kvf[:, :2 * FOX_HD].astype(BF16),
        "w_f": jnp.pad(w_kvf[:, 2 * FOX_HD:], ((0, 0), (0, LANES - FOX_HEADS))).astype(BF16),
        "b_f": jnp.pad(b_f, (0, LANES - FOX_HEADS)).reshape(1, LANES),
        "w_q": w_q_fox[0].astype(BF16), "w_out_fox": w_out_fox[0].astype(BF16),
        "w_up": w_up.astype(BF16), "w_dw": w_dw, "b_dw": row_vec(b_dw), "w_down": w_down.astype(BF16),
        "w_ple": w_ple.astype(BF16), "w_gate": w_ple_gate.astype(BF16),
        "log_gamma": jnp.log1p(-jnp.exp2(-5.0 - jnp.arange(RET_HEADS, dtype=F32))),
    }
    bm = min(ROW_BLOCK, tp)
    cos_p, sin_p = _rope_table(tp, 0)
    y_p, ret_p, conv_p, k_p, v_p, lf_p = _run_group(
        x_prompt.reshape(bp * tp, D_MODEL), p_prompt.reshape(depth, bp * tp, PLE_DIM), cos_p, sin_p, w, bm, tp, None)
    cos_s, sin_s = _rope_table(SUBLANES, past_len)
    cos_s = jnp.broadcast_to(cos_s[:1], (nb, ROPE_HALF))
    sin_s = jnp.broadcast_to(sin_s[:1], (nb, ROPE_HALF))
    sample = {"state_ret": state_ret[0], "state_conv": state_conv, "cache_k": cache_k, "cache_v": cache_v,
              "cache_logf": cache_logf, "page_table": page_table}
    y_s, ret_s, conv_s, k_s, v_s, lf_s = _run_group(
        x_sample.reshape(nb, D_MODEL), p_sample.reshape(depth, nb, PLE_DIM), cos_s, sin_s, w, nb, 1, sample)
    return (y_p.reshape(bp, tp, D_MODEL), y_s.reshape(nb, 1, D_MODEL),
            ret_p[None], ret_s[None], conv_p, conv_s,
            k_p.reshape(bp, tp, FOX_HEADS, FOX_DH), k_s.reshape(nb, 1, FOX_HEADS, FOX_DH),
            v_p.reshape(bp, tp, FOX_HEADS, FOX_DH), v_s.reshape(nb, 1, FOX_HEADS, FOX_DH),
            lf_p.reshape(bp, tp, FOX_HEADS), lf_s.reshape(nb, 1, FOX_HEADS))
```

```python
import functools
import math

import jax
import jax.numpy as jnp
from jax import lax
from jax.experimental import pallas as pl
from jax.experimental.pallas import tpu as pltpu

F32 = jnp.float32
BF16 = jnp.bfloat16

D_MODEL = 1024
RET_HEADS = 4
RET_DK = 256
RET_DV = 512
RET_QK = RET_HEADS * RET_DK
RET_VD = RET_HEADS * RET_DV
ROPE_BASE = 10000.0
ROPE_HALF = RET_DK // 2
FOX_HEADS = 16
FOX_DH = 64
FOX_HD = FOX_HEADS * FOX_DH
D_FF = 2816
PLE_DIM = 256
NORM_EPS = 1e-6

LANES = 128
SUBLANES = 8
VMEM_LIMIT = 56 * 1024 * 1024

ROW_BLOCK = 512
RET_CHUNK = 256
FFN_CHUNK = 256
ATT_BLOCK = 512
ATT_HEADS = 4
PAGES_PER_STEP = 32
VALUE_PAGES_PER_STEP = 64
BIAS_ONES = 48
EXP_UNDERFLOW = 110.0
NORM_PAD = 1.05


def _cparams(sem):
    return pltpu.CompilerParams(dimension_semantics=sem, vmem_limit_bytes=VMEM_LIMIT)


def _resident(shape):
    return pl.BlockSpec(shape, lambda *_: (0,) * len(shape), pipeline_mode=pl.Buffered(1))


def _dot(a, b):
    return jnp.dot(a, b, preferred_element_type=F32)


def _dot_nt(a, b):
    return lax.dot_general(a, b, (((1,), (1,)), ((), ())), preferred_element_type=F32)


def _dot_tn(a, b):
    return lax.dot_general(a, b, (((0,), (0,)), ((), ())), preferred_element_type=F32)


def _rms(x, g):
    return x * lax.rsqrt(jnp.mean(x * x, axis=-1, keepdims=True) + NORM_EPS) * g


def _sigmoid(x):
    return 1.0 / (1.0 + jnp.exp(-x))


def _gelu_tanh(x):
    return x * (0.5 * (1.0 + jnp.tanh(math.sqrt(2.0 / math.pi) * (x + 0.044715 * (x * x * x)))))


def _log_sigmoid(x):
    z = -x
    return -(jnp.maximum(z, 0.0) + jnp.log1p(jnp.exp(-jnp.abs(z))))


def _split3(x):
    hi = x.astype(BF16)
    r1 = x - hi.astype(F32)
    mid = r1.astype(BF16)
    lo = (r1 - mid.astype(F32)).astype(BF16)
    return hi, mid, lo


def _rope_kernel(cos_ref, sin_ref, *, base_pos):
    n = cos_ref.shape[0]
    i = pl.program_id(0)
    lane = lax.broadcasted_iota(jnp.int32, (1, ROPE_HALF), 1).astype(F32)
    inv = jnp.power(jnp.full((1, ROPE_HALF), ROPE_BASE, F32), -(lane / ROPE_HALF))
    pos = (lax.broadcasted_iota(jnp.int32, (n, ROPE_HALF), 0) + (i * n + base_pos)).astype(F32)
    ang = pos * inv
    cos_ref[...] = jnp.cos(ang)
    sin_ref[...] = jnp.sin(ang)


def _rope_table(n_rows, base_pos):
    bn = min(n_rows, 1024)
    spec = pl.BlockSpec((bn, ROPE_HALF), lambda i: (i, 0))
    return pl.pallas_call(
        functools.partial(_rope_kernel, base_pos=base_pos),
        grid=(n_rows // bn,),
        out_specs=[spec, spec],
        out_shape=[jax.ShapeDtypeStruct((n_rows, ROPE_HALF), F32)] * 2,
        compiler_params=_cparams(("arbitrary",)),
        name="rope_table",
    )()


def _ret_in_kernel(x_ref, g_ref, w_ref, cos_ref, sin_ref, q_ref, k_ref, v_ref, gate_ref):
    xn = _rms(x_ref[...], g_ref[...]).astype(BF16)
    cos = cos_ref[...]
    sin = sin_ref[...]
    for out_ref, base, scale in ((q_ref, 0, 1.0), (k_ref, RET_QK, RET_DK ** -0.5)):
        for h in range(RET_HEADS):
            lo = h * RET_DK
            u = _dot(xn, w_ref[:, base + lo:base + lo + RET_DK])
            x1, x2 = u[:, :ROPE_HALF], u[:, ROPE_HALF:]
            out_ref[:, lo:lo + ROPE_HALF] = ((x1 * cos - x2 * sin) * scale).astype(out_ref.dtype)
            out_ref[:, lo + ROPE_HALF:lo + RET_DK] = ((x1 * sin + x2 * cos) * scale).astype(out_ref.dtype)
    for h in range(RET_HEADS):
        lo = h * RET_DV
        v_ref[:, lo:lo + RET_DV] = _dot(xn, w_ref[:, 2 * RET_QK + lo:2 * RET_QK + lo + RET_DV]).astype(v_ref.dtype)
        gate_ref[:, lo:lo + RET_DV] = _dot(xn, w_ref[:, 2 * RET_QK + RET_VD + lo:2 * RET_QK + RET_VD + lo + RET_DV])


def _ret_in(x, g, w_bf, cos, sin, bm, qkv_dtype):
    rows = x.shape[0]
    n_tab = cos.shape[0] // bm
    row = lambda i: (i, 0)
    tab = pl.BlockSpec((bm, ROPE_HALF), lambda i: (i % n_tab, 0))
    return pl.pallas_call(
        _ret_in_kernel,
        grid=(rows // bm,),
        in_specs=[pl.BlockSpec((bm, D_MODEL), row), _resident((1, D_MODEL)), _resident(w_bf.shape), tab, tab],
        out_specs=[pl.BlockSpec((bm, RET_QK), row), pl.BlockSpec((bm, RET_QK), row),
                   pl.BlockSpec((bm, RET_VD), row), pl.BlockSpec((bm, RET_VD), row)],
        out_shape=[jax.ShapeDtypeStruct((rows, RET_QK), qkv_dtype), jax.ShapeDtypeStruct((rows, RET_QK), qkv_dtype),
                   jax.ShapeDtypeStruct((rows, RET_VD), qkv_dtype), jax.ShapeDtypeStruct((rows, RET_VD), F32)],
        compiler_params=_cparams(("arbitrary",)),
        name="ret_in",
    )(x, g, w_bf, cos, sin)


def _ret_chunk_kernel(lg_ref, q_ref, k_ref, v_ref, gate_ref, o_ref, s_ref, decay_sc):
    c = pl.program_id(1)
    C = q_ref.shape[0]

    @pl.when(jnp.logical_and(pl.program_id(0) == 0, c == 0))
    def _():
        rel = lax.broadcasted_iota(jnp.int32, (C, C), 0) - lax.broadcasted_iota(jnp.int32, (C, C), 1)
        dist = jnp.maximum(rel, 0).astype(F32)
        for h in range(RET_HEADS):
            decay_sc[h] = jnp.where(rel >= 0, jnp.exp(dist * lg_ref[h]), 0.0)

    @pl.when(c == 0)
    def _():
        s_ref[...] = jnp.zeros_like(s_ref)

    n = lax.broadcasted_iota(jnp.int32, (C, 1), 0).astype(F32)
    for h in range(RET_HEADS):
        lg = lg_ref[h]
        q = q_ref[:, h * RET_DK:(h + 1) * RET_DK]
        k = k_ref[:, h * RET_DK:(h + 1) * RET_DK]
        v = v_ref[:, h * RET_DV:(h + 1) * RET_DV]
        scores = _dot_nt(q, k) * decay_sc[h]
        inner = _dot(scores.astype(BF16), v)
        state = s_ref[0, h]
        cross = _dot(q, state.astype(BF16)) * jnp.exp((n + 1.0) * lg)
        kd = (k.astype(F32) * jnp.exp((C - 1.0 - n) * lg)).astype(BF16)
        chunk_decay = jnp.exp(jnp.full((1, RET_DV), float(C), F32) * lg)
        s_ref[0, h] = state * chunk_decay + _dot_tn(kd, v)
        o = inner + cross
        o = o * lax.rsqrt(jnp.mean(o * o, axis=-1, keepdims=True) + NORM_EPS)
        gate = gate_ref[:, h * RET_DV:(h + 1) * RET_DV]
        o_ref[:, h * RET_DV:(h + 1) * RET_DV] = (o * (gate * _sigmoid(gate))).astype(o_ref.dtype)


def _retention_prompt(q, k, v, gate, log_gamma, batch, seq):
    C = min(RET_CHUNK, seq)
    nc = seq // C
    blk = lambda b, c, lg: (b * nc + c, 0)
    return pl.pallas_call(
        _ret_chunk_kernel,
        grid_spec=pltpu.PrefetchScalarGridSpec(
            num_scalar_prefetch=1,
            grid=(batch, nc),
            in_specs=[pl.BlockSpec((C, RET_QK), blk), pl.BlockSpec((C, RET_QK), blk),
                      pl.BlockSpec((C, RET_VD), blk), pl.BlockSpec((C, RET_VD), blk)],
            out_specs=[pl.BlockSpec((C, RET_VD), blk),
                       pl.BlockSpec((1, RET_HEADS, RET_DK, RET_DV), lambda b, c, lg: (b, 0, 0, 0))],
            scratch_shapes=[pltpu.VMEM((RET_HEADS, C, C), F32)],
        ),
        out_shape=[jax.ShapeDtypeStruct((batch * seq, RET_VD), BF16),
                   jax.ShapeDtypeStruct((batch, RET_HEADS, RET_DK, RET_DV), F32)],
        compiler_params=_cparams(("arbitrary", "arbitrary")),
        name="retention_chunks",
    )(log_gamma, q, k, v, gate)


def _as_column(row):
    return jnp.transpose(jnp.broadcast_to(row, (LANES, LANES)))


def _ret_step_kernel(lg_ref, s_ref, q_ref, k_ref, v_ref, gate_ref, o_ref, so_ref):
    wide = lambda x: jnp.concatenate([x] * (RET_DV // LANES), axis=1)
    for h in range(RET_HEADS):
        gamma = jnp.exp(jnp.full((1, RET_DV), 1.0, F32) * lg_ref[h])
        state = s_ref[0, h]
        column = lambda ref: jnp.concatenate(
            [_as_column(ref[0, :, h * RET_DK + c * LANES:h * RET_DK + (c + 1) * LANES])
             for c in range(RET_DK // LANES)], axis=0)
        qc = column(q_ref)
        kc = column(k_ref)
        vr = v_ref[0, :, h * RET_DV:(h + 1) * RET_DV]
        qk = jnp.sum(qc * kc, axis=0, keepdims=True)
        qs = jnp.sum(wide(qc) * state, axis=0, keepdims=True)
        o = wide(qk) * vr + qs * gamma
        so_ref[0, h] = state * gamma + wide(kc) * vr
        o = o * lax.rsqrt(jnp.mean(o * o, axis=-1, keepdims=True) + NORM_EPS)
        gate = gate_ref[0, :, h * RET_DV:(h + 1) * RET_DV]
        o_ref[0, :, h * RET_DV:(h + 1) * RET_DV] = (o * (gate * _sigmoid(gate))).astype(o_ref.dtype)


def _retention_step(state, q, k, v, gate, log_gamma):
    nb = state.shape[0]
    st = pl.BlockSpec((1, RET_HEADS, RET_DK, RET_DV), lambda b, lg: (b, 0, 0, 0))
    rowk = pl.BlockSpec((1, 1, RET_QK), lambda b, lg: (b, 0, 0))
    rowv = pl.BlockSpec((1, 1, RET_VD), lambda b, lg: (b, 0, 0))
    o, s_new = pl.pallas_call(
        _ret_step_kernel,
        grid_spec=pltpu.PrefetchScalarGridSpec(
            num_scalar_prefetch=1, grid=(nb,),
            in_specs=[st, rowk, rowk, rowv, rowv], out_specs=[rowv, st]),
        out_shape=[jax.ShapeDtypeStruct((nb, 1, RET_VD), BF16), jax.ShapeDtypeStruct(state.shape, F32)],
        compiler_params=_cparams(("arbitrary",)),
        name="retention_step",
    )(log_gamma, state, q.reshape(nb, 1, RET_QK), k.reshape(nb, 1, RET_QK), v.reshape(nb, 1, RET_VD),
      gate.reshape(nb, 1, RET_VD))
    return o.reshape(nb, RET_VD), s_new


def _ffn_kernel(*refs, seq_mode, blocks_per_seq):
    if seq_mode:
        (x_ref, o_ref, wmix_ref, gmix_ref, gpre_ref, wup_ref, wdw_ref, bdw_ref, wdown_ref, gpost_ref, p_ref,
         wple_ref, wgate_ref, y_ref, tail_ref, h_sc, carry_sc) = refs
    else:
        (x_ref, o_ref, wmix_ref, gmix_ref, gpre_ref, wup_ref, wdw_ref, bdw_ref, wdown_ref, gpost_ref, p_ref,
         wple_ref, wgate_ref, prev2_ref, prev1_ref, y_ref, a_ref, h_sc) = refs
    x = x_ref[...] + _rms(_dot(o_ref[...], wmix_ref[...]), gmix_ref[...])
    bm = x.shape[0]
    xn = _rms(x, gpre_ref[...]).astype(BF16)
    if seq_mode:
        @pl.when(pl.program_id(0) % blocks_per_seq == 0)
        def _():
            carry_sc[...] = jnp.zeros_like(carry_sc)
        row = lax.broadcasted_iota(jnp.int32, (bm, FFN_CHUNK), 0)
    for c in range(D_FF // FFN_CHUNK):
        sl = slice(c * FFN_CHUNK, (c + 1) * FFN_CHUNK)
        a = _dot(xn, wup_ref[:, sl])
        b = _dot(xn, wup_ref[:, D_FF + c * FFN_CHUNK:D_FF + (c + 1) * FFN_CHUNK])
        if seq_mode:
            c2 = carry_sc[SUBLANES - 2:SUBLANES - 1, sl]
            c1 = carry_sc[SUBLANES - 1:SUBLANES, sl]
            a1 = jnp.where(row == 0, c1, pltpu.roll(a, 1, 0))
            a2 = jnp.where(row == 0, c2, jnp.where(row == 1, c1, pltpu.roll(a, 2, 0)))
            last = a[bm - SUBLANES:bm, :]
            carry_sc[:, sl] = last
            tail_ref[0, :, sl] = last
        else:
            a1 = prev1_ref[:, sl]
            a2 = prev2_ref[:, sl]
            a_ref[:, sl] = a
        conv = bdw_ref[:, sl] + a2 * wdw_ref[0:1, sl]
        conv = conv + a1 * wdw_ref[1:2, sl]
        conv = conv + a * wdw_ref[2:3, sl]
        h_sc[:, sl] = (_gelu_tanh(conv) * b).astype(BF16)
    f = _dot(h_sc[...], wdown_ref[...])
    x2 = x + _rms(f, gpost_ref[...])
    ple = _dot(p_ref[...].astype(BF16), wple_ref[...])
    gate = _dot(x2.astype(BF16), wgate_ref[...])
    y_ref[...] = x2 + ple * _sigmoid(gate)


def _ffn(x, o_bf, wmix_bf, gmix, gpre, wup_bf, wdw, bdw, wdown_bf, gpost, p, wple_bf, wgate_bf, bm, layer,
         seq_len=None, prev=None):
    rows = x.shape[0]
    row = lambda i: (i, 0)
    seq_mode = prev is None

    def of_layer(stacked):
        shape = stacked.shape[1:]
        return pl.BlockSpec((None,) + shape, lambda *_: (layer,) + (0,) * len(shape), pipeline_mode=pl.Buffered(1))

    in_specs = [pl.BlockSpec((bm, D_MODEL), row), pl.BlockSpec((bm, o_bf.shape[1]), row), _resident(wmix_bf.shape),
                _resident((1, D_MODEL)), _resident((1, D_MODEL)), of_layer(wup_bf),
                of_layer(wdw), of_layer(bdw), of_layer(wdown_bf), _resident((1, D_MODEL)),
                pl.BlockSpec((None, bm, PLE_DIM), lambda i: (layer, i, 0)), of_layer(wple_bf), of_layer(wgate_bf)]
    args = [x, o_bf, wmix_bf, gmix, gpre, wup_bf, wdw, bdw, wdown_bf, gpost, p, wple_bf, wgate_bf]
    scratch = [pltpu.VMEM((bm, D_FF), BF16)]
    if seq_mode:
        bps = seq_len // bm
        out_specs = [pl.BlockSpec((bm, D_MODEL), row), pl.BlockSpec((1, SUBLANES, D_FF), lambda i: (i // bps, 0, 0))]
        out_shape = [jax.ShapeDtypeStruct((rows, D_MODEL), F32),
                     jax.ShapeDtypeStruct((rows // seq_len, SUBLANES, D_FF), F32)]
        scratch.append(pltpu.VMEM((SUBLANES, D_FF), F32))
    else:
        bps = 1
        in_specs += [pl.BlockSpec((bm, D_FF), row), pl.BlockSpec((bm, D_FF), row)]
        args += list(prev)
        out_specs = [pl.BlockSpec((bm, D_MODEL), row), pl.BlockSpec((bm, D_FF), row)]
        out_shape = [jax.ShapeDtypeStruct((rows, D_MODEL), F32), jax.ShapeDtypeStruct((rows, D_FF), F32)]
    return pl.pallas_call(
        functools.partial(_ffn_kernel, seq_mode=seq_mode, blocks_per_seq=bps),
        grid=(rows // bm,),
        in_specs=in_specs, out_specs=out_specs, out_shape=out_shape, scratch_shapes=scratch,
        compiler_params=_cparams(("arbitrary",)),
        name="conv_ffn",
    )(*args)


def _kvq_kernel(x_ref, gkv_ref, wkv_ref, wf_ref, bf_ref, gq_ref, wq_ref,
                k_ref, v_ref, lf_ref, kb_ref, vb_ref, qb_ref, kn_ref, qn_ref, *, per_head_out):
    x = x_ref[...]
    xn = _rms(x, gkv_ref[...]).astype(BF16)
    k = _dot(xn, wkv_ref[:, :FOX_HD])
    v = _dot(xn, wkv_ref[:, FOX_HD:])
    if per_head_out:
        k_ref[...] = k.reshape(k.shape[0], FOX_HEADS, FOX_DH)
        v_ref[...] = v.reshape(v.shape[0], FOX_HEADS, FOX_DH)
    else:
        k_ref[...] = k
        v_ref[...] = v
    kb = k.astype(BF16)
    kb_ref[...] = kb
    vb_ref[...] = v.astype(BF16)
    f = _dot(xn, wf_ref[...]) + bf_ref[...]
    lf_ref[...] = _log_sigmoid(f)[:, :FOX_HEADS]
    xq = _rms(x, gq_ref[...]).astype(BF16)
    qb = (_dot(xq, wq_ref[...]) * FOX_DH ** -0.5).astype(BF16)
    qb_ref[...] = qb
    head_sum = (lax.broadcasted_iota(jnp.int32, (FOX_HD, LANES), 0) // FOX_DH ==
                lax.broadcasted_iota(jnp.int32, (FOX_HD, LANES), 1))
    head_sum = jnp.where(head_sum, 1.0, 0.0).astype(BF16)
    for src, dst in ((kb, kn_ref), (qb, qn_ref)):
        sf = src.astype(F32)
        n2 = jnp.max(_dot((sf * sf).astype(BF16), head_sum), axis=0, keepdims=True)
        dst[0] = jnp.broadcast_to(n2, (SUBLANES, LANES))


def _kvq(x, gkv, wkv_bf, wf_bf, bf_pad, gq, wq_bf, bm, per_head_out):
    rows = x.shape[0]
    row = lambda i: (i, 0)
    wide = pl.BlockSpec((bm, FOX_HD), row)
    norm = pl.BlockSpec((1, SUBLANES, LANES), lambda i: (i, 0, 0))
    norm_shape = jax.ShapeDtypeStruct((rows // bm, SUBLANES, LANES), F32)
    if per_head_out:
        kv_spec = pl.BlockSpec((bm, FOX_HEADS, FOX_DH), lambda i: (i, 0, 0))
        kv_shape = jax.ShapeDtypeStruct((rows, FOX_HEADS, FOX_DH), F32)
    else:
        kv_spec, kv_shape = wide, jax.ShapeDtypeStruct((rows, FOX_HD), F32)
    return pl.pallas_call(
        functools.partial(_kvq_kernel, per_head_out=per_head_out),
        grid=(rows // bm,),
        in_specs=[pl.BlockSpec((bm, D_MODEL), row), _resident((1, D_MODEL)), _resident(wkv_bf.shape),
                  _resident(wf_bf.shape), _resident((1, LANES)), _resident((1, D_MODEL)), _resident(wq_bf.shape)],
        out_specs=[kv_spec, kv_spec, pl.BlockSpec((bm, FOX_HEADS), row), wide, wide, wide, norm, norm],
        out_shape=[kv_shape, kv_shape,
                   jax.ShapeDtypeStruct((rows, FOX_HEADS), F32), jax.ShapeDtypeStruct((rows, FOX_HD), BF16),
                   jax.ShapeDtypeStruct((rows, FOX_HD), BF16), jax.ShapeDtypeStruct((rows, FOX_HD), BF16),
                   norm_shape, norm_shape],
        compiler_params=_cparams(("arbitrary",)),
        name="kv_q_proj",
    )(x, gkv, wkv_bf, wf_bf, bf_pad, gq, wq_bf)


def _cumsum_kernel(lf_ref, cq_ref, ck_ref, first_ref, last_ref, carry_sc):
    bl = lf_ref.shape[0]

    @pl.when(pl.program_id(1) == 0)
    def _():
        carry_sc[...] = jnp.zeros_like(carry_sc)

    lf = lf_ref[...]
    tri = (lax.broadcasted_iota(jnp.int32, (bl, bl), 1) <= lax.broadcasted_iota(jnp.int32, (bl, bl), 0))
    tri = jnp.where(tri, 1.0, 0.0).astype(BF16)
    hi, mid, lo = _split3(lf)
    c = (_dot(tri, lo) + _dot(tri, mid)) + _dot(tri, hi) + carry_sc[...]
    carry_sc[...] = c[bl - 1:bl, :]
    first_ref[0] = c[0:1, :]
    last_ref[0] = c[bl - 1:bl, :]

    lane_of_head = lax.broadcasted_iota(jnp.int32, (FOX_HEADS, LANES), 1) - 3 * lax.broadcasted_iota(
        jnp.int32, (FOX_HEADS, LANES), 0)
    lane = lax.broadcasted_iota(jnp.int32, (1, LANES), 1)
    cq = jnp.where(lane < BIAS_ONES, 1.0, 0.0)
    ck = jnp.where(jnp.logical_and(lane >= BIAS_ONES, lane < 2 * BIAS_ONES), 1.0, 0.0)
    for piece, part in enumerate(_split3(c)):
        cq = cq + _dot(part, jnp.where(lane_of_head == BIAS_ONES + piece, 1.0, 0.0).astype(BF16))
        ck = ck + _dot(part, jnp.where(lane_of_head == piece, -1.0, 0.0).astype(BF16))
    cq_ref[...] = cq.astype(BF16)
    ck_ref[...] = ck.astype(BF16)


def _cumsum_time(lf, batch, seq):
    bl = min(ATT_BLOCK, seq)
    nb = seq // bl
    blk = lambda b, i: (b * nb + i, 0)
    wide = pl.BlockSpec((bl, LANES), blk)
    edge = pl.BlockSpec((1, 1, FOX_HEADS), lambda b, i: (b * nb + i, 0, 0))
    return pl.pallas_call(
        _cumsum_kernel,
        grid=(batch, nb),
        in_specs=[pl.BlockSpec((bl, FOX_HEADS), blk)], out_specs=[wide, wide, edge, edge],
        out_shape=[jax.ShapeDtypeStruct((lf.shape[0], LANES), BF16)] * 2
        + [jax.ShapeDtypeStruct((batch * nb, 1, FOX_HEADS), F32)] * 2,
        scratch_shapes=[pltpu.VMEM((1, FOX_HEADS), F32)],
        compiler_params=_cparams(("arbitrary", "arbitrary")),
        name="logf_cumsum",
    )(lf)


def _fox_prompt_kernel(cfirst_ref, clast_ref, qkb_ref, q_ref, k_ref, v_ref, cq_ref, ck_ref, o_ref, m_sc, acc_sc):
    b = pl.program_id(0)
    hg = pl.program_id(1)
    i = pl.program_id(2)

    def dead(j):
        worst = None
        for hh in range(ATT_HEADS):
            h = ATT_HEADS * hg + hh
            gap = cfirst_ref[b, i, h] - clast_ref[b, j, h] + 2.0 * qkb_ref[b, h]
            worst = gap if worst is None else jnp.maximum(worst, gap)
        return worst < -EXP_UNDERFLOW

    j_lo = lax.while_loop(lambda j: jnp.logical_and(j < i, dead(j)), lambda j: j + 1, jnp.int32(0))
    bq = q_ref.shape[0]
    pair = 2 * FOX_DH
    pair_lanes = lambda ref, rows, p: ref[rows, p * pair:(p + 1) * pair]
    low = lax.broadcasted_iota(jnp.int32, (1, pair), 1) < FOX_DH
    causal = (lax.broadcasted_iota(jnp.int32, (bq, bq), 1) <= lax.broadcasted_iota(jnp.int32, (bq, bq), 0))
    ones = jnp.ones((bq, pair), BF16)
    lanes = lambda x, n: jnp.concatenate([x] * n, axis=1)
    bias_lane = lax.broadcasted_iota(jnp.int32, (1, LANES), 1)
    cq = cq_ref[...]
    q_aug = []
    for hh in range(ATT_HEADS):
        q = pair_lanes(q_ref, slice(None), hh // 2)
        qm = jnp.where(low if hh % 2 == 0 else jnp.logical_not(low), q, jnp.zeros_like(q))
        first = 3 * (ATT_HEADS * hg + hh)
        own = jnp.logical_or(jnp.logical_and(bias_lane >= first, bias_lane < first + 3),
                             jnp.logical_and(bias_lane >= first + BIAS_ONES, bias_lane < first + BIAS_ONES + 3))
        q_aug.append(jnp.concatenate([qm, jnp.where(own, cq, jnp.zeros_like(cq))], axis=1))

    def block(j, masked):
        rows = pl.ds(pl.multiple_of(j * bq, bq), bq)
        ck = ck_ref[rows, :]
        k_aug = [jnp.concatenate([pair_lanes(k_ref, rows, p), ck], axis=1) for p in range(ATT_HEADS // 2)]
        v_aug = [jnp.concatenate([pair_lanes(v_ref, rows, p), ones], axis=1) for p in range(ATT_HEADS // 2)]
        for hh in range(ATT_HEADS):
            s = _dot_nt(q_aug[hh], k_aug[hh // 2])
            if masked:
                s = jnp.where(causal, s, -jnp.inf)
                m_next = jnp.broadcast_to(jnp.max(s, axis=1, keepdims=True), (bq, LANES))
                p = jnp.exp(s - lanes(m_next, bq // LANES))
                acc_sc[hh] = _dot(p.astype(BF16), v_aug[hh // 2])
                m_sc[hh] = m_next
            else:
                m_prev = m_sc[hh]
                m_next = jnp.maximum(m_prev, jnp.max(s, axis=1, keepdims=True))
                alpha = jnp.exp(m_prev - m_next)
                p = jnp.exp(s - lanes(m_next, bq // LANES))
                acc_sc[hh] = lanes(alpha, 2) * acc_sc[hh] + _dot(p.astype(BF16), v_aug[hh // 2])
                m_sc[hh] = m_next

    n_live = i - j_lo + 1

    @pl.when(n_live == 1)
    def _():
        block(i, True)

    @pl.when(n_live == 2)
    def _():
        block(i, True)
        block(i - 1, False)

    @pl.when(n_live >= 3)
    def _():
        block(i, True)
        block(i - 1, False)
        block(i - 2, False)

    def body(t, carry):
        block(i - 3 - 2 * t, False)
        block(i - 4 - 2 * t, False)
        return carry

    rest = jnp.maximum(n_live - 3, 0)
    lax.fori_loop(0, rest // 2, body, 0)

    @pl.when(rest % 2 == 1)
    def _():
        block(j_lo, False)

    for p in range(ATT_HEADS // 2):
        o0 = acc_sc[2 * p, :, :pair] / acc_sc[2 * p, :, pair:]
        o1 = acc_sc[2 * p + 1, :, :pair] / acc_sc[2 * p + 1, :, pair:]
        o_ref[:, p * pair:(p + 1) * pair] = jnp.where(low, o0, o1).astype(o_ref.dtype)


def _fox_prompt(q_bf, k_bf, v_bf, cq, ck, c_first, c_last, qk_bound, batch, seq):
    bq = min(ATT_BLOCK, seq)
    nq = seq // bq
    pair = 2 * FOX_DH
    width = ATT_HEADS * FOX_DH
    qblk = pl.BlockSpec((bq, width), lambda b, hg, i, *_: (b * nq + i, hg))
    kvblk = pl.BlockSpec((seq, width), lambda b, hg, i, *_: (b, hg))
    return pl.pallas_call(
        _fox_prompt_kernel,
        grid_spec=pltpu.PrefetchScalarGridSpec(
            num_scalar_prefetch=3,
            grid=(batch, FOX_HEADS // ATT_HEADS, nq),
            in_specs=[qblk, kvblk, kvblk,
                      pl.BlockSpec((bq, LANES), lambda b, hg, i, *_: (b * nq + i, 0)),
                      pl.BlockSpec((seq, LANES), lambda b, hg, i, *_: (b, 0))],
            out_specs=qblk,
            scratch_shapes=[pltpu.VMEM((ATT_HEADS, bq, LANES), F32), pltpu.VMEM((ATT_HEADS, bq, 2 * pair), F32)]),
        out_shape=jax.ShapeDtypeStruct((batch * seq, FOX_HD), BF16),
        compiler_params=_cparams(("arbitrary", "arbitrary", "arbitrary")),
        name="fox_attention_prompt",
    )(c_first, c_last, qk_bound, q_bf, k_bf, v_bf, cq, ck)


def _fox_scores_kernel(pt_ref, q_ref, knew_ref, lfnew_ref, *refs, pp):
    k_refs, lf_refs = refs[:pp], refs[pp:2 * pp]
    s_ref, pmax_ref, m_ref, sself_ref, q_sc, carry_sc = refs[2 * pp:]
    step = pl.program_id(1)
    page = k_refs[0].shape[3]

    @pl.when(step == 0)
    def _():
        for hp in range(FOX_HEADS // 2):
            qc, kc = [_as_column(r[0, :, hp * LANES:(hp + 1) * LANES]) for r in (q_ref, knew_ref)]
            for hl in range(2):
                h = 2 * hp + hl
                qh = qc[hl * FOX_DH:(hl + 1) * FOX_DH]
                q_sc[h] = qh
                sself_ref[0, h:h + 1, :] = jnp.sum(qh * kc[hl * FOX_DH:(hl + 1) * FOX_DH], axis=0, keepdims=True)
        m_ref[0] = sself_ref[0]
        carry_sc[...] = _as_column(lfnew_ref[0])[:FOX_HEADS]

    later = (lax.broadcasted_iota(jnp.int32, (page, page), 0) > lax.broadcasted_iota(jnp.int32, (page, page), 1))
    later = jnp.where(later, 1.0, 0.0).astype(BF16)
    carry = carry_sc[...]
    biases = []
    for i in range(pp):
        lf = lf_refs[i][0]
        hi, mid, lo = _split3(lf)
        biases.append((_dot(lo, later) + _dot(mid, later)) + _dot(hi, later) + carry)
        carry = carry + jnp.sum(lf, axis=1, keepdims=True)
    carry_sc[...] = carry
    for h in range(FOX_HEADS):
        qh = q_sc[h]
        for i in range(pp):
            s_ref[0, h:h + 1, i * page:(i + 1) * page] = (
                jnp.sum(k_refs[i][0, h] * qh, axis=0, keepdims=True) + biases[i][h:h + 1, :])
    m = m_ref[0]
    for i in range(pp):
        page_max = jnp.max(s_ref[0, :, i * page:(i + 1) * page], axis=1, keepdims=True)
        pmax_ref[0, i] = jnp.broadcast_to(page_max, (FOX_HEADS, LANES))
        m = jnp.maximum(m, page_max)
    m_ref[0] = m


def _fox_values_kernel(page_ref, walk_ref, count_ref, s_ref, m_ref, sself_ref, vnew_ref, *refs, pp):
    v_refs = refs[:pp]
    o_ref, l_sc, acc_sc = refs[pp:]
    b = pl.program_id(0)
    step = pl.program_id(1)
    page = v_refs[0].shape[3]
    m = m_ref[0]

    @pl.when(step == 0)
    def _():
        p_self = jnp.exp(sself_ref[0] - m)
        l_sc[...] = p_self
        lane = lax.broadcasted_iota(jnp.int32, (FOX_DH, page), 1)
        for hp in range(FOX_HEADS // 2):
            vc = _as_column(vnew_ref[0, :, hp * LANES:(hp + 1) * LANES])
            for hl in range(2):
                h = 2 * hp + hl
                acc_sc[h] = jnp.where(lane == 0, vc[hl * FOX_DH:(hl + 1) * FOX_DH] * p_self[h:h + 1, :], 0.0)

    n_live = count_ref[b]
    for i in range(pp):
        slot = step * pp + i

        @pl.when(slot < n_live)
        def _():
            start = pl.multiple_of(walk_ref[b, slot] * page, page)
            p = jnp.exp(s_ref[0, :, pl.ds(start, page)] - m)
            l_sc[...] = l_sc[...] + jnp.sum(p, axis=1, keepdims=True)
            for h in range(FOX_HEADS):
                acc_sc[h] = acc_sc[h] + v_refs[i][0, h] * p[h:h + 1, :]

    @pl.when(step == pl.num_programs(1) - 1)
    def _():
        o_ref[0] = (jnp.sum(acc_sc[...], axis=2) / l_sc[:, :1]).astype(o_ref.dtype)


def _fox_decode(q, k_new, v_new, lf_new, cache_k, cache_v, cache_logf, page_table):
    nb, n_pages = page_table.shape
    page = cache_k.shape[1]
    assert page == LANES, "one cache page fills the lane axis"
    pp = min(PAGES_PER_STEP, n_pages)
    n_steps = n_pages // pp
    ck = jnp.transpose(cache_k, (0, 2, 3, 1))
    cv = jnp.transpose(cache_v, (0, 2, 3, 1))
    clf = jnp.transpose(cache_logf, (0, 2, 1))
    as_row = lambda a: a.reshape(nb, 1, FOX_HD)
    lf_row = jnp.pad(lf_new, ((0, 0), (0, LANES - FOX_HEADS))).reshape(nb, 1, LANES)
    heads_lanes = jax.ShapeDtypeStruct((nb, FOX_HEADS, LANES), F32)

    def k_page(i):
        return pl.BlockSpec((1, FOX_HEADS, FOX_DH, page),
                            lambda b, s, pt: (pt[b, n_pages - 1 - (s * pp + i)], 0, 0, 0))

    def lf_page(i):
        return pl.BlockSpec((1, FOX_HEADS, page), lambda b, s, pt: (pt[b, n_pages - 1 - (s * pp + i)], 0, 0))

    row = pl.BlockSpec((1, 1, FOX_HD), lambda b, s, *_: (b, 0, 0))
    per_seq = pl.BlockSpec((1, FOX_HEADS, LANES), lambda b, s, *_: (b, 0, 0))
    scores, page_max, m, s_self = pl.pallas_call(
        functools.partial(_fox_scores_kernel, pp=pp),
        grid_spec=pltpu.PrefetchScalarGridSpec(
            num_scalar_prefetch=1, grid=(nb, n_steps),
            in_specs=[row, row, pl.BlockSpec((1, 1, LANES), lambda b, s, pt: (b, 0, 0))]
            + [k_page(i) for i in range(pp)] + [lf_page(i) for i in range(pp)],
            out_specs=[pl.BlockSpec((1, FOX_HEADS, pp * page), lambda b, s, pt: (b, 0, s)),
                       pl.BlockSpec((1, pp, FOX_HEADS, LANES), lambda b, s, pt: (b, s, 0, 0)), per_seq, per_seq],
            scratch_shapes=[pltpu.VMEM((FOX_HEADS, FOX_DH, page), F32), pltpu.VMEM((FOX_HEADS, page), F32)]),
        out_shape=[jax.ShapeDtypeStruct((nb, FOX_HEADS, n_pages * page), F32),
                   jax.ShapeDtypeStruct((nb, n_pages, FOX_HEADS, LANES), F32), heads_lanes, heads_lanes],
        compiler_params=_cparams(("arbitrary", "arbitrary")),
        name="fox_decode_scores",
    )(page_table, as_row(q), as_row(k_new), lf_row, *([ck] * pp), *([clf] * pp))

    live = jnp.any(page_max[:, :, :, 0] - m[:, None, :, 0] >= -EXP_UNDERFLOW, axis=2)
    count = jnp.sum(live, axis=1).astype(jnp.int32)
    walk = jnp.argsort(jnp.logical_not(live), axis=1, stable=True).astype(jnp.int32)
    pages = jnp.take_along_axis(page_table[:, ::-1], walk, axis=1)
    pp = min(VALUE_PAGES_PER_STEP, n_pages)
    n_steps = n_pages // pp
    is_live = (jnp.arange(n_pages, dtype=jnp.int32)[None, :] < count[:, None]).reshape(nb * n_steps, pp)
    flat = pages.reshape(nb * n_steps, pp)
    last_live = lax.cummax(jnp.where(is_live, jnp.arange(nb * n_steps, dtype=jnp.int32)[:, None], 0), axis=0)
    pages = jnp.take_along_axis(flat, last_live, axis=0).reshape(nb, n_pages)

    def v_page(i):
        return pl.BlockSpec((1, FOX_HEADS, FOX_DH, page), lambda b, s, pg, wk, ct: (pg[b, s * pp + i], 0, 0, 0),
                            pipeline_mode=pl.Buffered(1))

    out = pl.pallas_call(
        functools.partial(_fox_values_kernel, pp=pp),
        grid_spec=pltpu.PrefetchScalarGridSpec(
            num_scalar_prefetch=3, grid=(nb, n_steps),
            in_specs=[pl.BlockSpec((1, FOX_HEADS, n_pages * page), lambda b, s, *_: (b, 0, 0)), per_seq, per_seq, row]
            + [v_page(i) for i in range(pp)],
            out_specs=pl.BlockSpec((1, FOX_HEADS, FOX_DH), lambda b, s, *_: (b, 0, 0)),
            scratch_shapes=[pltpu.VMEM((FOX_HEADS, LANES), F32), pltpu.VMEM((FOX_HEADS, FOX_DH, page), F32)]),
        out_shape=jax.ShapeDtypeStruct((nb, FOX_HEADS, FOX_DH), BF16),
        compiler_params=_cparams(("arbitrary", "arbitrary")),
        name="fox_decode_values",
    )(pages, walk, count, scores, m, s_self, as_row(v_new), *([cv] * pp))
    return out.reshape(nb, FOX_HD)


def _run_group(x, p, cos, sin, w, bm, seq_len, sample):
    rows = x.shape[0]
    qkv_dtype = BF16 if sample is None else F32
    q, k, v, gate = _ret_in(x, w["g_pre_mix"][0], w["w_in"], cos, sin, bm, qkv_dtype)
    if sample is None:
        batch = rows // seq_len
        o, ret_state = _retention_prompt(q, k, v, gate, w["log_gamma"], batch, seq_len)
    else:
        o, ret_state = _retention_step(sample["state_ret"], q, k, v, gate, w["log_gamma"])

    conv_out = []
    ffn_args = lambda i: (w["g_post_mix"][i], w["g_pre_ffn"][i], w["w_up"], w["w_dw"], w["b_dw"],
                          w["w_down"], w["g_post_ffn"][i], p, w["w_ple"], w["w_gate"], bm, i)
    if sample is None:
        x, tail = _ffn(x, o, w["w_out_ret"], *ffn_args(0), seq_len=seq_len)
        conv_out.append(tail[:, SUBLANES - 2:, :])
    else:
        buf = sample["state_conv"][0]
        x, a = _ffn(x, o, w["w_out_ret"], *ffn_args(0), prev=(buf[:, 0], buf[:, 1]))
        conv_out.append(jnp.stack([buf[:, 1], a], axis=1))

    k_new, v_new, lf_new, k_bf, v_bf, q_bf, k_n2, q_n2 = _kvq(
        x, w["g_kv"], w["w_kv"], w["w_f"], w["b_f"], w["g_pre_mix"][1], w["w_q"], bm, per_head_out=sample is None)
    if sample is None:
        cq, ck, c_first, c_last = _cumsum_time(lf_new, batch, seq_len)
        per_seq_max = lambda n2: jnp.max(n2.reshape(batch, -1, SUBLANES, LANES)[:, :, 0, :FOX_HEADS], axis=1)
        qk_bound = jnp.sqrt(per_seq_max(k_n2) * per_seq_max(q_n2)) * NORM_PAD
        edges = lambda e: e.reshape(batch, -1, FOX_HEADS)
        att = _fox_prompt(q_bf, k_bf, v_bf, cq, ck, edges(c_first), edges(c_last), qk_bound, batch, seq_len)
    else:
        att = _fox_decode(q_bf.astype(F32), k_new, v_new, lf_new, sample["cache_k"], sample["cache_v"],
                          sample["cache_logf"], sample["page_table"])

    if sample is None:
        x, tail = _ffn(x, att, w["w_out_fox"], *ffn_args(1), seq_len=seq_len)
        conv_out.append(tail[:, SUBLANES - 2:, :])
    else:
        buf = sample["state_conv"][1]
        x, a = _ffn(x, att, w["w_out_fox"], *ffn_args(1), prev=(buf[:, 0], buf[:, 1]))
        conv_out.append(jnp.stack([buf[:, 1], a], axis=1))
    return x, ret_state, jnp.stack(conv_out), k_new, v_new, lf_new


def kernel(x_prompt, x_sample, state_ret, state_conv, cache_k, cache_v, cache_logf, page_table, p_prompt, p_sample,
           norm_pre_mix, norm_post_mix, norm_pre_ffn, norm_post_ffn, w_in_ret, w_out_ret, norm_kv, w_kvf, b_f,
           w_q_fox, w_out_fox, w_up, w_dw, b_dw, w_down, w_ple, w_ple_gate):
    bp, tp, _ = x_prompt.shape
    nb, ts, _ = x_sample.shape
    assert ts == 1, "the sample group is one new token per sequence"
    past_len = page_table.shape[1] * cache_k.shape[1]
    depth = w_up.shape[0]
    row_vec = lambda a: a.reshape(a.shape[0], 1, a.shape[1])
    w = {
        "g_pre_mix": row_vec(norm_pre_mix), "g_post_mix": row_vec(norm_post_mix),
        "g_pre_ffn": row_vec(norm_pre_ffn), "g_post_ffn": row_vec(norm_post_ffn),
        "g_kv": norm_kv.reshape(1, D_MODEL),
        "w_in": w_in_ret[0].astype(BF16), "w_out_ret": w_out_ret[0].astype(BF16),
        "w_kv": w_kvf[:, :2 * FOX_HD].astype(BF16),
        "w_f": jnp.pad(w_kvf[:, 2 * FOX_HD:], ((0, 0), (0, LANES - FOX_HEADS))).astype(BF16),
        "b_f": jnp.pad(b_f, (0, LANES - FOX_HEADS)).reshape(1, LANES),
        "w_q": w_q_fox[0].astype(BF16), "w_out_fox": w_out_fox[0].astype(BF16),
        "w_up": w_up.astype(BF16), "w_dw": w_dw, "b_dw": row_vec(b_dw), "w_down": w_down.astype(BF16),
        "w_ple": w_ple.astype(BF16), "w_gate": w_ple_gate.astype(BF16),
        "log_gamma": jnp.log1p(-jnp.exp2(-5.0 - jnp.arange(RET_HEADS, dtype=F32))),
    }
    bm = min(ROW_BLOCK, tp)
    cos_p, sin_p = _rope_table(tp, 0)
    y_p, ret_p, conv_p, k_p, v_p, lf_p = _run_group(
        x_prompt.reshape(bp * tp, D_MODEL), p_prompt.reshape(depth, bp * tp, PLE_DIM), cos_p, sin_p, w, bm, tp, None)
    cos_s, sin_s = _rope_table(SUBLANES, past_len)
    cos_s = jnp.broadcast_to(cos_s[:1], (nb, ROPE_HALF))
    sin_s = jnp.broadcast_to(sin_s[:1], (nb, ROPE_HALF))
    sample = {"state_ret": state_ret[0], "state_conv": state_conv, "cache_k": cache_k, "cache_v": cache_v,
              "cache_logf": cache_logf, "page_table": page_table}
    y_s, ret_s, conv_s, k_s, v_s, lf_s = _run_group(
        x_sample.reshape(nb, D_MODEL), p_sample.reshape(depth, nb, PLE_DIM), cos_s, sin_s, w, nb, 1, sample)
    return (y_p.reshape(bp, tp, D_MODEL), y_s.reshape(nb, 1, D_MODEL),
            ret_p[None], ret_s[None], conv_p, conv_s,
            k_p.reshape(bp, tp, FOX_HEADS, FOX_DH), k_s.reshape(nb, 1, FOX_HEADS, FOX_DH),
            v_p.reshape(bp, tp, FOX_HEADS, FOX_DH), v_s.reshape(nb, 1, FOX_HEADS, FOX_DH),
            lf_p.reshape(bp, tp, FOX_HEADS), lf_s.reshape(nb, 1, FOX_HEADS))
```
